```python
import math
import jax, jax.numpy as jnp
from jax import lax
import numpy as np

D_MODEL = 1024
BATCH = 16
SEQ = 2048
DEPTH = 2

HEAD_DIM = 64
N_MEM_HEADS = 4
MEM_WIDTH = N_MEM_HEADS * HEAD_DIM
SEQ_WIDTH = D_MODEL - MEM_WIDTH
N_MEM = 256
ROPE_THETA = 10000.0
DSA_HEADS = SEQ_WIDTH // HEAD_DIM
IDX_HEADS = 8
IDX_DIM = 64
DSA_MAX_TOPK = 256
SPARSE_Q_BLOCK = 32
GLA_HEADS = 4
GLA_DV = SEQ_WIDTH // GLA_HEADS
GLA_DK = GLA_DV // 2
GLA_GATE_RANK = 16
GLA_TAU = 16.0
GLA_CHUNK = 64
N_GROUPS = 4
EXPERTS_PER_GROUP = 8
EXPERT_TOPK = 2
EXPERT_FF = 256
ALPHA = (2 * DEPTH) ** 0.25
BETA = (8 * DEPTH) ** -0.25
LN_EPS = 1e-5
RMS_EPS = 1e-6
N_DSA_LAYERS = (DEPTH + 1) // 2
N_GLA_LAYERS = DEPTH // 2

DSA_SPLITS = [SEQ_WIDTH, SEQ_WIDTH, SEQ_WIDTH, IDX_HEADS * IDX_DIM, IDX_DIM, IDX_HEADS, MEM_WIDTH]
GLA_SPLITS = [GLA_HEADS * GLA_DK, GLA_HEADS * GLA_DK, SEQ_WIDTH, SEQ_WIDTH, GLA_GATE_RANK, MEM_WIDTH]
DSA_IN_WIDTH = sum(DSA_SPLITS)
GLA_IN_WIDTH = sum(GLA_SPLITS)

kernel_name = "hybrid_dsa_gla_memory_hmoe_deepnorm"


def _split(a, sizes):
    return jnp.split(a, list(np.cumsum(sizes)[:-1]), axis=-1)


def layer_norm(x, g, b):
    xf = x.astype(jnp.float32)
    mu = jnp.mean(xf, -1, keepdims=True)
    var = jnp.mean(jnp.square(xf - mu), -1, keepdims=True)
    return ((xf - mu) * lax.rsqrt(var + LN_EPS) * g + b).astype(x.dtype)


def rope_tables(positions, dim):
    inv = ROPE_THETA ** (-jnp.arange(0, dim, 2, dtype=jnp.float32) / dim)
    ang = positions.astype(jnp.float32)[..., None] * inv
    return jnp.cos(ang), jnp.sin(ang)


def apply_rope(x, cos, sin):
    xf = x.astype(jnp.float32)
    x1, x2 = jnp.split(xf, 2, axis=-1)
    return jnp.concatenate([x1 * cos - x2 * sin, x2 * cos + x1 * sin], -1).astype(x.dtype)


def memory_attention(qm, mem, w_kv):
    B, M, _ = mem.shape
    km, vm = jnp.split(mem @ w_kv, 2, axis=-1)
    km = km.reshape(B, M, N_MEM_HEADS, HEAD_DIM)
    vm = vm.reshape(B, M, N_MEM_HEADS, HEAD_DIM)
    logits = jnp.einsum('bshd,bmhd->bhsm', qm, km).astype(jnp.float32) * HEAD_DIM ** -0.5
    p = jax.nn.softmax(logits, axis=-1).astype(vm.dtype)
    return jnp.einsum('bhsm,bmhd->bshd', p, vm)


def sparse_attention(q, k, v, qi, ki, wi, topk):
    B, S, H, dh = q.shape
    nb = S // SPARSE_Q_BLOCK
    spos = jnp.arange(S)
    gather = jax.vmap(lambda a, i: a[i])

    def to_blocks(a):
        return a.reshape((B, nb, SPARSE_Q_BLOCK) + a.shape[2:]).swapaxes(0, 1)

    def block(args):
        start, qb, qib, wib = args
        tpos = start + jnp.arange(SPARSE_Q_BLOCK)
        rel = jax.nn.relu(jnp.einsum('bqhd,bsd->bqhs', qib, ki))
        score = jnp.einsum('bqhs,bqh->bqs', rel, wib).astype(jnp.float32)
        causal = spos[None, :] <= tpos[:, None]
        score = jnp.where(causal[None], score, -jnp.inf)
        _, sel = lax.top_k(score, topk)
        valid = sel <= tpos[None, :, None]
        kg = gather(k, sel)
        vg = gather(v, sel)
        logits = jnp.einsum('bqhd,bqkhd->bqhk', qb, kg).astype(jnp.float32) * dh ** -0.5
        logits = jnp.where(valid[:, :, None, :], logits, -jnp.inf)
        p = jax.nn.softmax(logits, axis=-1).astype(vg.dtype)
        return jnp.einsum('bqhk,bqkhd->bqhd', p, vg)

    starts = jnp.arange(nb, dtype=jnp.int32) * SPARSE_Q_BLOCK
    out = lax.map(block, (starts, to_blocks(q), to_blocks(qi), to_blocks(wi)))
    return out.swapaxes(0, 1).reshape(B, S, H, dh)


def dsa_mixer(x, cos, sin, w_in, idx_k_g, idx_k_b):
    B, S, _ = x.shape
    q, k, v, qi, ki, wi, qm = _split(x @ w_in, DSA_SPLITS)
    c4, s4 = cos[:, :, None, :], sin[:, :, None, :]
    q = apply_rope(q.reshape(B, S, DSA_HEADS, HEAD_DIM), c4, s4)
    k = apply_rope(k.reshape(B, S, DSA_HEADS, HEAD_DIM), c4, s4)
    v = v.reshape(B, S, DSA_HEADS, HEAD_DIM)
    qi = apply_rope(qi.reshape(B, S, IDX_HEADS, IDX_DIM), c4, s4)
    ki = apply_rope(layer_norm(ki, idx_k_g, idx_k_b), cos, sin)
    wi = wi * (IDX_HEADS ** -0.5 * IDX_DIM ** -0.5)
    topk = min(DSA_MAX_TOPK, S // 4)
    out = sparse_attention(q, k, v, qi, ki, wi, topk)
    return out.reshape(B, S, SEQ_WIDTH), qm


def gla_chunked(q, k, v, g):
    B, S, H, dk = q.shape
    dv = v.shape[-1]
    C = GLA_CHUNK
    nc = S // C

    def chunks(a):
        return a.astype(jnp.float32).reshape(B, nc, C, H, a.shape[-1]).transpose(1, 0, 3, 2, 4)

    tri = jnp.tril(jnp.ones((C, C), bool))[:, :, None]

    def step(state, inp):
        qc, kc, vc, gc = inp
        b = jnp.cumsum(gc, axis=2)
        o_inter = jnp.einsum('bhcd,bhde->bhce', qc * jnp.exp(b), state)
        diff = b[:, :, :, None, :] - b[:, :, None, :, :]
        decay = jnp.where(tri, jnp.exp(jnp.where(tri, diff, 0.0)), 0.0)
        attn = jnp.einsum('bhid,bhjd,bhijd->bhij', qc, kc, decay)
        o_intra = jnp.einsum('bhij,bhje->bhie', attn, vc)
        b_last = b[:, :, -1, :]
        state = state * jnp.exp(b_last)[..., None] + jnp.einsum(
            'bhjd,bhje->bhde', kc * jnp.exp(b_last[:, :, None, :] - b), vc)
        return state, o_inter + o_intra

    state0 = jnp.zeros((B, H, dk, dv), jnp.float32)
    _, o = lax.scan(step, state0, (chunks(q), chunks(k), chunks(v), chunks(g)))
    return o.transpose(1, 0, 3, 2, 4).reshape(B, S, H, dv)


def gla_mixer(x, w_in, w_gate, b_gate, norm_g):
    B, S, _ = x.shape
    q, k, v, r, a1, qm = _split(x @ w_in, GLA_SPLITS)
    q = q.reshape(B, S, GLA_HEADS, GLA_DK) * GLA_DK ** -0.5
    k = k.reshape(B, S, GLA_HEADS, GLA_DK)
    v = v.reshape(B, S, GLA_HEADS, GLA_DV)
    g = jax.nn.log_sigmoid((a1 @ w_gate + b_gate).astype(jnp.float32)) / GLA_TAU
    g = g.reshape(B, S, GLA_HEADS, GLA_DK)
    o = gla_chunked(q, k, v, g)
    o = o * lax.rsqrt(jnp.mean(jnp.square(o), -1, keepdims=True) + RMS_EPS) * norm_g
    o = o.astype(x.dtype) * jax.nn.silu(r.reshape(B, S, GLA_HEADS, GLA_DV))
    return o.reshape(B, S, SEQ_WIDTH), qm


def hier_moe(x, w_group, b_group, w_router, b_router, w13, w2):
    B, S, D = x.shape
    t = x.reshape(-1, D)
    glog = (t @ w_group + b_group).astype(jnp.float32)
    gprob = jax.nn.softmax(glog, axis=-1)
    gsel = jnp.argmax(glog, axis=-1)
    pg = jnp.take_along_axis(gprob, gsel[:, None], axis=1)[:, 0]
    elog = (jnp.einsum('td,gde->tge', t, w_router) + b_router).astype(jnp.float32)
    elog = jnp.take_along_axis(elog, gsel[:, None, None], axis=1)[:, 0]
    top_v, top_i = lax.top_k(elog, EXPERT_TOPK)
    wts = jax.nn.softmax(top_v, axis=-1) * pg[:, None]
    emask = jnp.sum(jax.nn.one_hot(top_i, EXPERTS_PER_GROUP, dtype=jnp.float32) * wts[..., None], axis=1)
    comb = jax.nn.one_hot(gsel, N_GROUPS, dtype=jnp.float32)[:, :, None] * emask[:, None, :]
    y = jnp.zeros(t.shape, jnp.float32)
    for gi in range(N_GROUPS):
        h = jnp.einsum('td,edf->tef', t, w13[gi])
        a, u = jnp.split(h, 2, axis=-1)
        act = jax.nn.silu(a) * u * comb[:, gi, :, None].astype(h.dtype)
        y = y + jnp.einsum('tef,efd->td', act, w2[gi]).astype(jnp.float32)
    return y.astype(x.dtype).reshape(B, S, D)


def setup_inputs(seed: int = 0) -> dict:
    key = jax.random.key(seed)
    ks = jax.random.split(key, 24)
    f32 = jnp.float32

    def nrm(k, shape, fan_in, scale=1.0):
        return jax.random.normal(k, shape, f32) * (scale * fan_in ** -0.5)

    def small(k, shape, s):
        return jax.random.normal(k, shape, f32) * s

    G, E, F = N_GROUPS, EXPERTS_PER_GROUP, EXPERT_FF
    offsets = jax.random.randint(ks[2], (BATCH, 1), 0, 4096, dtype=jnp.int32)
    positions = (offsets + jnp.arange(SEQ, dtype=jnp.int32)[None, :]).astype(jnp.int32)
    return {
        "x": jax.random.normal(ks[0], (BATCH, SEQ, D_MODEL), f32),
        "mem": jax.random.normal(ks[1], (BATCH, N_MEM, D_MODEL), f32),
        "positions": positions,
        "dsa_w_in": nrm(ks[3], (N_DSA_LAYERS, D_MODEL, DSA_IN_WIDTH), D_MODEL),
        "dsa_idx_k_g": 1.0 + small(ks[4], (N_DSA_LAYERS, IDX_DIM), 0.02),
        "dsa_idx_k_b": small(ks[5], (N_DSA_LAYERS, IDX_DIM), 0.02),
        "gla_w_in": nrm(ks[6], (N_GLA_LAYERS, D_MODEL, GLA_IN_WIDTH), D_MODEL),
        "gla_w_gate": nrm(ks[7], (N_GLA_LAYERS, GLA_GATE_RANK, GLA_HEADS * GLA_DK), GLA_GATE_RANK),
        "gla_b_gate": small(ks[8], (N_GLA_LAYERS, GLA_HEADS * GLA_DK), 0.02),
        "gla_norm_g": 1.0 + small(ks[9], (N_GLA_LAYERS, GLA_DV), 0.02),
        "w_mem_kv": nrm(ks[10], (DEPTH, D_MODEL, 2 * MEM_WIDTH), D_MODEL),
        "w_out": nrm(ks[11], (DEPTH, D_MODEL, D_MODEL), D_MODEL, BETA),
        "ln1_g": 1.0 + small(ks[12], (DEPTH, D_MODEL), 0.02),
        "ln1_b": small(ks[13], (DEPTH, D_MODEL), 0.02),
        "ln2_g": 1.0 + small(ks[14], (DEPTH, D_MODEL), 0.02),
        "ln2_b": small(ks[15], (DEPTH, D_MODEL), 0.02),
        "moe_w_group": nrm(ks[16], (DEPTH, D_MODEL, G), D_MODEL),
        "moe_b_group": small(ks[17], (DEPTH, G), 0.01),
        "moe_w_router": nrm(ks[18], (DEPTH, G, D_MODEL, E), D_MODEL),
        "moe_b_router": small(ks[19], (DEPTH, G, E), 0.01),
        "moe_w13": nrm(ks[20], (DEPTH, G, E, D_MODEL, 2 * F), D_MODEL),
        "moe_w2": nrm(ks[21], (DEPTH, G, E, F, D_MODEL), F, BETA),
    }


def reference(x, mem, positions, dsa_w_in, dsa_idx_k_g, dsa_idx_k_b, gla_w_in, gla_w_gate,
              gla_b_gate, gla_norm_g, w_mem_kv, w_out, ln1_g, ln1_b, ln2_g, ln2_b,
              moe_w_group, moe_b_group, moe_w_router, moe_b_router, moe_w13, moe_w2):
    B, S, _ = x.shape
    cos, sin = rope_tables(positions, HEAD_DIM)
    ia = 0
    ib = 0
    for i in range(DEPTH):
        if i % 2 == 0:
            seq_out, qm = dsa_mixer(x, cos, sin, dsa_w_in[ia], dsa_idx_k_g[ia], dsa_idx_k_b[ia])
            ia += 1
        else:
            seq_out, qm = gla_mixer(x, gla_w_in[ib], gla_w_gate[ib], gla_b_gate[ib], gla_norm_g[ib])
            ib += 1
        mem_out = memory_attention(qm.reshape(B, S, N_MEM_HEADS, HEAD_DIM), mem, w_mem_kv[i])
        mixed = jnp.concatenate([seq_out, mem_out.reshape(B, S, MEM_WIDTH)], axis=-1) @ w_out[i]
        x = layer_norm(ALPHA * x + mixed, ln1_g[i], ln1_b[i])
        ffn = hier_moe(x, moe_w_group[i], moe_b_group[i], moe_w_router[i], moe_b_router[i],
                       moe_w13[i], moe_w2[i])
        x = layer_norm(ALPHA * x + ffn, ln2_g[i], ln2_b[i])
    return x
```

```python
import functools
import math

import jax
import jax.numpy as jnp
from jax import lax
from jax.experimental import pallas as pl
from jax.experimental.pallas import tpu as pltpu

F32 = jnp.float32
BF16 = jnp.bfloat16
I32 = jnp.int32

LANES = 128
D_MODEL = 1024
HEAD_DIM = 64
N_MEM_HEADS = 4
MEM_WIDTH = N_MEM_HEADS * HEAD_DIM
SEQ_WIDTH = D_MODEL - MEM_WIDTH
ROPE_THETA = 10000.0
DSA_HEADS = SEQ_WIDTH // HEAD_DIM
IDX_HEADS = 8
IDX_DIM = 64
DSA_MAX_TOPK = 256
GLA_HEADS = 4
GLA_DV = SEQ_WIDTH // GLA_HEADS
GLA_DK = GLA_DV // 2
GLA_DKP = 128
GLA_DVP = 256
GLA_GATE_RANK = 16
GLA_TAU = 16.0
GLA_CHUNK = 64
N_GROUPS = 4
EXPERTS_PER_GROUP = 8
N_EXPERTS = N_GROUPS * EXPERTS_PER_GROUP
EXPERT_FF = 256
LN_EPS = 1e-5
RMS_EPS = 1e-6
MASK_BIAS = -1e30
INT_MIN = -2 ** 31
VMEM_LIMIT = 56 * 1024 * 1024


def _dot(a, b):
    return jnp.dot(a, b, preferred_element_type=F32)


def _dot_nt(a, b):
    return lax.dot_general(a, b, (((1,), (1,)), ((), ())), preferred_element_type=F32)


def _dot_tn(a, b):
    return lax.dot_general(a, b, (((0,), (0,)), ((), ())), preferred_element_type=F32)


def _split_bf16(a):
    hi = a.astype(BF16)
    lo = (a - hi.astype(F32)).astype(BF16)
    return hi, lo


def _dot_f32(a, b):
    ah, al = _split_bf16(a)
    bh, bl = _split_bf16(b)
    return _dot(ah, bh) + (_dot(ah, bl) + _dot(al, bh))


def _layer_norm(y, g, b):
    mu = jnp.mean(y, axis=-1, keepdims=True)
    yc = y - mu
    var = jnp.mean(yc * yc, axis=-1, keepdims=True)
    return yc * lax.rsqrt(var + LN_EPS) * g + b


def _params(*sem):
    return pltpu.CompilerParams(dimension_semantics=sem, vmem_limit_bytes=VMEM_LIMIT)


def _rope_fn(cos, sin):
    lane = lax.broadcasted_iota(I32, (1, LANES), 1)
    first_half = (lane % HEAD_DIM) < (HEAD_DIM // 2)

    def rope(a):
        swapped = jnp.where(first_half, pltpu.roll(a, LANES - HEAD_DIM // 2, 1),
                            pltpu.roll(a, HEAD_DIM // 2, 1))
        return a * cos + swapped * sin
    return rope


def _proj_dsa_kernel(x_ref, w_ref, cos_ref, sin_ref, lng_ref, lnb_ref,
                     q_ref, k_ref, v_ref, qi_ref, qm_ref, ki_ref, wi_ref):
    xb = x_ref[...].astype(BF16)
    rope = _rope_fn(cos_ref[...], sin_ref[...])
    segments = ((0, 768, q_ref, True), (768, 768, k_ref, True), (1536, 768, v_ref, False),
                (2304, 512, qi_ref, True), (2816, 256, qm_ref, False))
    for off, width, out_ref, roped in segments:
        acc = _dot(xb, w_ref[:, off:off + width])
        for c in range(width // LANES):
            blk = acc[:, c * LANES:(c + 1) * LANES]
            if roped:
                blk = rope(blk)
            out_ref[:, c * LANES:(c + 1) * LANES] = blk.astype(out_ref.dtype)
    acc = _dot(xb, w_ref[:, 3072:3200])
    lane = lax.broadcasted_iota(I32, (1, LANES), 1)
    is_ki = lane < IDX_DIM
    mu = jnp.sum(jnp.where(is_ki, acc, 0.0), axis=1, keepdims=True) * (1.0 / IDX_DIM)
    d = jnp.where(is_ki, acc - mu, 0.0)
    var = jnp.sum(d * d, axis=1, keepdims=True) * (1.0 / IDX_DIM)
    ki = rope(d * lax.rsqrt(var + LN_EPS) * lng_ref[...] + lnb_ref[...])
    ki_ref[:, :LANES] = ki.astype(BF16)
    ki_ref[:, LANES:] = pltpu.roll(ki, IDX_DIM, 1).astype(BF16)
    wi = pltpu.roll(acc, LANES - IDX_DIM, 1)
    wi_ref[...] = jnp.where(lane < IDX_HEADS, wi * (IDX_HEADS ** -0.5 * IDX_DIM ** -0.5), 0.0)


def _proj_dsa(x2d, w, cosf, sinf, lng, lnb, tm):
    T = x2d.shape[0]
    row = lambda n: pl.BlockSpec((tm, n), lambda i: (i, 0))
    full = lambda a: pl.BlockSpec(a.shape, lambda i: (0,) * a.ndim)
    outs = [(768, BF16), (768, BF16), (768, BF16), (512, BF16), (256, BF16), (256, BF16), (LANES, F32)]
    return pl.pallas_call(
        _proj_dsa_kernel,
        grid=(T // tm,),
        in_specs=[row(D_MODEL), full(w), row(LANES), row(LANES), full(lng), full(lnb)],
        out_specs=[row(n) for n, _ in outs],
        out_shape=[jax.ShapeDtypeStruct((T, n), dt) for n, dt in outs],
        compiler_params=_params("parallel"),
        name="proj_dsa",
    )(x2d, w, cosf, sinf, lng, lnb)


def _proj_gla_kernel(x_ref, w_ref, q_ref, k_ref, v_ref, r_ref, qm_ref, a_ref):
    xb = x_ref[...].astype(BF16)
    segments = ((0, 512, q_ref), (512, 512, k_ref), (1024, 1024, v_ref), (2048, 1024, r_ref),
                (3072, 256, qm_ref), (3328, 128, a_ref))
    for off, width, out_ref in segments:
        out_ref[...] = _dot(xb, w_ref[:, off:off + width]).astype(out_ref.dtype)


def _proj_gla(x2d, w, tm):
    T = x2d.shape[0]
    row = lambda n: pl.BlockSpec((tm, n), lambda i: (i, 0))
    outs = [(512, BF16), (512, BF16), (1024, BF16), (1024, BF16), (256, BF16), (LANES, F32)]
    return pl.pallas_call(
        _proj_gla_kernel,
        grid=(T // tm,),
        in_specs=[row(D_MODEL), pl.BlockSpec(w.shape, lambda i: (0, 0))],
        out_specs=[row(n) for n, _ in outs],
        out_shape=[jax.ShapeDtypeStruct((T, n), dt) for n, dt in outs],
        compiler_params=_params("parallel"),
        name="proj_gla",
    )(x2d, w)


def _idx_kernel(qi_ref, ki_ref, wi_ref, bias_ref, keys_ref, cut_ref, *, tq, tk, nkb, topk, seq):
    qt = pl.program_id(1)
    n_act = qt + 1
    row_g = qt * tq + lax.broadcasted_iota(I32, (tq, tk), 0)
    col_l = lax.broadcasted_iota(I32, (tq, tk), 1)
    wi = wi_ref[0]

    def score_body(kb, carry):
        kblk = ki_ref[0, pl.ds(pl.multiple_of(kb * tk, tk), tk), :]
        k_lo = kblk[:, :LANES]
        k_hi = kblk[:, LANES:]
        sc = jnp.zeros((tq, tk), F32)
        for p in range(IDX_HEADS // 2):
            qp = qi_ref[0, :, p * LANES:(p + 1) * LANES]
            sc = sc + jnp.maximum(_dot_nt(qp, k_lo), 0.0) * wi[:, 2 * p:2 * p + 1]
            sc = sc + jnp.maximum(_dot_nt(qp, k_hi), 0.0) * wi[:, 2 * p + 1:2 * p + 2]
        bits = lax.bitcast_convert_type(sc, I32)
        key = bits ^ ((bits >> 31) & 0x7FFFFFFF)
        keys_ref[kb] = jnp.where(kb * tk + col_l > row_g, INT_MIN, key)
        return carry

    lax.fori_loop(0, n_act, score_body, 0)

    def count(pred):
        def body(kb, acc):
            m = pred(keys_ref[kb], kb).astype(F32)
            r = m[:, :LANES]
            for c in range(1, tk // LANES):
                r = r + m[:, c * LANES:(c + 1) * LANES]
            return acc + r
        acc = lax.fori_loop(0, n_act, body, jnp.zeros((tq, LANES), F32))
        return jnp.sum(acc, axis=1, keepdims=True)

    def bit_body(i, base):
        cand = base ^ lax.shift_left(jnp.int32(1), (31 - i).astype(I32))
        cnt = count(lambda kk, kb: kk >= cand)
        return jnp.where(cnt >= topk, cand, base)

    base = lax.fori_loop(0, 32, bit_body, jnp.full((tq, 1), INT_MIN, I32))

    cnt_gt = count(lambda kk, kb: kk > base)
    cnt_ge = count(lambda kk, kb: kk >= base)
    need = topk - cnt_gt
    tie = (cnt_ge > topk) & (base != INT_MIN)
    cut_ref[...] = jnp.full((tq, 1), seq, I32)

    @pl.when(jnp.max(tie.astype(F32)) > 0.0)
    def _():
        nbits = int(math.log2(seq))

        def idx_body(i, m):
            cand = m | lax.shift_left(jnp.int32(1), (nbits - 1 - i).astype(I32))
            below = count(lambda kk, kb: (kk == base) & (kb * tk + col_l < cand))
            return jnp.where(below < need, cand, m)

        m = lax.fori_loop(0, nbits, idx_body, jnp.zeros((tq, 1), I32))
        cut_ref[...] = jnp.where(tie, m, seq)

    cut = cut_ref[...]
    for kb in range(nkb):
        @pl.when(kb <= qt)
        def _():
            kk = keys_ref[kb]
            sel = ((kk > base) | ((kk == base) & (kb * tk + col_l <= cut))) & (kk != INT_MIN)
            bias_ref[0, 0, kb] = jnp.where(sel, F32(0.0), F32(MASK_BIAS)).astype(BF16)

        @pl.when(kb > qt)
        def _():
            bias_ref[0, 0, kb] = jnp.full((tq, tk), MASK_BIAS, BF16)


def _idx_mask(qi, ki2, wi, topk, tq):
    B, S, _ = qi.shape
    nq = S // tq
    kern = functools.partial(_idx_kernel, tq=tq, tk=tq, nkb=nq, topk=topk, seq=S)
    return pl.pallas_call(
        kern,
        grid=(B, nq),
        in_specs=[pl.BlockSpec((1, tq, IDX_HEADS * IDX_DIM), lambda b, q: (b, q, 0)),
                  pl.BlockSpec((1, S, 2 * LANES), lambda b, q: (b, 0, 0)),
                  pl.BlockSpec((1, tq, LANES), lambda b, q: (b, q, 0))],
        out_specs=pl.BlockSpec((1, 1, nq, tq, tq), lambda b, q: (b, q, 0, 0, 0)),
        out_shape=jax.ShapeDtypeStruct((B, nq, nq, tq, tq), BF16),
        scratch_shapes=[pltpu.VMEM((nq, tq, tq), I32), pltpu.VMEM((tq, 1), I32)],
        compiler_params=_params("parallel", "arbitrary"),
        name="dsa_index_select",
    )(qi, ki2, wi)


def _dsa_attn_kernel(q_ref, k_ref, v_ref, bias_ref, o_ref, m_ref, l_ref, acc_ref, *, tq, tk):
    qt = pl.program_id(2)
    lane = lax.broadcasted_iota(I32, (1, LANES), 1)
    low = lane < HEAD_DIM
    q = q_ref[0]
    zero = jnp.zeros_like(q)
    q_heads = (jnp.where(low, q, zero), jnp.where(low, zero, q))
    m_ref[...] = jnp.full(m_ref.shape, -jnp.inf, F32)
    l_ref[...] = jnp.zeros(l_ref.shape, F32)
    acc_ref[...] = jnp.zeros(acc_ref.shape, F32)
    scale = HEAD_DIM ** -0.5

    def body(kb, carry):
        rows = pl.ds(pl.multiple_of(kb * tk, tk), tk)
        kblk = k_ref[0, rows, :]
        vblk = v_ref[0, rows, :]
        bias = bias_ref[0, 0, kb].astype(F32)
        for h in range(2):
            s = _dot_nt(q_heads[h], kblk) * scale + bias
            m_old = m_ref[h]
            m_new = jnp.maximum(m_old, jnp.max(s, axis=1, keepdims=True))
            alpha = jnp.exp(m_old - m_new)
            p = jnp.exp(s - m_new)
            l_ref[h] = alpha * l_ref[h] + jnp.sum(p, axis=1, keepdims=True)
            acc_ref[h] = alpha * acc_ref[h] + _dot(p.astype(BF16), vblk)
            m_ref[h] = m_new
        return carry

    lax.fori_loop(0, qt + 1, body, 0)
    o0 = acc_ref[0] / l_ref[0]
    o1 = acc_ref[1] / l_ref[1]
    o_ref[0] = jnp.where(low, o0, o1).astype(o_ref.dtype)


def _dsa_attn(q, k, v, bias, tq):
    B, S, W = q.shape
    nq = S // tq
    npair = W // LANES
    kern = functools.partial(_dsa_attn_kernel, tq=tq, tk=tq)
    return pl.pallas_call(
        kern,
        grid=(B, npair, nq),
        in_specs=[pl.BlockSpec((1, tq, LANES), lambda b, h, i: (b, i, h)),
                  pl.BlockSpec((1, S, LANES), lambda b, h, i: (b, 0, h)),
                  pl.BlockSpec((1, S, LANES), lambda b, h, i: (b, 0, h)),
                  pl.BlockSpec((1, 1, nq, tq, tq), lambda b, h, i: (b, i, 0, 0, 0))],
        out_specs=pl.BlockSpec((1, tq, LANES), lambda b, h, i: (b, i, h)),
        out_shape=jax.ShapeDtypeStruct((B, S, W), BF16),
        scratch_shapes=[pltpu.VMEM((2, tq, 1), F32), pltpu.VMEM((2, tq, 1), F32),
                        pltpu.VMEM((2, tq, LANES), F32)],
        compiler_params=_params("parallel", "parallel", "arbitrary"),
        name="dsa_attention",
    )(q, k, v, bias)


def _mem_attn_kernel(qm_ref, mem_ref, wkv_ref, o_ref, k_scr, v_scr):
    @pl.when(pl.program_id(1) == 0)
    def _():
        kv = _dot(mem_ref[0].astype(BF16), wkv_ref[...])
        k_scr[...] = kv[:, :MEM_WIDTH].astype(BF16)
        v_scr[...] = kv[:, MEM_WIDTH:].astype(BF16)

    lane = lax.broadcasted_iota(I32, (1, LANES), 1)
    low = lane < HEAD_DIM
    scale = HEAD_DIM ** -0.5
    for pair in range(MEM_WIDTH // LANES):
        cols = slice(pair * LANES, (pair + 1) * LANES)
        q = qm_ref[0, :, cols]
        zero = jnp.zeros_like(q)
        kp = k_scr[:, cols]
        vp = v_scr[:, cols]
        outs = []
        for qh in (jnp.where(low, q, zero), jnp.where(low, zero, q)):
            s = _dot_nt(qh, kp) * scale
            s = s - jnp.max(s, axis=1, keepdims=True)
            p = jnp.exp(s)
            p = p / jnp.sum(p, axis=1, keepdims=True)
            outs.append(_dot(p.astype(BF16), vp))
        o_ref[0, :, cols] = jnp.where(low, outs[0], outs[1]).astype(o_ref.dtype)


def _mem_attn(qm, mem, wkv, tq):
    B, S, _ = qm.shape
    M = mem.shape[1]
    return pl.pallas_call(
        _mem_attn_kernel,
        grid=(B, S // tq),
        in_specs=[pl.BlockSpec((1, tq, MEM_WIDTH), lambda b, i: (b, i, 0)),
                  pl.BlockSpec((1, M, D_MODEL), lambda b, i: (b, 0, 0)),
                  pl.BlockSpec(wkv.shape, lambda b, i: (0, 0))],
        out_specs=pl.BlockSpec((1, tq, MEM_WIDTH), lambda b, i: (b, i, 0)),
        out_shape=jax.ShapeDtypeStruct((B, S, MEM_WIDTH), BF16),
        scratch_shapes=[pltpu.VMEM((M, MEM_WIDTH), BF16), pltpu.VMEM((M, MEM_WIDTH), BF16)],
        compiler_params=_params("parallel", "arbitrary"),
        name="memory_attention",
    )(qm, mem, wkv)


def _gla_kernel(q_ref, k_ref, v_ref, r_ref, a_ref, wg_ref, bg_ref, ng_ref, o_ref, state_ref, *, nchunk):
    C = GLA_CHUNK
    state_ref[...] = jnp.zeros(state_ref.shape, F32)
    ri = lax.broadcasted_iota(I32, (C, C), 0)
    ci = lax.broadcasted_iota(I32, (C, C), 1)
    causal = ri >= ci
    tri = causal.astype(BF16)
    wg = wg_ref[0]
    bg = bg_ref[0]
    ng = ng_ref[...]

    def body(c, carry):
        rows = pl.ds(pl.multiple_of(c * C, C), C)
        z = _dot_f32(a_ref[0, rows, :], wg) + bg
        g = (jnp.minimum(z, 0.0) - jnp.log1p(jnp.exp(-jnp.abs(z)))) * (1.0 / GLA_TAU)
        g_hi, g_lo = _split_bf16(g)
        g_lo2 = (g - g_hi.astype(F32) - g_lo.astype(F32)).astype(BF16)
        b = _dot(tri, g_hi) + (_dot(tri, g_lo) + _dot(tri, g_lo2))
        b_last = b[C - 1:C, :]
        q = q_ref[0, rows, :].astype(F32) * (GLA_DK ** -0.5)
        k = k_ref[0, rows, :].astype(F32)
        v = v_ref[0, rows, :]
        qb = (q * jnp.exp(b)).astype(BF16)
        kb = (k * jnp.exp(-b)).astype(BF16)
        kd = (k * jnp.exp(b_last - b)).astype(BF16)
        st = state_ref[...]
        o = _dot_nt(qb, st.astype(BF16))
        attn = jnp.where(causal, _dot_nt(qb, kb), 0.0)
        o = o + _dot(attn.astype(BF16), v)
        state_ref[...] = st * jnp.exp(b_last) + _dot_tn(v, kd)
        ms = jnp.sum(o * o, axis=1, keepdims=True) * (1.0 / GLA_DV)
        o = o * lax.rsqrt(ms + RMS_EPS) * ng
        r = r_ref[0, rows, :].astype(F32)
        o_ref[0, rows, :] = (o * (r * jax.nn.sigmoid(r))).astype(o_ref.dtype)
        return carry

    lax.fori_loop(0, nchunk, body, 0)


def _gla(q, k, v, r, a1, wg, bg, ng):
    B, S, _ = q.shape
    kern = functools.partial(_gla_kernel, nchunk=S // GLA_CHUNK)
    kspec = pl.BlockSpec((1, S, GLA_DKP), lambda b, h: (b, 0, h))
    vspec = pl.BlockSpec((1, S, GLA_DVP), lambda b, h: (b, 0, h))
    return pl.pallas_call(
        kern,
        grid=(B, GLA_HEADS),
        in_specs=[kspec, kspec, vspec, vspec,
                  pl.BlockSpec((1, S, LANES), lambda b, h: (b, 0, 0)),
                  pl.BlockSpec((1, LANES, GLA_DKP), lambda b, h: (h, 0, 0)),
                  pl.BlockSpec((1, 1, GLA_DKP), lambda b, h: (h, 0, 0)),
                  pl.BlockSpec((1, GLA_DVP), lambda b, h: (0, 0))],
        out_specs=vspec,
        out_shape=jax.ShapeDtypeStruct((B, S, GLA_HEADS * GLA_DVP), BF16),
        scratch_shapes=[pltpu.VMEM((GLA_DVP, GLA_DKP), F32)],
        compiler_params=_params("parallel", "parallel"),
        name="gla",
    )(q, k, v, r, a1, wg, bg, ng)


def _mix_router_kernel(seq_ref, memo_ref, wa_ref, wb_ref, x_ref, g_ref, b_ref, wr_ref, br_ref,
                       x1_ref, comb_ref, *, alpha):
    mixed = _dot(seq_ref[...], wa_ref[...]) + _dot(memo_ref[...], wb_ref[...])
    x1 = _layer_norm(alpha * x_ref[...] + mixed, g_ref[...], b_ref[...])
    x1_ref[...] = x1
    logits = _dot_f32(x1, wr_ref[...]) + br_ref[...]
    lane = lax.broadcasted_iota(I32, logits.shape, 1)
    neg = -jnp.inf
    glog = jnp.where(lane < N_GROUPS, logits, neg)
    gmax = jnp.max(glog, axis=1, keepdims=True)
    gsel = jnp.min(jnp.where(glog == gmax, lane, LANES), axis=1, keepdims=True)
    pg = 1.0 / jnp.sum(jnp.exp(glog - gmax), axis=1, keepdims=True)
    lo = N_GROUPS + EXPERTS_PER_GROUP * gsel
    elog = jnp.where((lane >= lo) & (lane < lo + EXPERTS_PER_GROUP), logits, neg)
    v1 = jnp.max(elog, axis=1, keepdims=True)
    i1 = jnp.min(jnp.where(elog == v1, lane, LANES), axis=1, keepdims=True)
    elog2 = jnp.where(lane == i1, neg, elog)
    v2 = jnp.max(elog2, axis=1, keepdims=True)
    i2 = jnp.min(jnp.where(elog2 == v2, lane, LANES), axis=1, keepdims=True)
    e2 = jnp.exp(v2 - v1)
    den = 1.0 + e2
    comb_ref[...] = (jnp.where(lane == i1, pg / den, 0.0) + jnp.where(lane == i2, pg * e2 / den, 0.0))


def _mix_router(seq, memo, wa, wb, x2d, g, b, wr, br, alpha, tm):
    T = x2d.shape[0]
    row = lambda n: pl.BlockSpec((tm, n), lambda i: (i, 0))
    full = lambda a: pl.BlockSpec(a.shape, lambda i: (0,) * a.ndim)
    kern = functools.partial(_mix_router_kernel, alpha=alpha)
    return pl.pallas_call(
        kern,
        grid=(T // tm,),
        in_specs=[row(seq.shape[1]), row(MEM_WIDTH), full(wa), full(wb), row(D_MODEL), full(g), full(b),
                  full(wr), full(br)],
        out_specs=[row(D_MODEL), row(LANES)],
        out_shape=[jax.ShapeDtypeStruct((T, D_MODEL), F32), jax.ShapeDtypeStruct((T, LANES), F32)],
        compiler_params=_params("parallel"),
        name="outproj_ln_router",
    )(seq, memo, wa, wb, x2d, g, b, wr, br)


def _moe_kernel(x_ref, comb_ref, w13_ref, w2_ref, g_ref, b_ref, o_ref, xb_ref, acc_ref, *, alpha):
    e = pl.program_id(1)

    @pl.when(e == 0)
    def _():
        xb_ref[...] = x_ref[...].astype(BF16)
        acc_ref[...] = jnp.zeros(acc_ref.shape, F32)

    h = _dot(xb_ref[...], w13_ref[0])
    a = h[:, :EXPERT_FF]
    u = h[:, EXPERT_FF:]
    comb = comb_ref[...]
    lane = lax.broadcasted_iota(I32, comb.shape, 1)
    c = jnp.sum(jnp.where(lane == N_GROUPS + e, comb, 0.0), axis=1, keepdims=True)
    act = (a * jax.nn.sigmoid(a)) * u * c
    acc_ref[...] += _dot(act.astype(BF16), w2_ref[0])

    @pl.when(e == pl.num_programs(1) - 1)
    def _():
        o_ref[...] = _layer_norm(alpha * x_ref[...] + acc_ref[...], g_ref[...], b_ref[...])


def _moe(x1, comb, w13, w2, g, b, alpha, tm):
    T = x1.shape[0]
    kern = functools.partial(_moe_kernel, alpha=alpha)
    return pl.pallas_call(
        kern,
        grid=(T // tm, N_EXPERTS),
        in_specs=[pl.BlockSpec((tm, D_MODEL), lambda i, e: (i, 0)),
                  pl.BlockSpec((tm, LANES), lambda i, e: (i, 0)),
                  pl.BlockSpec((1, D_MODEL, 2 * EXPERT_FF), lambda i, e: (e, 0, 0)),
                  pl.BlockSpec((1, EXPERT_FF, D_MODEL), lambda i, e: (e, 0, 0)),
                  pl.BlockSpec((1, D_MODEL), lambda i, e: (0, 0)),
                  pl.BlockSpec((1, D_MODEL), lambda i, e: (0, 0))],
        out_specs=pl.BlockSpec((tm, D_MODEL), lambda i, e: (i, 0)),
        out_shape=jax.ShapeDtypeStruct((T, D_MODEL), F32),
        scratch_shapes=[pltpu.VMEM((tm, D_MODEL), BF16), pltpu.VMEM((tm, D_MODEL), F32)],
        compiler_params=_params("parallel", "arbitrary"),
        name="moe_experts_ln",
    )(x1, comb, w13, w2, g, b)


def _pad_cols(a, width):
    return jnp.pad(a, ((0, 0), (0, width - a.shape[1])))


def _pad_heads(w, heads, dim, dim_pad):
    rows = w.shape[0]
    return jnp.pad(w.reshape(rows, heads, dim), ((0, 0), (0, 0), (0, dim_pad - dim))).reshape(rows, heads * dim_pad)


def _tile(n, pref):
    t = pref
    while n % t:
        t //= 2
    return t


def kernel(x, mem, positions, dsa_w_in, dsa_idx_k_g, dsa_idx_k_b, gla_w_in, gla_w_gate, gla_b_gate, gla_norm_g,
           w_mem_kv, w_out, ln1_g, ln1_b, ln2_g, ln2_b, moe_w_group, moe_b_group, moe_w_router, moe_b_router,
           moe_w13, moe_w2):
    B, S, D = x.shape
    T = B * S
    depth = w_out.shape[0]
    alpha = (2 * depth) ** 0.25
    tm = _tile(T, 512)
    tq = _tile(S, 256)
    topk = min(DSA_MAX_TOPK, S // 4)

    inv = ROPE_THETA ** (-jnp.arange(0, HEAD_DIM, 2, dtype=F32) / HEAD_DIM)
    ang = positions.astype(F32).reshape(T, 1) * inv
    cos, sin = jnp.cos(ang), jnp.sin(ang)
    cosf = jnp.concatenate([cos, cos, cos, cos], axis=1)
    sinf = jnp.concatenate([-sin, sin, -sin, sin], axis=1)

    xc = x.reshape(T, D)
    ia = ib = 0
    for i in range(depth):
        if i % 2 == 0:
            w = dsa_w_in[ia]
            wq, wk, wv, wqi, wki, wwi, wqm = jnp.split(w, [768, 1536, 2304, 2816, 2880, 2888], axis=1)
            w_all = jnp.concatenate([wq, wk, wv, wqi, wqm, wki, _pad_cols(wwi, LANES - IDX_DIM)], axis=1).astype(BF16)
            lng = _pad_cols(dsa_idx_k_g[ia][None, :], LANES)
            lnb = _pad_cols(dsa_idx_k_b[ia][None, :], LANES)
            q, k, v, qi, qm, ki2, wi = _proj_dsa(xc, w_all, cosf, sinf, lng, lnb, tm)
            r3 = lambda a: a.reshape(B, S, a.shape[1])
            bias = _idx_mask(r3(qi), r3(ki2), r3(wi), topk, tq)
            seq = _dsa_attn(r3(q), r3(k), r3(v), bias, tq).reshape(T, SEQ_WIDTH)
            wa = w_out[i][:SEQ_WIDTH].astype(BF16)
            ia += 1
        else:
            w = gla_w_in[ib]
            wq, wk, wv, wr_, wa1, wqm = jnp.split(w, [384, 768, 1536, 2304, 2320], axis=1)
            w_all = jnp.concatenate([
                _pad_heads(wq, GLA_HEADS, GLA_DK, GLA_DKP), _pad_heads(wk, GLA_HEADS, GLA_DK, GLA_DKP),
                _pad_heads(wv, GLA_HEADS, GLA_DV, GLA_DVP), _pad_heads(wr_, GLA_HEADS, GLA_DV, GLA_DVP),
                wqm, _pad_cols(wa1, LANES)], axis=1).astype(BF16)
            q, k, v, r, qm, a1 = _proj_gla(xc, w_all, tm)
            r3 = lambda a: a.reshape(B, S, a.shape[1])
            wg = _pad_heads(gla_w_gate[ib], GLA_HEADS, GLA_DK, GLA_DKP)
            wg = jnp.pad(wg, ((0, LANES - GLA_GATE_RANK), (0, 0)))
            wg = wg.reshape(LANES, GLA_HEADS, GLA_DKP).transpose(1, 0, 2)
            bg = _pad_heads(gla_b_gate[ib][None, :], GLA_HEADS, GLA_DK, GLA_DKP).reshape(GLA_HEADS, 1, GLA_DKP)
            ng = _pad_cols(gla_norm_g[ib][None, :], GLA_DVP)
            seq = _gla(r3(q), r3(k), r3(v), r3(r), r3(a1), wg, bg, ng).reshape(T, GLA_HEADS * GLA_DVP)
            wa = w_out[i][:SEQ_WIDTH].reshape(GLA_HEADS, GLA_DV, D)
            wa = jnp.pad(wa, ((0, 0), (0, GLA_DVP - GLA_DV), (0, 0))).reshape(GLA_HEADS * GLA_DVP, D).astype(BF16)
            ib += 1
        memo = _mem_attn(qm.reshape(B, S, MEM_WIDTH), mem, w_mem_kv[i].astype(BF16), tq).reshape(T, MEM_WIDTH)
        wb = w_out[i][SEQ_WIDTH:].astype(BF16)
        wr = jnp.concatenate([moe_w_group[i], moe_w_router[i].transpose(1, 0, 2).reshape(D, N_EXPERTS)], axis=1)
        wr = _pad_cols(wr, LANES)
        br = _pad_cols(jnp.concatenate([moe_b_group[i], moe_b_router[i].reshape(-1)])[None, :], LANES)
        x1, comb = _mix_router(seq, memo, wa, wb, xc, ln1_g[i][None, :], ln1_b[i][None, :], wr, br, alpha, tm)
        w13 = moe_w13[i].reshape(N_EXPERTS, D, 2 * EXPERT_FF).astype(BF16)
        w2 = moe_w2[i].reshape(N_EXPERTS, EXPERT_FF, D).astype(BF16)
        xc = _moe(x1, comb, w13, w2, ln2_g[i][None, :], ln2_b[i][None, :], alpha, _tile(T, 1024))
    return xc.reshape(B, S, D)
```

```python
import functools
import math

import jax
import jax.numpy as jnp
from jax import lax
from jax.experimental import pallas as pl
from jax.experimental.pallas import tpu as pltpu

F32 = jnp.float32
BF16 = jnp.bfloat16
I32 = jnp.int32

LANES = 128
D_MODEL = 1024
HEAD_DIM = 64
N_MEM_HEADS = 4
MEM_WIDTH = N_MEM_HEADS * HEAD_DIM
SEQ_WIDTH = D_MODEL - MEM_WIDTH
ROPE_THETA = 10000.0
DSA_HEADS = SEQ_WIDTH // HEAD_DIM
IDX_HEADS = 8
IDX_DIM = 64
DSA_MAX_TOPK = 256
GLA_HEADS = 4
GLA_DV = SEQ_WIDTH // GLA_HEADS
GLA_DK = GLA_DV // 2
GLA_DKP = 128
GLA_DVP = 256
GLA_GATE_RANK = 16
GLA_TAU = 16.0
GLA_CHUNK = 64
N_GROUPS = 4
EXPERTS_PER_GROUP = 8
N_EXPERTS = N_GROUPS * EXPERTS_PER_GROUP
EXPERT_FF = 256
LN_EPS = 1e-5
RMS_EPS = 1e-6
MASK_BIAS = -1e30
INT_MIN = -2 ** 31
VMEM_LIMIT = 56 * 1024 * 1024


def _dot(a, b):
    return jnp.dot(a, b, preferred_element_type=F32)


def _dot_nt(a, b):
    return lax.dot_general(a, b, (((1,), (1,)), ((), ())), preferred_element_type=F32)


def _dot_tn(a, b):
    return lax.dot_general(a, b, (((0,), (0,)), ((), ())), preferred_element_type=F32)


def _split_bf16(a):
    hi = a.astype(BF16)
    lo = (a - hi.astype(F32)).astype(BF16)
    return hi, lo


def _dot_f32(a, b):
    ah, al = _split_bf16(a)
    bh, bl = _split_bf16(b)
    return _dot(ah, bh) + (_dot(ah, bl) + _dot(al, bh))


def _layer_norm(y, g, b):
    mu = jnp.mean(y, axis=-1, keepdims=True)
    yc = y - mu
    var = jnp.mean(yc * yc, axis=-1, keepdims=True)
    return yc * lax.rsqrt(var + LN_EPS) * g + b


def _params(*sem):
    return pltpu.CompilerParams(dimension_semantics=sem, vmem_limit_bytes=VMEM_LIMIT)


def _rope_fn(cos, sin, axis):
    shape = (1, LANES) if axis == 1 else (LANES, 1)
    pos = lax.broadcasted_iota(I32, shape, axis)
    first_half = (pos % HEAD_DIM) < (HEAD_DIM // 2)

    def rope(a):
        swapped = jnp.where(first_half, pltpu.roll(a, LANES - HEAD_DIM // 2, axis),
                            pltpu.roll(a, HEAD_DIM // 2, axis))
        return a * cos + swapped * sin
    return rope


def _proj_dsa_kernel(x_ref, wa_ref, wb_ref, cos_ref, sin_ref, cost_ref, sint_ref, lng_ref, lnb_ref,
                     k_ref, qm_ref, ki_ref, qt_ref, vt_ref, qit_ref, wit_ref):
    xb = x_ref[...].astype(BF16)
    tm = xb.shape[0]
    rope = _rope_fn(cos_ref[...], sin_ref[...], 1)
    rope_t = _rope_fn(cost_ref[...], sint_ref[...], 0)
    acc = _dot(xb, wa_ref[:, 0:SEQ_WIDTH])
    for c in range(SEQ_WIDTH // LANES):
        k_ref[:, c * LANES:(c + 1) * LANES] = rope(acc[:, c * LANES:(c + 1) * LANES]).astype(BF16)
    qm_ref[...] = _dot(xb, wa_ref[:, SEQ_WIDTH:SEQ_WIDTH + MEM_WIDTH]).astype(BF16)
    acc = _dot(xb, wa_ref[:, D_MODEL:D_MODEL + LANES])
    lane = lax.broadcasted_iota(I32, (1, LANES), 1)
    is_ki = lane < IDX_DIM
    mu = jnp.sum(jnp.where(is_ki, acc, 0.0), axis=1, keepdims=True) * (1.0 / IDX_DIM)
    d = jnp.where(is_ki, acc - mu, 0.0)
    var = jnp.sum(d * d, axis=1, keepdims=True) * (1.0 / IDX_DIM)
    ki_ref[...] = rope(d * lax.rsqrt(var + LN_EPS) * lng_ref[...] + lnb_ref[...]).astype(BF16)

    def store_t(out_ref, r, val):
        nblk, _, tb = out_ref.shape
        rows = val.shape[0]
        for j in range(nblk):
            out_ref[j, r * rows:(r + 1) * rows, :] = val[:, j * tb:(j + 1) * tb].astype(out_ref.dtype)

    q_scale = HEAD_DIM ** -0.5 * math.log2(math.e)
    for r in range(SEQ_WIDTH // LANES):
        acc = _dot_nt(wb_ref[r * LANES:(r + 1) * LANES, :], xb)
        store_t(qt_ref, r, rope_t(acc) * q_scale)
    off = SEQ_WIDTH
    for r in range(SEQ_WIDTH // LANES):
        store_t(vt_ref, r, _dot_nt(wb_ref[off + r * LANES:off + (r + 1) * LANES, :], xb))
    off = 2 * SEQ_WIDTH
    for r in range(IDX_HEADS * IDX_DIM // LANES):
        store_t(qit_ref, r, rope_t(_dot_nt(wb_ref[off + r * LANES:off + (r + 1) * LANES, :], xb)))
    off = 2 * SEQ_WIDTH + IDX_HEADS * IDX_DIM
    wi = _dot_nt(wb_ref[off:off + 16, :], xb)[:IDX_HEADS, :] * (IDX_HEADS ** -0.5 * IDX_DIM ** -0.5)
    store_t(wit_ref, 0, wi)


def _proj_dsa(x2d, wa, wb, cosf, sinf, cost, sint, lng, lnb, tm, tq_att, tq_idx):
    T = x2d.shape[0]
    row = lambda n: pl.BlockSpec((tm, n), lambda i: (i, 0))
    col = pl.BlockSpec((LANES, tm), lambda i: (0, i))
    full = lambda a: pl.BlockSpec(a.shape, lambda i: (0,) * a.ndim)
    featmaj = lambda n, tb: pl.BlockSpec((tm // tb, n, tb), lambda i: (i, 0, 0))
    fshape = lambda n, tb, dt: jax.ShapeDtypeStruct((T // tb, n, tb), dt)
    return pl.pallas_call(
        _proj_dsa_kernel,
        grid=(T // tm,),
        in_specs=[row(D_MODEL), full(wa), full(wb), row(LANES), row(LANES), col, col, full(lng), full(lnb)],
        out_specs=[row(SEQ_WIDTH), row(MEM_WIDTH), row(LANES),
                   featmaj(SEQ_WIDTH, tq_att), featmaj(SEQ_WIDTH, tq_att),
                   featmaj(IDX_HEADS * IDX_DIM, tq_idx), featmaj(IDX_HEADS, tq_idx)],
        out_shape=[jax.ShapeDtypeStruct((T, SEQ_WIDTH), BF16), jax.ShapeDtypeStruct((T, MEM_WIDTH), BF16),
                   jax.ShapeDtypeStruct((T, LANES), BF16),
                   fshape(SEQ_WIDTH, tq_att, BF16), fshape(SEQ_WIDTH, tq_att, BF16),
                   fshape(IDX_HEADS * IDX_DIM, tq_idx, BF16), fshape(IDX_HEADS, tq_idx, F32)],
        compiler_params=_params("parallel"),
        name="proj_dsa",
    )(x2d, wa, wb, cosf, sinf, cost, sint, lng, lnb)


def _proj_gla_kernel(x_ref, w_ref, q_ref, k_ref, v_ref, r_ref, qm_ref, a_ref):
    xb = x_ref[...].astype(BF16)
    segments = ((0, 512, q_ref), (512, 512, k_ref), (1024, 1024, v_ref), (2048, 1024, r_ref),
                (3072, 256, qm_ref), (3328, 128, a_ref))
    for off, width, out_ref in segments:
        out_ref[...] = _dot(xb, w_ref[:, off:off + width]).astype(out_ref.dtype)


def _proj_gla(x2d, w, tm):
    T = x2d.shape[0]
    row = lambda n: pl.BlockSpec((tm, n), lambda i: (i, 0))
    outs = [(512, BF16), (512, BF16), (1024, BF16), (1024, BF16), (256, BF16), (LANES, F32)]
    return pl.pallas_call(
        _proj_gla_kernel,
        grid=(T // tm,),
        in_specs=[row(D_MODEL), pl.BlockSpec(w.shape, lambda i: (0, 0))],
        out_specs=[row(n) for n, _ in outs],
        out_shape=[jax.ShapeDtypeStruct((T, n), dt) for n, dt in outs],
        compiler_params=_params("parallel"),
        name="proj_gla",
    )(x2d, w)


def _idx_kernel(qi_ref, ki_ref, wi_ref, bias_ref, keys_ref, cut_ref, *, tq, tk, nkb, topk, seq):
    qt = pl.program_id(1)
    n_act = qt + 1
    key_l = lax.broadcasted_iota(I32, (tk, tq), 0)
    qry_g = qt * tq + lax.broadcasted_iota(I32, (tk, tq), 1)
    wi = wi_ref[0]

    def score_body(kb, carry):
        kblk = ki_ref[0, pl.ds(pl.multiple_of(kb * tk, tk), tk), :][:, :IDX_DIM]
        sc = jnp.zeros((tk, tq), F32)
        for h in range(IDX_HEADS):
            qh = qi_ref[0, h * IDX_DIM:(h + 1) * IDX_DIM, :]
            sc = sc + jnp.maximum(_dot(kblk, qh), 0.0) * wi[h:h + 1, :]
        bits = lax.bitcast_convert_type(sc, I32)
        key = bits ^ ((bits >> 31) & 0x7FFFFFFF)
        keys_ref[kb] = jnp.where(kb * tk + key_l > qry_g, INT_MIN, key)
        return carry

    lax.fori_loop(0, n_act, score_body, 0)

    def count(pred):
        def body(kb, acc):
            m = pred(keys_ref[kb], kb).astype(F32)
            parts = [m[j * 8:(j + 1) * 8] for j in range(4)]
            for j in range(4, tk // 8):
                parts[j % 4] = parts[j % 4] + m[j * 8:(j + 1) * 8]
            return acc + ((parts[0] + parts[1]) + (parts[2] + parts[3]))
        acc = lax.fori_loop(0, n_act, body, jnp.zeros((8, tq), F32))
        return jnp.sum(acc, axis=0, keepdims=True)

    def bit_body(i, base):
        cand = base ^ lax.shift_left(jnp.int32(1), (31 - i).astype(I32))
        cnt = count(lambda kk, kb: kk >= cand)
        return jnp.where(cnt >= topk, cand, base)

    base = lax.fori_loop(0, 32, bit_body, jnp.full((1, tq), INT_MIN, I32))

    cnt_gt = count(lambda kk, kb: kk > base)
    cnt_ge = count(lambda kk, kb: kk >= base)
    need = topk - cnt_gt
    tie = (cnt_ge > topk) & (base != INT_MIN)
    cut_ref[...] = jnp.full((1, tq), seq, I32)

    @pl.when(jnp.max(tie.astype(F32)) > 0.0)
    def _():
        nbits = int(math.log2(seq))

        def idx_body(i, m):
            cand = m | lax.shift_left(jnp.int32(1), (nbits - 1 - i).astype(I32))
            below = count(lambda kk, kb: (kk == base) & (kb * tk + key_l < cand))
            return jnp.where(below < need, cand, m)

        m = lax.fori_loop(0, nbits, idx_body, jnp.zeros((1, tq), I32))
        cut_ref[...] = jnp.where(tie, m, seq)

    cut = cut_ref[...]
    for kb in range(nkb):
        @pl.when(kb <= qt)
        def _():
            kk = keys_ref[kb]
            sel = ((kk > base) | ((kk == base) & (kb * tk + key_l <= cut))) & (kk != INT_MIN)
            bias_ref[0, 0, kb] = jnp.where(sel, F32(0.0), F32(MASK_BIAS)).astype(BF16)

        @pl.when(kb > qt)
        def _():
            bias_ref[0, 0, kb] = jnp.full((tk, tq), MASK_BIAS, BF16)


def _idx_mask(qit, ki, wit, B, S, topk, tq):
    nq = S // tq
    kern = functools.partial(_idx_kernel, tq=tq, tk=tq, nkb=nq, topk=topk, seq=S)
    return pl.pallas_call(
        kern,
        grid=(B, nq),
        in_specs=[pl.BlockSpec((1, IDX_HEADS * IDX_DIM, tq), lambda b, q: (b * nq + q, 0, 0)),
                  pl.BlockSpec((1, S, LANES), lambda b, q: (b, 0, 0)),
                  pl.BlockSpec((1, IDX_HEADS, tq), lambda b, q: (b * nq + q, 0, 0))],
        out_specs=pl.BlockSpec((1, 1, nq, tq, tq), lambda b, q: (b, q, 0, 0, 0)),
        out_shape=jax.ShapeDtypeStruct((B, nq, nq, tq, tq), BF16),
        scratch_shapes=[pltpu.VMEM((nq, tq, tq), I32), pltpu.VMEM((1, tq), I32)],
        compiler_params=_params("parallel", "arbitrary"),
        name="dsa_index_select",
    )(qit, ki, wit)


def _dsa_attn_kernel(q_ref, k_ref, v_ref, bias_ref, o_ref, m_ref, acc_ref, *, tq, tk, sub):
    qt = pl.program_id(2)
    nsub = tq // sub
    low = lax.broadcasted_iota(I32, (LANES, 1), 0) < HEAD_DIM
    q = q_ref[0]
    zero = jnp.zeros_like(q)
    q_heads = (jnp.where(low, q, zero), jnp.where(low, zero, q))
    m_ref[...] = jnp.full(m_ref.shape, -jnp.inf, F32)
    acc_ref[...] = jnp.zeros(acc_ref.shape, F32)

    def body(kb, carry):
        kblk = k_ref[0, pl.ds(pl.multiple_of(kb * tk, tk), tk), :]
        vt = v_ref[kb]
        one = jnp.ones_like(vt)
        v_heads = (jnp.where(low, vt, one), jnp.where(low, one, vt))
        bias = jnp.concatenate(
            [jnp.concatenate([bias_ref[0, c, kb * nsub + a] for c in range(nsub)], axis=1) for a in range(nsub)],
            axis=0).astype(F32)
        s = [_dot(kblk, q_heads[h]) + bias for h in range(2)]
        m_old = [m_ref[h] for h in range(2)]
        m_new = [jnp.maximum(m_old[h], jnp.max(s[h], axis=0, keepdims=True)) for h in range(2)]
        p = [jnp.exp2(s[h] - m_new[h]).astype(BF16) for h in range(2)]
        pv = [_dot(v_heads[h], p[h]) for h in range(2)]
        for h in range(2):
            acc_ref[h] = jnp.exp2(m_old[h] - m_new[h]) * acc_ref[h] + pv[h]
            m_ref[h] = m_new[h]
        return carry

    lax.fori_loop(0, qt + 1, body, 0)
    a0 = acc_ref[0]
    a1 = acc_ref[1]
    o = jnp.where(low, a0 / a0[HEAD_DIM:HEAD_DIM + 1, :], a1 / a1[0:1, :])
    o_ref[0] = o.T.astype(o_ref.dtype)


def _dsa_attn(qt, k, vt, bias, B, S, tq):
    nq = S // tq
    npair = SEQ_WIDTH // LANES
    nsb, sub = bias.shape[2], bias.shape[3]
    kern = functools.partial(_dsa_attn_kernel, tq=tq, tk=tq, sub=sub)
    return pl.pallas_call(
        kern,
        grid=(B, npair, nq),
        in_specs=[pl.BlockSpec((1, LANES, tq), lambda b, h, i: (b * nq + i, h, 0)),
                  pl.BlockSpec((1, S, LANES), lambda b, h, i: (b, 0, h)),
                  pl.BlockSpec((nq, LANES, tq), lambda b, h, i: (b, h, 0)),
                  pl.BlockSpec((1, tq // sub, nsb, sub, sub), lambda b, h, i: (b, i, 0, 0, 0))],
        out_specs=pl.BlockSpec((1, tq, LANES), lambda b, h, i: (b, i, h)),
        out_shape=jax.ShapeDtypeStruct((B, S, SEQ_WIDTH), BF16),
        scratch_shapes=[pltpu.VMEM((2, 1, tq), F32), pltpu.VMEM((2, LANES, tq), F32)],
        compiler_params=_params("parallel", "parallel", "arbitrary"),
        name="dsa_attention",
    )(qt, k, vt, bias)


def _mem_attn_kernel(qm_ref, mem_ref, wkv_ref, o_ref, k_scr, v_scr):
    @pl.when(pl.program_id(1) == 0)
    def _():
        kv = _dot(mem_ref[0].astype(BF16), wkv_ref[...])
        k_scr[...] = kv[:, :MEM_WIDTH].astype(BF16)
        v_scr[...] = kv[:, MEM_WIDTH:].astype(BF16)

    lane = lax.broadcasted_iota(I32, (1, LANES), 1)
    low = lane < HEAD_DIM
    scale = HEAD_DIM ** -0.5
    for pair in range(MEM_WIDTH // LANES):
        cols = slice(pair * LANES, (pair + 1) * LANES)
        q = qm_ref[0, :, cols]
        zero = jnp.zeros_like(q)
        kp = k_scr[:, cols]
        vp = v_scr[:, cols]
        outs = []
        for qh in (jnp.where(low, q, zero), jnp.where(low, zero, q)):
            s = _dot_nt(qh, kp) * scale
            s = s - jnp.max(s, axis=1, keepdims=True)
            p = jnp.exp(s)
            p = p / jnp.sum(p, axis=1, keepdims=True)
            outs.append(_dot(p.astype(BF16), vp))
        o_ref[0, :, cols] = jnp.where(low, outs[0], outs[1]).astype(o_ref.dtype)


def _mem_attn(qm, mem, wkv, tq):
    B, S, _ = qm.shape
    M = mem.shape[1]
    return pl.pallas_call(
        _mem_attn_kernel,
        grid=(B, S // tq),
        in_specs=[pl.BlockSpec((1, tq, MEM_WIDTH), lambda b, i: (b, i, 0)),
                  pl.BlockSpec((1, M, D_MODEL), lambda b, i: (b, 0, 0)),
                  pl.BlockSpec(wkv.shape, lambda b, i: (0, 0))],
        out_specs=pl.BlockSpec((1, tq, MEM_WIDTH), lambda b, i: (b, i, 0)),
        out_shape=jax.ShapeDtypeStruct((B, S, MEM_WIDTH), BF16),
        scratch_shapes=[pltpu.VMEM((M, MEM_WIDTH), BF16), pltpu.VMEM((M, MEM_WIDTH), BF16)],
        compiler_params=_params("parallel", "arbitrary"),
        name="memory_attention",
    )(qm, mem, wkv)


def _gla_kernel(q_ref, k_ref, v_ref, r_ref, a_ref, wg_ref, bg_ref, ng_ref, o_ref, state_ref, *, nchunk):
    C = GLA_CHUNK
    state_ref[...] = jnp.zeros(state_ref.shape, F32)
    ri = lax.broadcasted_iota(I32, (C, C), 0)
    ci = lax.broadcasted_iota(I32, (C, C), 1)
    causal = ri >= ci
    tri = causal.astype(BF16)
    wg = wg_ref[0]
    bg = bg_ref[0]
    ng = ng_ref[...]

    def body(c, carry):
        rows = pl.ds(pl.multiple_of(c * C, C), C)
        z = _dot_f32(a_ref[0, rows, :], wg) + bg
        g = (jnp.minimum(z, 0.0) - jnp.log1p(jnp.exp(-jnp.abs(z)))) * (1.0 / GLA_TAU)
        g_hi, g_lo = _split_bf16(g)
        g_lo2 = (g - g_hi.astype(F32) - g_lo.astype(F32)).astype(BF16)
        b = _dot(tri, g_hi) + (_dot(tri, g_lo) + _dot(tri, g_lo2))
        b_last = b[C - 1:C, :]
        q = q_ref[0, rows, :].astype(F32) * (GLA_DK ** -0.5)
        k = k_ref[0, rows, :].astype(F32)
        v = v_ref[0, rows, :]
        qb = (q * jnp.exp(b)).astype(BF16)
        kb = (k * jnp.exp(-b)).astype(BF16)
        kd = (k * jnp.exp(b_last - b)).astype(BF16)
        st = state_ref[...]
        o = _dot_nt(qb, st.astype(BF16))
        attn = jnp.where(causal, _dot_nt(qb, kb), 0.0)
        o = o + _dot(attn.astype(BF16), v)
        state_ref[...] = st * jnp.exp(b_last) + _dot_tn(v, kd)
        ms = jnp.sum(o * o, axis=1, keepdims=True) * (1.0 / GLA_DV)
        o = o * lax.rsqrt(ms + RMS_EPS) * ng
        r = r_ref[0, rows, :].astype(F32)
        o_ref[0, rows, :] = (o * (r * jax.nn.sigmoid(r))).astype(o_ref.dtype)
        return carry

    lax.fori_loop(0, nchunk, body, 0)


def _gla(q, k, v, r, a1, wg, bg, ng):
    B, S, _ = q.shape
    kern = functools.partial(_gla_kernel, nchunk=S // GLA_CHUNK)
    kspec = pl.BlockSpec((1, S, GLA_DKP), lambda b, h: (b, 0, h))
    vspec = pl.BlockSpec((1, S, GLA_DVP), lambda b, h: (b, 0, h))
    return pl.pallas_call(
        kern,
        grid=(B, GLA_HEADS),
        in_specs=[kspec, kspec, vspec, vspec,
                  pl.BlockSpec((1, S, LANES), lambda b, h: (b, 0, 0)),
                  pl.BlockSpec((1, LANES, GLA_DKP), lambda b, h: (h, 0, 0)),
                  pl.BlockSpec((1, 1, GLA_DKP), lambda b, h: (h, 0, 0)),
                  pl.BlockSpec((1, GLA_DVP), lambda b, h: (0, 0))],
        out_specs=vspec,
        out_shape=jax.ShapeDtypeStruct((B, S, GLA_HEADS * GLA_DVP), BF16),
        scratch_shapes=[pltpu.VMEM((GLA_DVP, GLA_DKP), F32)],
        compiler_params=_params("parallel", "parallel"),
        name="gla",
    )(q, k, v, r, a1, wg, bg, ng)


def _mix_router_kernel(seq_ref, memo_ref, wa_ref, wb_ref, x_ref, g_ref, b_ref, wr_ref, br_ref,
                       x1_ref, comb_ref, *, alpha):
    mixed = _dot(seq_ref[...], wa_ref[...]) + _dot(memo_ref[...], wb_ref[...])
    x1 = _layer_norm(alpha * x_ref[...] + mixed, g_ref[...], b_ref[...])
    x1_ref[...] = x1
    logits = _dot_f32(x1, wr_ref[...]) + br_ref[...]
    lane = lax.broadcasted_iota(I32, logits.shape, 1)
    neg = -jnp.inf
    glog = jnp.where(lane < N_GROUPS, logits, neg)
    gmax = jnp.max(glog, axis=1, keepdims=True)
    gsel = jnp.min(jnp.where(glog == gmax, lane, LANES), axis=1, keepdims=True)
    pg = 1.0 / jnp.sum(jnp.exp(glog - gmax), axis=1, keepdims=True)
    lo = N_GROUPS + EXPERTS_PER_GROUP * gsel
    elog = jnp.where((lane >= lo) & (lane < lo + EXPERTS_PER_GROUP), logits, neg)
    v1 = jnp.max(elog, axis=1, keepdims=True)
    i1 = jnp.min(jnp.where(elog == v1, lane, LANES), axis=1, keepdims=True)
    elog2 = jnp.where(lane == i1, neg, elog)
    v2 = jnp.max(elog2, axis=1, keepdims=True)
    i2 = jnp.min(jnp.where(elog2 == v2, lane, LANES), axis=1, keepdims=True)
    e2 = jnp.exp(v2 - v1)
    den = 1.0 + e2
    comb_ref[...] = (jnp.where(lane == i1, pg / den, 0.0) + jnp.where(lane == i2, pg * e2 / den, 0.0))


def _mix_router(seq, memo, wa, wb, x2d, g, b, wr, br, alpha, tm):
    T = x2d.shape[0]
    row = lambda n: pl.BlockSpec((tm, n), lambda i: (i, 0))
    full = lambda a: pl.BlockSpec(a.shape, lambda i: (0,) * a.ndim)
    kern = functools.partial(_mix_router_kernel, alpha=alpha)
    return pl.pallas_call(
        kern,
        grid=(T // tm,),
        in_specs=[row(seq.shape[1]), row(MEM_WIDTH), full(wa), full(wb), row(D_MODEL), full(g), full(b),
                  full(wr), full(br)],
        out_specs=[row(D_MODEL), row(LANES)],
        out_shape=[jax.ShapeDtypeStruct((T, D_MODEL), F32), jax.ShapeDtypeStruct((T, LANES), F32)],
        compiler_params=_params("parallel"),
        name="outproj_ln_router",
    )(seq, memo, wa, wb, x2d, g, b, wr, br)


def _moe_kernel(x_ref, comb_ref, w13_ref, w2_ref, g_ref, b_ref, o_ref, xb_ref, acc_ref, *, alpha):
    e = pl.program_id(1)

    @pl.when(e == 0)
    def _():
        xb_ref[...] = x_ref[...].astype(BF16)
        acc_ref[...] = jnp.zeros(acc_ref.shape, F32)

    h = _dot(xb_ref[...], w13_ref[0])
    a = h[:, :EXPERT_FF]
    u = h[:, EXPERT_FF:]
    comb = comb_ref[...]
    lane = lax.broadcasted_iota(I32, comb.shape, 1)
    c = jnp.sum(jnp.where(lane == N_GROUPS + e, comb, 0.0), axis=1, keepdims=True)
    act = (a * jax.nn.sigmoid(a)) * u * c
    acc_ref[...] += _dot(act.astype(BF16), w2_ref[0])

    @pl.when(e == pl.num_programs(1) - 1)
    def _():
        o_ref[...] = _layer_norm(alpha * x_ref[...] + acc_ref[...], g_ref[...], b_ref[...])


def _moe(x1, comb, w13, w2, g, b, alpha, tm):
    T = x1.shape[0]
    kern = functools.partial(_moe_kernel, alpha=alpha)
    return pl.pallas_call(
        kern,
        grid=(T // tm, N_EXPERTS),
        in_specs=[pl.BlockSpec((tm, D_MODEL), lambda i, e: (i, 0)),
                  pl.BlockSpec((tm, LANES), lambda i, e: (i, 0)),
                  pl.BlockSpec((1, D_MODEL, 2 * EXPERT_FF), lambda i, e: (e, 0, 0)),
                  pl.BlockSpec((1, EXPERT_FF, D_MODEL), lambda i, e: (e, 0, 0)),
                  pl.BlockSpec((1, D_MODEL), lambda i, e: (0, 0)),
                  pl.BlockSpec((1, D_MODEL), lambda i, e: (0, 0))],
        out_specs=pl.BlockSpec((tm, D_MODEL), lambda i, e: (i, 0)),
        out_shape=jax.ShapeDtypeStruct((T, D_MODEL), F32),
        scratch_shapes=[pltpu.VMEM((tm, D_MODEL), BF16), pltpu.VMEM((tm, D_MODEL), F32)],
        compiler_params=_params("parallel", "arbitrary"),
        name="moe_experts_ln",
    )(x1, comb, w13, w2, g, b)


def _pad_cols(a, width):
    return jnp.pad(a, ((0, 0), (0, width - a.shape[1])))


def _pad_heads(w, heads, dim, dim_pad):
    rows = w.shape[0]
    return jnp.pad(w.reshape(rows, heads, dim), ((0, 0), (0, 0), (0, dim_pad - dim))).reshape(rows, heads * dim_pad)


def _tile(n, pref):
    t = pref
    while n % t:
        t //= 2
    return t


def kernel(x, mem, positions, dsa_w_in, dsa_idx_k_g, dsa_idx_k_b, gla_w_in, gla_w_gate, gla_b_gate, gla_norm_g,
           w_mem_kv, w_out, ln1_g, ln1_b, ln2_g, ln2_b, moe_w_group, moe_b_group, moe_w_router, moe_b_router,
           moe_w13, moe_w2):
    B, S, D = x.shape
    T = B * S
    depth = w_out.shape[0]
    alpha = (2 * depth) ** 0.25
    tm = _tile(T, 512)
    tq = _tile(S, 512)
    tq_idx = _tile(S, 256)
    topk = min(DSA_MAX_TOPK, S // 4)

    inv = ROPE_THETA ** (-jnp.arange(0, HEAD_DIM, 2, dtype=F32) / HEAD_DIM)
    ang = positions.astype(F32).reshape(T, 1) * inv
    cos, sin = jnp.cos(ang), jnp.sin(ang)
    cosf = jnp.concatenate([cos, cos, cos, cos], axis=1)
    sinf = jnp.concatenate([-sin, sin, -sin, sin], axis=1)

    xc = x.reshape(T, D)
    ia = ib = 0
    for i in range(depth):
        if i % 2 == 0:
            w = dsa_w_in[ia]
            wq, wk, wv, wqi, wki, wwi, wqm = jnp.split(w, [768, 1536, 2304, 2816, 2880, 2888], axis=1)
            w_tok = jnp.concatenate([wk, wqm, _pad_cols(wki, LANES)], axis=1).astype(BF16)
            w_feat = jnp.concatenate([wq, wv, wqi, _pad_cols(wwi, 16)], axis=1).T.astype(BF16)
            lng = _pad_cols(dsa_idx_k_g[ia][None, :], LANES)
            lnb = _pad_cols(dsa_idx_k_b[ia][None, :], LANES)
            k, qm, ki, q_t, v_t, qi_t, wi_t = _proj_dsa(xc, w_tok, w_feat, cosf, sinf, cosf.T, sinf.T, lng, lnb,
                                                         tm, tq, tq_idx)
            bias = _idx_mask(qi_t, ki.reshape(B, S, LANES), wi_t, B, S, topk, tq_idx)
            seq = _dsa_attn(q_t, k.reshape(B, S, SEQ_WIDTH), v_t, bias, B, S, tq).reshape(T, SEQ_WIDTH)
            wa = w_out[i][:SEQ_WIDTH].astype(BF16)
            ia += 1
        else:
            w = gla_w_in[ib]
            wq, wk, wv, wr_, wa1, wqm = jnp.split(w, [384, 768, 1536, 2304, 2320], axis=1)
            w_all = jnp.concatenate([
                _pad_heads(wq, GLA_HEADS, GLA_DK, GLA_DKP), _pad_heads(wk, GLA_HEADS, GLA_DK, GLA_DKP),
                _pad_heads(wv, GLA_HEADS, GLA_DV, GLA_DVP), _pad_heads(wr_, GLA_HEADS, GLA_DV, GLA_DVP),
                wqm, _pad_cols(wa1, LANES)], axis=1).astype(BF16)
            q, k, v, r, qm, a1 = _proj_gla(xc, w_all, tm)
            r3 = lambda a: a.reshape(B, S, a.shape[1])
            wg = _pad_heads(gla_w_gate[ib], GLA_HEADS, GLA_DK, GLA_DKP)
            wg = jnp.pad(wg, ((0, LANES - GLA_GATE_RANK), (0, 0)))
            wg = wg.reshape(LANES, GLA_HEADS, GLA_DKP).transpose(1, 0, 2)
            bg = _pad_heads(gla_b_gate[ib][None, :], GLA_HEADS, GLA_DK, GLA_DKP).reshape(GLA_HEADS, 1, GLA_DKP)
            ng = _pad_cols(gla_norm_g[ib][None, :], GLA_DVP)
            seq = _gla(r3(q), r3(k), r3(v), r3(r), r3(a1), wg, bg, ng).reshape(T, GLA_HEADS * GLA_DVP)
            wa = w_out[i][:SEQ_WIDTH].reshape(GLA_HEADS, GLA_DV, D)
            wa = jnp.pad(wa, ((0, 0), (0, GLA_DVP - GLA_DV), (0, 0))).reshape(GLA_HEADS * GLA_DVP, D).astype(BF16)
            ib += 1
        memo = _mem_attn(qm.reshape(B, S, MEM_WIDTH), mem, w_mem_kv[i].astype(BF16), tq).reshape(T, MEM_WIDTH)
        wb = w_out[i][SEQ_WIDTH:].astype(BF16)
        wr = jnp.concatenate([moe_w_group[i], moe_w_router[i].transpose(1, 0, 2).reshape(D, N_EXPERTS)], axis=1)
        wr = _pad_cols(wr, LANES)
        br = _pad_cols(jnp.concatenate([moe_b_group[i], moe_b_router[i].reshape(-1)])[None, :], LANES)
        x1, comb = _mix_router(seq, memo, wa, wb, xc, ln1_g[i][None, :], ln1_b[i][None, :], wr, br, alpha, tm)
        w13 = moe_w13[i].reshape(N_EXPERTS, D, 2 * EXPERT_FF).astype(BF16)
        w2 = moe_w2[i].reshape(N_EXPERTS, EXPERT_FF, D).astype(BF16)
        xc = _moe(x1, comb, w13, w2, ln2_g[i][None, :], ln2_b[i][None, :], alpha, _tile(T, 1024))
    return xc.reshape(B, S, D)
```

```python
import functools
import math

import jax
import jax.numpy as jnp
from jax import lax
from jax.experimental import pallas as pl
from jax.experimental.pallas import tpu as pltpu

F32 = jnp.float32
BF16 = jnp.bfloat16
I32 = jnp.int32

LANES = 128
D_MODEL = 1024
HEAD_DIM = 64
N_MEM_HEADS = 4
MEM_WIDTH = N_MEM_HEADS * HEAD_DIM
SEQ_WIDTH = D_MODEL - MEM_WIDTH
ROPE_THETA = 10000.0
DSA_HEADS = SEQ_WIDTH // HEAD_DIM
IDX_HEADS = 8
IDX_DIM = 64
DSA_MAX_TOPK = 256
GLA_HEADS = 4
GLA_DV = SEQ_WIDTH // GLA_HEADS
GLA_DK = GLA_DV // 2
GLA_DKP = 128
GLA_DVP = 256
GLA_GATE_RANK = 16
GLA_TAU = 16.0
GLA_CHUNK = 64
N_GROUPS = 4
EXPERTS_PER_GROUP = 8
N_EXPERTS = N_GROUPS * EXPERTS_PER_GROUP
EXPERT_FF = 256
LN_EPS = 1e-5
RMS_EPS = 1e-6
MASK_BIAS = -1e30
INT_MIN = -2 ** 31
VMEM_LIMIT = 56 * 1024 * 1024


def _dot(a, b):
    return jnp.dot(a, b, preferred_element_type=F32)


def _dot_nt(a, b):
    return lax.dot_general(a, b, (((1,), (1,)), ((), ())), preferred_element_type=F32)


def _dot_tn(a, b):
    return lax.dot_general(a, b, (((0,), (0,)), ((), ())), preferred_element_type=F32)


def _split_bf16(a):
    hi = a.astype(BF16)
    lo = (a - hi.astype(F32)).astype(BF16)
    return hi, lo


def _dot_f32(a, b):
    ah, al = _split_bf16(a)
    bh, bl = _split_bf16(b)
    return _dot(ah, bh) + (_dot(ah, bl) + _dot(al, bh))


def _layer_norm(y, g, b):
    mu = jnp.mean(y, axis=-1, keepdims=True)
    yc = y - mu
    var = jnp.mean(yc * yc, axis=-1, keepdims=True)
    return yc * lax.rsqrt(var + LN_EPS) * g + b


def _params(*sem):
    return pltpu.CompilerParams(dimension_semantics=sem, vmem_limit_bytes=VMEM_LIMIT)


def _rope_fn(cos, sin, axis):
    shape = (1, LANES) if axis == 1 else (LANES, 1)
    pos = lax.broadcasted_iota(I32, shape, axis)
    first_half = (pos % HEAD_DIM) < (HEAD_DIM // 2)

    def rope(a):
        swapped = jnp.where(first_half, pltpu.roll(a, LANES - HEAD_DIM // 2, axis),
                            pltpu.roll(a, HEAD_DIM // 2, axis))
        return a * cos + swapped * sin
    return rope


def _proj_dsa_kernel(x_ref, wa_ref, wb_ref, cos_ref, sin_ref, cost_ref, sint_ref, lng_ref, lnb_ref,
                     k_ref, qm_ref, ki_ref, qt_ref, vt_ref, qit_ref, wit_ref):
    xb = x_ref[...].astype(BF16)
    tm = xb.shape[0]
    rope = _rope_fn(cos_ref[...], sin_ref[...], 1)
    rope_t = _rope_fn(cost_ref[...], sint_ref[...], 0)
    acc = _dot(xb, wa_ref[:, 0:SEQ_WIDTH])
    for c in range(SEQ_WIDTH // LANES):
        k_ref[:, c * LANES:(c + 1) * LANES] = rope(acc[:, c * LANES:(c + 1) * LANES]).astype(BF16)
    qm_ref[...] = _dot(xb, wa_ref[:, SEQ_WIDTH:SEQ_WIDTH + MEM_WIDTH]).astype(BF16)
    acc = _dot(xb, wa_ref[:, D_MODEL:D_MODEL + LANES])
    lane = lax.broadcasted_iota(I32, (1, LANES), 1)
    is_ki = lane < IDX_DIM
    mu = jnp.sum(jnp.where(is_ki, acc, 0.0), axis=1, keepdims=True) * (1.0 / IDX_DIM)
    d = jnp.where(is_ki, acc - mu, 0.0)
    var = jnp.sum(d * d, axis=1, keepdims=True) * (1.0 / IDX_DIM)
    ki_ref[...] = rope(d * lax.rsqrt(var + LN_EPS) * lng_ref[...] + lnb_ref[...]).astype(BF16)

    def store_t(out_ref, r, val):
        nblk, _, tb = out_ref.shape
        rows = val.shape[0]
        for j in range(nblk):
            out_ref[j, r * rows:(r + 1) * rows, :] = val[:, j * tb:(j + 1) * tb].astype(out_ref.dtype)

    q_scale = HEAD_DIM ** -0.5 * math.log2(math.e)
    for r in range(SEQ_WIDTH // LANES):
        acc = _dot_nt(wb_ref[r * LANES:(r + 1) * LANES, :], xb)
        store_t(qt_ref, r, rope_t(acc) * q_scale)
    off = SEQ_WIDTH
    for r in range(SEQ_WIDTH // LANES):
        store_t(vt_ref, r, _dot_nt(wb_ref[off + r * LANES:off + (r + 1) * LANES, :], xb))
    off = 2 * SEQ_WIDTH
    for r in range(IDX_HEADS * IDX_DIM // LANES):
        store_t(qit_ref, r, rope_t(_dot_nt(wb_ref[off + r * LANES:off + (r + 1) * LANES, :], xb)))
    off = 2 * SEQ_WIDTH + IDX_HEADS * IDX_DIM
    wi = _dot_nt(wb_ref[off:off + 16, :], xb)[:IDX_HEADS, :] * (IDX_HEADS ** -0.5 * IDX_DIM ** -0.5)
    store_t(wit_ref, 0, wi)


def _proj_dsa(x2d, wa, wb, cosf, sinf, cost, sint, lng, lnb, tm, tq_att, tq_idx):
    T = x2d.shape[0]
    row = lambda n: pl.BlockSpec((tm, n), lambda i: (i, 0))
    col = pl.BlockSpec((LANES, tm), lambda i: (0, i))
    full = lambda a: pl.BlockSpec(a.shape, lambda i: (0,) * a.ndim)
    featmaj = lambda n, tb: pl.BlockSpec((tm // tb, n, tb), lambda i: (i, 0, 0))
    fshape = lambda n, tb, dt: jax.ShapeDtypeStruct((T // tb, n, tb), dt)
    return pl.pallas_call(
        _proj_dsa_kernel,
        grid=(T // tm,),
        in_specs=[row(D_MODEL), full(wa), full(wb), row(LANES), row(LANES), col, col, full(lng), full(lnb)],
        out_specs=[row(SEQ_WIDTH), row(MEM_WIDTH), row(LANES),
                   featmaj(SEQ_WIDTH, tq_att), featmaj(SEQ_WIDTH, tq_att),
                   featmaj(IDX_HEADS * IDX_DIM, tq_idx), featmaj(IDX_HEADS, tq_idx)],
        out_shape=[jax.ShapeDtypeStruct((T, SEQ_WIDTH), BF16), jax.ShapeDtypeStruct((T, MEM_WIDTH), BF16),
                   jax.ShapeDtypeStruct((T, LANES), BF16),
                   fshape(SEQ_WIDTH, tq_att, BF16), fshape(SEQ_WIDTH, tq_att, BF16),
                   fshape(IDX_HEADS * IDX_DIM, tq_idx, BF16), fshape(IDX_HEADS, tq_idx, F32)],
        compiler_params=_params("parallel"),
        name="proj_dsa",
    )(x2d, wa, wb, cosf, sinf, cost, sint, lng, lnb)


def _proj_gla_kernel(x_ref, w_ref, q_ref, k_ref, v_ref, r_ref, qm_ref, a_ref):
    xb = x_ref[...].astype(BF16)
    segments = ((0, 512, q_ref), (512, 512, k_ref), (1024, 1024, v_ref), (2048, 1024, r_ref),
                (3072, 256, qm_ref), (3328, 128, a_ref))
    for off, width, out_ref in segments:
        out_ref[...] = _dot(xb, w_ref[:, off:off + width]).astype(out_ref.dtype)


def _proj_gla(x2d, w, tm):
    T = x2d.shape[0]
    row = lambda n: pl.BlockSpec((tm, n), lambda i: (i, 0))
    outs = [(512, BF16), (512, BF16), (1024, BF16), (1024, BF16), (256, BF16), (LANES, F32)]
    return pl.pallas_call(
        _proj_gla_kernel,
        grid=(T // tm,),
        in_specs=[row(D_MODEL), pl.BlockSpec(w.shape, lambda i: (0, 0))],
        out_specs=[row(n) for n, _ in outs],
        out_shape=[jax.ShapeDtypeStruct((T, n), dt) for n, dt in outs],
        compiler_params=_params("parallel"),
        name="proj_gla",
    )(x2d, w)


def _idx_kernel(qi_ref, ki_ref, wi_ref, bias_ref, keys_ref, cut_ref, *, tq, tk, nkb, topk, seq):
    qt = pl.program_id(1)
    n_act = qt + 1
    key_l = lax.broadcasted_iota(I32, (tk, tq), 0)
    qry_g = qt * tq + lax.broadcasted_iota(I32, (tk, tq), 1)
    wi = wi_ref[0]

    def score_body(kb, carry):
        kblk = ki_ref[0, pl.ds(pl.multiple_of(kb * tk, tk), tk), :][:, :IDX_DIM]
        sc = jnp.zeros((tk, tq), F32)
        for h in range(IDX_HEADS):
            qh = qi_ref[0, h * IDX_DIM:(h + 1) * IDX_DIM, :]
            sc = sc + jnp.maximum(_dot(kblk, qh), 0.0) * wi[h:h + 1, :]
        bits = lax.bitcast_convert_type(sc, I32)
        key = bits ^ ((bits >> 31) & 0x7FFFFFFF)
        keys_ref[kb] = jnp.where(kb * tk + key_l > qry_g, INT_MIN, key)
        return carry

    lax.fori_loop(0, n_act, score_body, 0)

    def count(pred):
        def body(kb, acc):
            m = pred(keys_ref[kb], kb).astype(F32)
            parts = [m[j * 8:(j + 1) * 8] for j in range(4)]
            for j in range(4, tk // 8):
                parts[j % 4] = parts[j % 4] + m[j * 8:(j + 1) * 8]
            return acc + ((parts[0] + parts[1]) + (parts[2] + parts[3]))
        acc = lax.fori_loop(0, n_act, body, jnp.zeros((8, tq), F32))
        return jnp.sum(acc, axis=0, keepdims=True)

    def bit_body(i, base):
        cand = base ^ lax.shift_left(jnp.int32(1), (31 - i).astype(I32))
        cnt = count(lambda kk, kb: kk >= cand)
        return jnp.where(cnt >= topk, cand, base)

    base = lax.fori_loop(0, 32, bit_body, jnp.full((1, tq), INT_MIN, I32))

    cnt_gt = count(lambda kk, kb: kk > base)
    cnt_ge = count(lambda kk, kb: kk >= base)
    need = topk - cnt_gt
    tie = (cnt_ge > topk) & (base != INT_MIN)
    cut_ref[...] = jnp.full((1, tq), seq, I32)

    @pl.when(jnp.max(tie.astype(F32)) > 0.0)
    def _():
        nbits = int(math.log2(seq))

        def idx_body(i, m):
            cand = m | lax.shift_left(jnp.int32(1), (nbits - 1 - i).astype(I32))
            below = count(lambda kk, kb: (kk == base) & (kb * tk + key_l < cand))
            return jnp.where(below < need, cand, m)

        m = lax.fori_loop(0, nbits, idx_body, jnp.zeros((1, tq), I32))
        cut_ref[...] = jnp.where(tie, m, seq)

    cut = cut_ref[...]
    for kb in range(nkb):
        @pl.when(kb <= qt)
        def _():
            kk = keys_ref[kb]
            sel = ((kk > base) | ((kk == base) & (kb * tk + key_l <= cut))) & (kk != INT_MIN)
            bias_ref[0, 0, kb] = jnp.where(sel, F32(0.0), F32(MASK_BIAS)).astype(BF16)

        @pl.when(kb > qt)
        def _():
            bias_ref[0, 0, kb] = jnp.full((tk, tq), MASK_BIAS, BF16)


def _idx_mask(qit, ki, wit, B, S, topk, tq):
    nq = S // tq
    kern = functools.partial(_idx_kernel, tq=tq, tk=tq, nkb=nq, topk=topk, seq=S)
    return pl.pallas_call(
        kern,
        grid=(B, nq),
        in_specs=[pl.BlockSpec((1, IDX_HEADS * IDX_DIM, tq), lambda b, q: (b * nq + q, 0, 0)),
                  pl.BlockSpec((1, S, LANES), lambda b, q: (b, 0, 0)),
                  pl.BlockSpec((1, IDX_HEADS, tq), lambda b, q: (b * nq + q, 0, 0))],
        out_specs=pl.BlockSpec((1, 1, nq, tq, tq), lambda b, q: (b, q, 0, 0, 0)),
        out_shape=jax.ShapeDtypeStruct((B, nq, nq, tq, tq), BF16),
        scratch_shapes=[pltpu.VMEM((nq, tq, tq), I32), pltpu.VMEM((1, tq), I32)],
        compiler_params=_params("parallel", "arbitrary"),
        name="dsa_index_select",
    )(qit, ki, wit)


def _dsa_attn_kernel(q_ref, k_ref, v_ref, bias_ref, o_ref, m_ref, acc_ref, *, tq, tk, sub):
    qt = pl.program_id(2)
    nsub = tq // sub
    low = lax.broadcasted_iota(I32, (LANES, 1), 0) < HEAD_DIM
    q = q_ref[0]
    zero = jnp.zeros_like(q)
    q_heads = (jnp.where(low, q, zero), jnp.where(low, zero, q))
    m_ref[...] = jnp.full(m_ref.shape, -jnp.inf, F32)
    acc_ref[...] = jnp.zeros(acc_ref.shape, F32)

    def body(kb, carry):
        kblk = k_ref[0, pl.ds(pl.multiple_of(kb * tk, tk), tk), :]
        vt = v_ref[kb]
        one = jnp.ones_like(vt)
        v_heads = (jnp.where(low, vt, one), jnp.where(low, one, vt))
        bias = jnp.concatenate(
            [jnp.concatenate([bias_ref[0, c, kb * nsub + a] for c in range(nsub)], axis=1) for a in range(nsub)],
            axis=0).astype(F32)
        s = [_dot(kblk, q_heads[h]) + bias for h in range(2)]
        m_old = [m_ref[h] for h in range(2)]
        m_new = [jnp.maximum(m_old[h], jnp.max(s[h], axis=0, keepdims=True)) for h in range(2)]
        p = [jnp.exp2(s[h] - m_new[h]).astype(BF16) for h in range(2)]
        pv = [_dot(v_heads[h], p[h]) for h in range(2)]
        for h in range(2):
            acc_ref[h] = jnp.exp2(m_old[h] - m_new[h]) * acc_ref[h] + pv[h]
            m_ref[h] = m_new[h]
        return carry

    lax.fori_loop(0, qt + 1, body, 0)
    a0 = acc_ref[0]
    a1 = acc_ref[1]
    o = jnp.where(low, a0 / a0[HEAD_DIM:HEAD_DIM + 1, :], a1 / a1[0:1, :])
    o_ref[0] = o.T.astype(o_ref.dtype)


def _dsa_attn(qt, k, vt, bias, B, S, tq):
    nq = S // tq
    npair = SEQ_WIDTH // LANES
    nsb, sub = bias.shape[2], bias.shape[3]
    kern = functools.partial(_dsa_attn_kernel, tq=tq, tk=tq, sub=sub)
    return pl.pallas_call(
        kern,
        grid=(B, npair, nq),
        in_specs=[pl.BlockSpec((1, LANES, tq), lambda b, h, i: (b * nq + i, h, 0)),
                  pl.BlockSpec((1, S, LANES), lambda b, h, i: (b, 0, h)),
                  pl.BlockSpec((nq, LANES, tq), lambda b, h, i: (b, h, 0)),
                  pl.BlockSpec((1, tq // sub, nsb, sub, sub), lambda b, h, i: (b, i, 0, 0, 0))],
        out_specs=pl.BlockSpec((1, tq, LANES), lambda b, h, i: (b, i, h)),
        out_shape=jax.ShapeDtypeStruct((B, S, SEQ_WIDTH), BF16),
        scratch_shapes=[pltpu.VMEM((2, 1, tq), F32), pltpu.VMEM((2, LANES, tq), F32)],
        compiler_params=_params("parallel", "parallel", "arbitrary"),
        name="dsa_attention",
    )(qt, k, vt, bias)


def _mem_attn_kernel(qm_ref, mem_ref, wkv_ref, o_ref, k_scr, v_scr):
    @pl.when(pl.program_id(1) == 0)
    def _():
        kv = _dot(mem_ref[0].astype(BF16), wkv_ref[...])
        k_scr[...] = kv[:, :MEM_WIDTH].astype(BF16)
        v_scr[...] = kv[:, MEM_WIDTH:].astype(BF16)

    lane = lax.broadcasted_iota(I32, (1, LANES), 1)
    low = lane < HEAD_DIM
    scale = HEAD_DIM ** -0.5
    for pair in range(MEM_WIDTH // LANES):
        cols = slice(pair * LANES, (pair + 1) * LANES)
        q = qm_ref[0, :, cols]
        zero = jnp.zeros_like(q)
        kp = k_scr[:, cols]
        vp = v_scr[:, cols]
        outs = []
        for qh in (jnp.where(low, q, zero), jnp.where(low, zero, q)):
            s = _dot_nt(qh, kp) * scale
            s = s - jnp.max(s, axis=1, keepdims=True)
            p = jnp.exp(s)
            p = p / jnp.sum(p, axis=1, keepdims=True)
            outs.append(_dot(p.astype(BF16), vp))
        o_ref[0, :, cols] = jnp.where(low, outs[0], outs[1]).astype(o_ref.dtype)


def _mem_attn(qm, mem, wkv, tq):
    B, S, _ = qm.shape
    M = mem.shape[1]
    return pl.pallas_call(
        _mem_attn_kernel,
        grid=(B, S // tq),
        in_specs=[pl.BlockSpec((1, tq, MEM_WIDTH), lambda b, i: (b, i, 0)),
                  pl.BlockSpec((1, M, D_MODEL), lambda b, i: (b, 0, 0)),
                  pl.BlockSpec(wkv.shape, lambda b, i: (0, 0))],
        out_specs=pl.BlockSpec((1, tq, MEM_WIDTH), lambda b, i: (b, i, 0)),
        out_shape=jax.ShapeDtypeStruct((B, S, MEM_WIDTH), BF16),
        scratch_shapes=[pltpu.VMEM((M, MEM_WIDTH), BF16), pltpu.VMEM((M, MEM_WIDTH), BF16)],
        compiler_params=_params("parallel", "arbitrary"),
        name="memory_attention",
    )(qm, mem, wkv)


def _gla_kernel(q_ref, k_ref, v_ref, r_ref, a_ref, wg_ref, bg_ref, ng_ref, o_ref,
                qb_ref, kb_ref, kd_ref, dl_ref, oi_ref, kv_ref, *, nchunk):
    C = GLA_CHUNK
    S = nchunk * C
    z = _dot_f32(a_ref[0], wg_ref[0]) + bg_ref[0]
    b = (jnp.minimum(z, 0.0) - jnp.log1p(jnp.exp(-jnp.abs(z)))) * (1.0 / GLA_TAU)
    pos = lax.broadcasted_iota(I32, (S, 1), 0) % C
    shift = 1
    while shift < C:
        b = b + jnp.where(pos >= shift, pltpu.roll(b, shift, 0), 0.0)
        shift *= 2
    b3 = b.reshape(nchunk, C, GLA_DKP)
    b_last = b3[:, C - 1:C, :]
    q = q_ref[0].astype(F32) * (GLA_DK ** -0.5)
    k = k_ref[0].astype(F32)
    qb_ref[...] = (q * jnp.exp(b)).astype(BF16)
    kb_ref[...] = (k * jnp.exp(-b)).astype(BF16)
    kd_ref[...] = (k.reshape(nchunk, C, GLA_DKP) * jnp.exp(b_last - b3)).reshape(S, GLA_DKP).astype(BF16)
    dl_ref[...] = jnp.exp(b_last)
    causal = lax.broadcasted_iota(I32, (C, C), 0) >= lax.broadcasted_iota(I32, (C, C), 1)

    def intra(c, carry):
        rows = pl.ds(pl.multiple_of(c * C, C), C)
        v = v_ref[0, rows, :]
        attn = jnp.where(causal, _dot_nt(qb_ref[rows, :], kb_ref[rows, :]), 0.0)
        oi_ref[rows, :] = _dot(attn.astype(BF16), v)
        kv_ref[c] = _dot_tn(v, kd_ref[rows, :])
        return carry

    lax.fori_loop(0, nchunk, intra, 0, unroll=4)
    ng = ng_ref[...]

    def inter(c, st):
        rows = pl.ds(pl.multiple_of(c * C, C), C)
        o = oi_ref[rows, :] + _dot_nt(qb_ref[rows, :], st.astype(BF16))
        ms = jnp.sum(o * o, axis=1, keepdims=True) * (1.0 / GLA_DV)
        o = o * lax.rsqrt(ms + RMS_EPS) * ng
        r = r_ref[0, rows, :].astype(F32)
        o_ref[0, rows, :] = (o * (r * jax.nn.sigmoid(r))).astype(o_ref.dtype)
        return st * dl_ref[c] + kv_ref[c]

    lax.fori_loop(0, nchunk, inter, jnp.zeros((GLA_DVP, GLA_DKP), F32), unroll=4)


def _gla(q, k, v, r, a1, wg, bg, ng):
    B, S, _ = q.shape
    nchunk = S // GLA_CHUNK
    kern = functools.partial(_gla_kernel, nchunk=nchunk)
    kspec = pl.BlockSpec((1, S, GLA_DKP), lambda b, h: (b, 0, h))
    vspec = pl.BlockSpec((1, S, GLA_DVP), lambda b, h: (b, 0, h))
    return pl.pallas_call(
        kern,
        grid=(B, GLA_HEADS),
        in_specs=[kspec, kspec, vspec, vspec,
                  pl.BlockSpec((1, S, LANES), lambda b, h: (b, 0, 0)),
                  pl.BlockSpec((1, LANES, GLA_DKP), lambda b, h: (h, 0, 0)),
                  pl.BlockSpec((1, 1, GLA_DKP), lambda b, h: (h, 0, 0)),
                  pl.BlockSpec((1, GLA_DVP), lambda b, h: (0, 0))],
        out_specs=vspec,
        out_shape=jax.ShapeDtypeStruct((B, S, GLA_HEADS * GLA_DVP), BF16),
        scratch_shapes=[pltpu.VMEM((S, GLA_DKP), BF16), pltpu.VMEM((S, GLA_DKP), BF16), pltpu.VMEM((S, GLA_DKP), BF16),
                        pltpu.VMEM((nchunk, 1, GLA_DKP), F32), pltpu.VMEM((S, GLA_DVP), F32),
                        pltpu.VMEM((nchunk, GLA_DVP, GLA_DKP), F32)],
        compiler_params=_params("parallel", "parallel"),
        name="gla",
    )(q, k, v, r, a1, wg, bg, ng)


def _mix_router_kernel(seq_ref, memo_ref, wa_ref, wb_ref, x_ref, g_ref, b_ref, wr_ref, br_ref,
                       x1_ref, comb_ref, *, alpha):
    mixed = _dot(seq_ref[...], wa_ref[...]) + _dot(memo_ref[...], wb_ref[...])
    x1 = _layer_norm(alpha * x_ref[...] + mixed, g_ref[...], b_ref[...])
    x1_ref[...] = x1
    logits = _dot_f32(x1, wr_ref[...]) + br_ref[...]
    lane = lax.broadcasted_iota(I32, logits.shape, 1)
    neg = -jnp.inf
    glog = jnp.where(lane < N_GROUPS, logits, neg)
    gmax = jnp.max(glog, axis=1, keepdims=True)
    gsel = jnp.min(jnp.where(glog == gmax, lane, LANES), axis=1, keepdims=True)
    pg = 1.0 / jnp.sum(jnp.exp(glog - gmax), axis=1, keepdims=True)
    lo = N_GROUPS + EXPERTS_PER_GROUP * gsel
    elog = jnp.where((lane >= lo) & (lane < lo + EXPERTS_PER_GROUP), logits, neg)
    v1 = jnp.max(elog, axis=1, keepdims=True)
    i1 = jnp.min(jnp.where(elog == v1, lane, LANES), axis=1, keepdims=True)
    elog2 = jnp.where(lane == i1, neg, elog)
    v2 = jnp.max(elog2, axis=1, keepdims=True)
    i2 = jnp.min(jnp.where(elog2 == v2, lane, LANES), axis=1, keepdims=True)
    e2 = jnp.exp(v2 - v1)
    den = 1.0 + e2
    comb_ref[...] = (jnp.where(lane == i1 - lo, pg / den, 0.0) + jnp.where(lane == i2 - lo, pg * e2 / den, 0.0)
                     + jnp.where(lane == EXPERTS_PER_GROUP, (gsel.astype(F32)), 0.0))


def _mix_router(seq, memo, wa, wb, x2d, g, b, wr, br, alpha, tm):
    T = x2d.shape[0]
    row = lambda n: pl.BlockSpec((tm, n), lambda i: (i, 0))
    full = lambda a: pl.BlockSpec(a.shape, lambda i: (0,) * a.ndim)
    kern = functools.partial(_mix_router_kernel, alpha=alpha)
    return pl.pallas_call(
        kern,
        grid=(T // tm,),
        in_specs=[row(seq.shape[1]), row(MEM_WIDTH), full(wa), full(wb), row(D_MODEL), full(g), full(b),
                  full(wr), full(br)],
        out_specs=[row(D_MODEL), row(LANES)],
        out_shape=[jax.ShapeDtypeStruct((T, D_MODEL), F32), jax.ShapeDtypeStruct((T, LANES), F32)],
        compiler_params=_params("parallel"),
        name="outproj_ln_router",
    )(seq, memo, wa, wb, x2d, g, b, wr, br)


MOE_BLK = 256
MOE_HALF = EXPERTS_PER_GROUP // 2


def _moe_kernel(x_ref, route_ref, w13_ref, w2_ref, g_ref, b_ref, o_ref,
                xs_ref, ys_ref, cs_ref, dest_ref, yt_ref, blk_ref, *, alpha, tm, nblk):
    g = pl.program_id(1)
    hf = pl.program_id(2)
    slot = lax.broadcasted_iota(I32, (MOE_BLK, tm), 0)

    def perm_block(j):
        return (slot + j * MOE_BLK == dest_ref[...]).astype(BF16)

    @pl.when((g == 0) & (hf == 0))
    def _():
        rt = route_ref[...].T
        gid = rt[EXPERTS_PER_GROUP:EXPERTS_PER_GROUP + 1, :]
        grp = lax.broadcasted_iota(I32, (8, 1), 0).astype(F32)
        onehot = (gid == grp).astype(F32)
        earlier = (lax.broadcasted_iota(I32, (tm, tm), 0) < lax.broadcasted_iota(I32, (tm, tm), 1)).astype(BF16)
        rank = _dot(onehot.astype(BF16), earlier)
        cnt = jnp.sum(onehot, axis=1, keepdims=True)
        nb = jnp.floor((cnt + (MOE_BLK - 1)) * (1.0 / MOE_BLK))
        sb = [jnp.zeros((1, 1), F32)]
        for k in range(1, N_GROUPS):
            sb.append(sb[-1] + nb[k - 1:k, :])
        for k in range(N_GROUPS):
            blk_ref[k] = jnp.max(sb[k]).astype(I32)
            blk_ref[N_GROUPS + k] = jnp.max(nb[k:k + 1, :]).astype(I32)
        start = jnp.concatenate(sb + [jnp.zeros((8 - N_GROUPS, 1), F32)], axis=0) * MOE_BLK
        dest_ref[...] = jnp.sum(onehot * (start + rank), axis=0, keepdims=True).astype(I32)
        xt = x_ref[...].T.astype(BF16)
        w = rt[0:EXPERTS_PER_GROUP, :]
        w_hi, w_lo = _split_bf16(w)
        w_lo2 = (w - w_hi.astype(F32) - w_lo.astype(F32)).astype(BF16)
        used = blk_ref[N_GROUPS - 1] + blk_ref[2 * N_GROUPS - 1]
        for j in range(nblk):
            @pl.when(j < used)
            def _():
                p = perm_block(j)
                xs_ref[j] = _dot_nt(xt, p).astype(BF16)
                cs_ref[j] = _dot_nt(w_hi, p) + (_dot_nt(w_lo, p) + _dot_nt(w_lo2, p))
                ys_ref[j] = jnp.zeros((D_MODEL, MOE_BLK), F32)

    first = blk_ref[g]

    def block_body(jj, carry):
        j = first + jj
        h = _dot(w13_ref[0], xs_ref[j])
        cw = cs_ref[j]
        acts = []
        for e in range(MOE_HALF):
            a = h[e * 2 * EXPERT_FF:e * 2 * EXPERT_FF + EXPERT_FF]
            u = h[e * 2 * EXPERT_FF + EXPERT_FF:(e + 1) * 2 * EXPERT_FF]
            c = jnp.where(hf == 0, cw[e:e + 1, :], cw[MOE_HALF + e:MOE_HALF + e + 1, :])
            acts.append(((a * jax.nn.sigmoid(a)) * u * c).astype(BF16))
        ys_ref[j] += _dot(w2_ref[0], jnp.concatenate(acts, axis=0))
        return carry

    lax.fori_loop(0, blk_ref[N_GROUPS + g], block_body, 0)

    @pl.when((g == N_GROUPS - 1) & (hf == 1))
    def _():
        used = blk_ref[N_GROUPS - 1] + blk_ref[2 * N_GROUPS - 1]
        yt_ref[...] = jnp.zeros((D_MODEL, tm), F32)
        for j in range(nblk):
            @pl.when(j < used)
            def _():
                yt_ref[...] += _dot(ys_ref[j].astype(BF16), perm_block(j))
        o_ref[...] = _layer_norm(alpha * x_ref[...] + yt_ref[...].T, g_ref[...], b_ref[...])


def _moe(x1, route, w13t, w2t, g, b, alpha, tm):
    T = x1.shape[0]
    nblk = tm // MOE_BLK + N_GROUPS
    kern = functools.partial(_moe_kernel, alpha=alpha, tm=tm, nblk=nblk)
    return pl.pallas_call(
        kern,
        grid=(T // tm, N_GROUPS, 2),
        in_specs=[pl.BlockSpec((tm, D_MODEL), lambda i, gi, h: (i, 0)),
                  pl.BlockSpec((tm, LANES), lambda i, gi, h: (i, 0)),
                  pl.BlockSpec((1, MOE_HALF * 2 * EXPERT_FF, D_MODEL), lambda i, gi, h: (gi * 2 + h, 0, 0)),
                  pl.BlockSpec((1, D_MODEL, MOE_HALF * EXPERT_FF), lambda i, gi, h: (gi * 2 + h, 0, 0)),
                  pl.BlockSpec((1, D_MODEL), lambda i, gi, h: (0, 0)),
                  pl.BlockSpec((1, D_MODEL), lambda i, gi, h: (0, 0))],
        out_specs=pl.BlockSpec((tm, D_MODEL), lambda i, gi, h: (i, 0)),
        out_shape=jax.ShapeDtypeStruct((T, D_MODEL), F32),
        scratch_shapes=[pltpu.VMEM((nblk, D_MODEL, MOE_BLK), BF16),
                        pltpu.VMEM((nblk, D_MODEL, MOE_BLK), F32),
                        pltpu.VMEM((nblk, EXPERTS_PER_GROUP, MOE_BLK), F32),
                        pltpu.VMEM((1, tm), I32),
                        pltpu.VMEM((D_MODEL, tm), F32),
                        pltpu.SMEM((2 * N_GROUPS,), I32)],
        compiler_params=_params("parallel", "arbitrary", "arbitrary"),
        name="moe_experts_ln",
    )(x1, route, w13t, w2t, g, b)


def _pad_cols(a, width):
    return jnp.pad(a, ((0, 0), (0, width - a.shape[1])))


def _pad_heads(w, heads, dim, dim_pad):
    rows = w.shape[0]
    return jnp.pad(w.reshape(rows, heads, dim), ((0, 0), (0, 0), (0, dim_pad - dim))).reshape(rows, heads * dim_pad)


def _tile(n, pref):
    t = pref
    while n % t:
        t //= 2
    return t


def kernel(x, mem, positions, dsa_w_in, dsa_idx_k_g, dsa_idx_k_b, gla_w_in, gla_w_gate, gla_b_gate, gla_norm_g,
           w_mem_kv, w_out, ln1_g, ln1_b, ln2_g, ln2_b, moe_w_group, moe_b_group, moe_w_router, moe_b_router,
           moe_w13, moe_w2):
    B, S, D = x.shape
    T = B * S
    depth = w_out.shape[0]
    alpha = (2 * depth) ** 0.25
    tm = _tile(T, 512)
    tq = _tile(S, 512)
    tq_idx = _tile(S, 256)
    topk = min(DSA_MAX_TOPK, S // 4)

    inv = ROPE_THETA ** (-jnp.arange(0, HEAD_DIM, 2, dtype=F32) / HEAD_DIM)
    ang = positions.astype(F32).reshape(T, 1) * inv
    cos, sin = jnp.cos(ang), jnp.sin(ang)
    cosf = jnp.concatenate([cos, cos, cos, cos], axis=1)
    sinf = jnp.concatenate([-sin, sin, -sin, sin], axis=1)

    xc = x.reshape(T, D)
    ia = ib = 0
    for i in range(depth):
        if i % 2 == 0:
            w = dsa_w_in[ia]
            wq, wk, wv, wqi, wki, wwi, wqm = jnp.split(w, [768, 1536, 2304, 2816, 2880, 2888], axis=1)
            w_tok = jnp.concatenate([wk, wqm, _pad_cols(wki, LANES)], axis=1).astype(BF16)
            w_feat = jnp.concatenate([wq, wv, wqi, _pad_cols(wwi, 16)], axis=1).T.astype(BF16)
            lng = _pad_cols(dsa_idx_k_g[ia][None, :], LANES)
            lnb = _pad_cols(dsa_idx_k_b[ia][None, :], LANES)
            k, qm, ki, q_t, v_t, qi_t, wi_t = _proj_dsa(xc, w_tok, w_feat, cosf, sinf, cosf.T, sinf.T, lng, lnb,
                                                         tm, tq, tq_idx)
            bias = _idx_mask(qi_t, ki.reshape(B, S, LANES), wi_t, B, S, topk, tq_idx)
            seq = _dsa_attn(q_t, k.reshape(B, S, SEQ_WIDTH), v_t, bias, B, S, tq).reshape(T, SEQ_WIDTH)
            wa = w_out[i][:SEQ_WIDTH].astype(BF16)
            ia += 1
        else:
            w = gla_w_in[ib]
            wq, wk, wv, wr_, wa1, wqm = jnp.split(w, [384, 768, 1536, 2304, 2320], axis=1)
            w_all = jnp.concatenate([
                _pad_heads(wq, GLA_HEADS, GLA_DK, GLA_DKP), _pad_heads(wk, GLA_HEADS, GLA_DK, GLA_DKP),
                _pad_heads(wv, GLA_HEADS, GLA_DV, GLA_DVP), _pad_heads(wr_, GLA_HEADS, GLA_DV, GLA_DVP),
                wqm, _pad_cols(wa1, LANES)], axis=1).astype(BF16)
            q, k, v, r, qm, a1 = _proj_gla(xc, w_all, tm)
            r3 = lambda a: a.reshape(B, S, a.shape[1])
            wg = _pad_heads(gla_w_gate[ib], GLA_HEADS, GLA_DK, GLA_DKP)
            wg = jnp.pad(wg, ((0, LANES - GLA_GATE_RANK), (0, 0)))
            wg = wg.reshape(LANES, GLA_HEADS, GLA_DKP).transpose(1, 0, 2)
            bg = _pad_heads(gla_b_gate[ib][None, :], GLA_HEADS, GLA_DK, GLA_DKP).reshape(GLA_HEADS, 1, GLA_DKP)
            ng = _pad_cols(gla_norm_g[ib][None, :], GLA_DVP)
            seq = _gla(r3(q), r3(k), r3(v), r3(r), r3(a1), wg, bg, ng).reshape(T, GLA_HEADS * GLA_DVP)
            wa = w_out[i][:SEQ_WIDTH].reshape(GLA_HEADS, GLA_DV, D)
            wa = jnp.pad(wa, ((0, 0), (0, GLA_DVP - GLA_DV), (0, 0))).reshape(GLA_HEADS * GLA_DVP, D).astype(BF16)
            ib += 1
        memo = _mem_attn(qm.reshape(B, S, MEM_WIDTH), mem, w_mem_kv[i].astype(BF16), tq).reshape(T, MEM_WIDTH)
        wb = w_out[i][SEQ_WIDTH:].astype(BF16)
        wr = jnp.concatenate([moe_w_group[i], moe_w_router[i].transpose(1, 0, 2).reshape(D, N_EXPERTS)], axis=1)
        wr = _pad_cols(wr, LANES)
        br = _pad_cols(jnp.concatenate([moe_b_group[i], moe_b_router[i].reshape(-1)])[None, :], LANES)
        x1, route = _mix_router(seq, memo, wa, wb, xc, ln1_g[i][None, :], ln1_b[i][None, :], wr, br, alpha, tm)
        w13t = moe_w13[i].transpose(0, 1, 3, 2).reshape(N_GROUPS * 2, MOE_HALF * 2 * EXPERT_FF, D).astype(BF16)
        w2t = moe_w2[i].reshape(N_GROUPS * 2, MOE_HALF * EXPERT_FF, D).transpose(0, 2, 1).astype(BF16)
        xc = _moe(x1, route, w13t, w2t, ln2_g[i][None, :], ln2_b[i][None, :], alpha, _tile(T, 1024))
    return xc.reshape(B, S, D)
```

```python
import functools
import math

import jax
import jax.numpy as jnp
from jax import lax
from jax.experimental import pallas as pl
from jax.experimental.pallas import tpu as pltpu

F32 = jnp.float32
BF16 = jnp.bfloat16
I32 = jnp.int32
I16 = jnp.int16

LANES = 128
D_MODEL = 1024
HEAD_DIM = 64
N_MEM_HEADS = 4
MEM_WIDTH = N_MEM_HEADS * HEAD_DIM
SEQ_WIDTH = D_MODEL - MEM_WIDTH
ROPE_THETA = 10000.0
DSA_HEADS = SEQ_WIDTH // HEAD_DIM
IDX_HEADS = 8
IDX_DIM = 64
DSA_MAX_TOPK = 256
GLA_HEADS = 4
GLA_DV = SEQ_WIDTH // GLA_HEADS
GLA_DK = GLA_DV // 2
GLA_DKP = 128
GLA_DVP = 256
GLA_GATE_RANK = 16
GLA_TAU = 16.0
GLA_CHUNK = 64
N_GROUPS = 4
EXPERTS_PER_GROUP = 8
N_EXPERTS = N_GROUPS * EXPERTS_PER_GROUP
EXPERT_FF = 256
LN_EPS = 1e-5
RMS_EPS = 1e-6
MASK_BIAS = -1e30
INT_MIN = -2 ** 31
VMEM_LIMIT = 56 * 1024 * 1024


def _dot(a, b):
    return jnp.dot(a, b, preferred_element_type=F32)


def _dot_nt(a, b):
    return lax.dot_general(a, b, (((1,), (1,)), ((), ())), preferred_element_type=F32)


def _dot_tn(a, b):
    return lax.dot_general(a, b, (((0,), (0,)), ((), ())), preferred_element_type=F32)


def _split_bf16(a):
    hi = a.astype(BF16)
    lo = (a - hi.astype(F32)).astype(BF16)
    return hi, lo


def _dot_f32(a, b):
    ah, al = _split_bf16(a)
    bh, bl = _split_bf16(b)
    return _dot(ah, bh) + (_dot(ah, bl) + _dot(al, bh))


def _layer_norm(y, g, b):
    mu = jnp.mean(y, axis=-1, keepdims=True)
    yc = y - mu
    var = jnp.mean(yc * yc, axis=-1, keepdims=True)
    return yc * lax.rsqrt(var + LN_EPS) * g + b


def _params(*sem):
    return pltpu.CompilerParams(dimension_semantics=sem, vmem_limit_bytes=VMEM_LIMIT)


def _rope_fn(cos, sin, axis):
    shape = (1, LANES) if axis == 1 else (LANES, 1)
    pos = lax.broadcasted_iota(I32, shape, axis)
    first_half = (pos % HEAD_DIM) < (HEAD_DIM // 2)

    def rope(a):
        swapped = jnp.where(first_half, pltpu.roll(a, LANES - HEAD_DIM // 2, axis),
                            pltpu.roll(a, HEAD_DIM // 2, axis))
        return a * cos + swapped * sin
    return rope


def _proj_dsa_kernel(x_ref, wa_ref, wb_ref, cos_ref, sin_ref, cost_ref, sint_ref, lng_ref, lnb_ref,
                     k_ref, qm_ref, ki_ref, qt_ref, vt_ref, qit_ref, wit_ref):
    xb = x_ref[...].astype(BF16)
    rope = _rope_fn(cos_ref[...], sin_ref[...], 1)
    rope_t = _rope_fn(cost_ref[...], sint_ref[...], 0)
    tok = _dot(xb, wa_ref[...])
    for c in range(SEQ_WIDTH // LANES):
        k_ref[:, c * LANES:(c + 1) * LANES] = rope(tok[:, c * LANES:(c + 1) * LANES]).astype(BF16)
    qm_ref[...] = tok[:, SEQ_WIDTH:SEQ_WIDTH + MEM_WIDTH].astype(BF16)
    acc = tok[:, D_MODEL:D_MODEL + LANES]
    lane = lax.broadcasted_iota(I32, (1, LANES), 1)
    is_ki = lane < IDX_DIM
    mu = jnp.sum(jnp.where(is_ki, acc, 0.0), axis=1, keepdims=True) * (1.0 / IDX_DIM)
    d = jnp.where(is_ki, acc - mu, 0.0)
    var = jnp.sum(d * d, axis=1, keepdims=True) * (1.0 / IDX_DIM)
    ki_ref[...] = rope(d * lax.rsqrt(var + LN_EPS) * lng_ref[...] + lnb_ref[...]).astype(BF16)

    def store_t(out_ref, r, val):
        nblk, _, tb = out_ref.shape
        rows = val.shape[0]
        for j in range(nblk):
            out_ref[j, r * rows:(r + 1) * rows, :] = val[:, j * tb:(j + 1) * tb].astype(out_ref.dtype)

    q_scale = HEAD_DIM ** -0.5 * math.log2(math.e)
    feat = _dot_nt(wb_ref[...], xb)
    for r in range(SEQ_WIDTH // LANES):
        store_t(qt_ref, r, rope_t(feat[r * LANES:(r + 1) * LANES]) * q_scale)
    off = SEQ_WIDTH
    for r in range(SEQ_WIDTH // LANES):
        store_t(vt_ref, r, feat[off + r * LANES:off + (r + 1) * LANES])
    off = 2 * SEQ_WIDTH
    for r in range(IDX_HEADS * IDX_DIM // LANES):
        store_t(qit_ref, r, rope_t(feat[off + r * LANES:off + (r + 1) * LANES]))
    off = 2 * SEQ_WIDTH + IDX_HEADS * IDX_DIM
    store_t(wit_ref, 0, feat[off:off + IDX_HEADS] * (IDX_HEADS ** -0.5 * IDX_DIM ** -0.5))


def _proj_dsa(x2d, wa, wb, cosf, sinf, cost, sint, lng, lnb, tm, tq_att, tq_idx):
    T = x2d.shape[0]
    row = lambda n: pl.BlockSpec((tm, n), lambda i: (i, 0))
    col = pl.BlockSpec((LANES, tm), lambda i: (0, i))
    full = lambda a: pl.BlockSpec(a.shape, lambda i: (0,) * a.ndim)
    featmaj = lambda n, tb: pl.BlockSpec((tm // tb, n, tb), lambda i: (i, 0, 0))
    fshape = lambda n, tb, dt: jax.ShapeDtypeStruct((T // tb, n, tb), dt)
    return pl.pallas_call(
        _proj_dsa_kernel,
        grid=(T // tm,),
        in_specs=[row(D_MODEL), full(wa), full(wb), row(LANES), row(LANES), col, col, full(lng), full(lnb)],
        out_specs=[row(SEQ_WIDTH), row(MEM_WIDTH), row(LANES),
                   featmaj(SEQ_WIDTH, tq_att), featmaj(SEQ_WIDTH, tq_att),
                   featmaj(IDX_HEADS * IDX_DIM, tq_idx), featmaj(IDX_HEADS, tq_idx)],
        out_shape=[jax.ShapeDtypeStruct((T, SEQ_WIDTH), BF16), jax.ShapeDtypeStruct((T, MEM_WIDTH), BF16),
                   jax.ShapeDtypeStruct((T, LANES), BF16),
                   fshape(SEQ_WIDTH, tq_att, BF16), fshape(SEQ_WIDTH, tq_att, BF16),
                   fshape(IDX_HEADS * IDX_DIM, tq_idx, BF16), fshape(IDX_HEADS, tq_idx, F32)],
        compiler_params=_params("parallel"),
        name="proj_dsa",
    )(x2d, wa, wb, cosf, sinf, cost, sint, lng, lnb)


def _proj_gla_kernel(x_ref, w_ref, q_ref, k_ref, v_ref, r_ref, qm_ref, a_ref):
    xb = x_ref[...].astype(BF16)
    segments = ((0, 512, q_ref), (512, 512, k_ref), (1024, 1024, v_ref), (2048, 1024, r_ref),
                (3072, 256, qm_ref), (3328, 128, a_ref))
    for off, width, out_ref in segments:
        out_ref[...] = _dot(xb, w_ref[:, off:off + width]).astype(out_ref.dtype)


def _proj_gla(x2d, w, tm):
    T = x2d.shape[0]
    row = lambda n: pl.BlockSpec((tm, n), lambda i: (i, 0))
    outs = [(512, BF16), (512, BF16), (1024, BF16), (1024, BF16), (256, BF16), (LANES, F32)]
    return pl.pallas_call(
        _proj_gla_kernel,
        grid=(T // tm,),
        in_specs=[row(D_MODEL), pl.BlockSpec(w.shape, lambda i: (0, 0))],
        out_specs=[row(n) for n, _ in outs],
        out_shape=[jax.ShapeDtypeStruct((T, n), dt) for n, dt in outs],
        compiler_params=_params("parallel"),
        name="proj_gla",
    )(x2d, w)


def _idx_kernel(qi_ref, ki_ref, wi_ref, bias_ref, keys_ref, hi_ref, lo_ref, cut_ref, *, tq, tk, nkb, topk, seq):
    qt = pl.program_id(1)
    n_act = qt + 1
    key_l = lax.broadcasted_iota(I32, (tk, tq), 0)
    qry_g = qt * tq + lax.broadcasted_iota(I32, (tk, tq), 1)
    wi = wi_ref[0]

    def score_body(kb, carry):
        kblk = ki_ref[0, pl.ds(pl.multiple_of(kb * tk, tk), tk), :][:, :IDX_DIM]
        sc = jnp.zeros((tk, tq), F32)
        for h in range(IDX_HEADS):
            qh = qi_ref[0, h * IDX_DIM:(h + 1) * IDX_DIM, :]
            sc = sc + jnp.maximum(_dot(kblk, qh), 0.0) * wi[h:h + 1, :]
        bits = lax.bitcast_convert_type(sc, I32)
        key = bits ^ ((bits >> 31) & 0x7FFFFFFF)
        key = jnp.where(kb * tk + key_l > qry_g, INT_MIN, key)
        keys_ref[kb] = key
        hi_ref[kb] = (key >> 16).astype(I16)
        lo_ref[kb] = (key ^ 0x8000).astype(I16)
        return carry

    lax.fori_loop(0, n_act, score_body, 0)

    def count16(ref, cand, strict=False):
        cand16 = cand.astype(I16)

        def body(kb, acc):
            v = ref[kb]
            m = jnp.where((v > cand16) if strict else (v >= cand16), jnp.int16(1), jnp.int16(0))
            parts = [m[j * 16:(j + 1) * 16] for j in range(4)]
            for j in range(4, tk // 16):
                parts[j % 4] = parts[j % 4] + m[j * 16:(j + 1) * 16]
            return acc + ((parts[0] + parts[1]) + (parts[2] + parts[3]))
        acc = lax.fori_loop(0, n_act, body, jnp.zeros((16, tq), I16))
        return jnp.sum(acc.astype(I32).astype(F32), axis=0, keepdims=True)

    def search16(ref, need):
        def bit_body(i, base_u):
            cand_u = base_u | lax.shift_left(jnp.int32(1), lax.convert_element_type(15 - i, I32))
            cnt = count16(ref, cand_u - 32768)
            return jnp.where(cnt >= need, cand_u, base_u)
        return lax.fori_loop(0, 16, bit_body, jnp.zeros((1, tq), I32)) - 32768

    base_hi = search16(hi_ref, topk)
    above = count16(hi_ref, base_hi, strict=True)
    base_hi16 = base_hi.astype(I16)

    def bucket_body(kb, carry):
        lo_ref[kb] = jnp.where(hi_ref[kb] == base_hi16, lo_ref[kb], jnp.int16(-32768))
        return carry

    lax.fori_loop(0, n_act, bucket_body, 0)
    base_lo = search16(lo_ref, topk - above)
    base = lax.shift_left(base_hi, jnp.int32(16)) | (base_lo + 32768)

    def count(pred):
        def body(kb, acc):
            m = pred(keys_ref[kb], kb).astype(F32)
            parts = [m[j * 8:(j + 1) * 8] for j in range(4)]
            for j in range(4, tk // 8):
                parts[j % 4] = parts[j % 4] + m[j * 8:(j + 1) * 8]
            return acc + ((parts[0] + parts[1]) + (parts[2] + parts[3]))
        acc = lax.fori_loop(0, n_act, body, jnp.zeros((8, tq), F32))
        return jnp.sum(acc, axis=0, keepdims=True)

    cnt_gt = count(lambda kk, kb: kk > base)
    cnt_ge = count(lambda kk, kb: kk >= base)
    need = topk - cnt_gt
    tie = (cnt_ge > topk) & (base != INT_MIN)
    cut_ref[...] = jnp.full((1, tq), seq, I32)

    @pl.when(jnp.max(tie.astype(F32)) > 0.0)
    def _():
        nbits = int(math.log2(seq))

        def idx_body(i, m):
            cand = m | lax.shift_left(jnp.int32(1), lax.convert_element_type(nbits - 1 - i, I32))
            below = count(lambda kk, kb: (kk == base) & (kb * tk + key_l < cand))
            return jnp.where(below < need, cand, m)

        m = lax.fori_loop(0, nbits, idx_body, jnp.zeros((1, tq), I32))
        cut_ref[...] = jnp.where(tie, m, seq)

    cut = cut_ref[...]
    for kb in range(nkb):
        @pl.when(kb <= qt)
        def _():
            kk = keys_ref[kb]
            sel = ((kk > base) | ((kk == base) & (kb * tk + key_l <= cut))) & (kk != INT_MIN)
            bias_ref[0, 0, kb] = jnp.where(sel, F32(0.0), F32(MASK_BIAS)).astype(BF16)

        @pl.when(kb > qt)
        def _():
            bias_ref[0, 0, kb] = jnp.full((tk, tq), MASK_BIAS, BF16)


def _idx_mask(qit, ki, wit, B, S, topk, tq):
    nq = S // tq
    kern = functools.partial(_idx_kernel, tq=tq, tk=tq, nkb=nq, topk=topk, seq=S)
    return pl.pallas_call(
        kern,
        grid=(B, nq),
        in_specs=[pl.BlockSpec((1, IDX_HEADS * IDX_DIM, tq), lambda b, q: (b * nq + q, 0, 0)),
                  pl.BlockSpec((1, S, LANES), lambda b, q: (b, 0, 0)),
                  pl.BlockSpec((1, IDX_HEADS, tq), lambda b, q: (b * nq + q, 0, 0))],
        out_specs=pl.BlockSpec((1, 1, nq, tq, tq), lambda b, q: (b, q, 0, 0, 0)),
        out_shape=jax.ShapeDtypeStruct((B, nq, nq, tq, tq), BF16),
        scratch_shapes=[pltpu.VMEM((nq, tq, tq), I32), pltpu.VMEM((nq, tq, tq), I16), pltpu.VMEM((nq, tq, tq), I16),
                        pltpu.VMEM((1, tq), I32)],
        compiler_params=_params("parallel", "arbitrary"),
        name="dsa_index_select",
    )(qit, ki, wit)


def _dsa_attn_kernel(q_ref, k_ref, v_ref, bias_ref, o_ref, m_ref, acc_ref, *, tq, tk, sub):
    qt = pl.program_id(2)
    nsub = tq // sub
    low = lax.broadcasted_iota(I32, (LANES, 1), 0) < HEAD_DIM
    q = q_ref[0]
    zero = jnp.zeros_like(q)
    q_heads = (jnp.where(low, q, zero), jnp.where(low, zero, q))
    m_ref[...] = jnp.full(m_ref.shape, -jnp.inf, F32)
    acc_ref[...] = jnp.zeros(acc_ref.shape, F32)

    def body(kb, carry):
        kblk = k_ref[0, pl.ds(pl.multiple_of(kb * tk, tk), tk), :]
        vt = v_ref[kb]
        one = jnp.ones_like(vt)
        v_heads = (jnp.where(low, vt, one), jnp.where(low, one, vt))
        bias = jnp.concatenate(
            [jnp.concatenate([bias_ref[0, c, kb * nsub + a] for c in range(nsub)], axis=1) for a in range(nsub)],
            axis=0).astype(F32)
        s = [_dot(kblk, q_heads[h]) + bias for h in range(2)]
        m_old = [m_ref[h] for h in range(2)]
        m_new = [jnp.maximum(m_old[h], jnp.max(s[h], axis=0, keepdims=True)) for h in range(2)]
        p = [jnp.exp2(s[h] - m_new[h]).astype(BF16) for h in range(2)]
        pv = [_dot(v_heads[h], p[h]) for h in range(2)]
        for h in range(2):
            acc_ref[h] = jnp.exp2(m_old[h] - m_new[h]) * acc_ref[h] + pv[h]
            m_ref[h] = m_new[h]
        return carry

    lax.fori_loop(0, qt + 1, body, 0)
    a0 = acc_ref[0]
    a1 = acc_ref[1]
    o = jnp.where(low, a0 / a0[HEAD_DIM:HEAD_DIM + 1, :], a1 / a1[0:1, :])
    o_ref[0] = o.T.astype(o_ref.dtype)


def _dsa_attn(qt, k, vt, bias, B, S, tq):
    nq = S // tq
    npair = SEQ_WIDTH // LANES
    nsb, sub = bias.shape[2], bias.shape[3]
    kern = functools.partial(_dsa_attn_kernel, tq=tq, tk=tq, sub=sub)
    return pl.pallas_call(
        kern,
        grid=(B, npair, nq),
        in_specs=[pl.BlockSpec((1, LANES, tq), lambda b, h, i: (b * nq + i, h, 0)),
                  pl.BlockSpec((1, S, LANES), lambda b, h, i: (b, 0, h)),
                  pl.BlockSpec((nq, LANES, tq), lambda b, h, i: (b, h, 0)),
                  pl.BlockSpec((1, tq // sub, nsb, sub, sub), lambda b, h, i: (b, i, 0, 0, 0))],
        out_specs=pl.BlockSpec((1, tq, LANES), lambda b, h, i: (b, i, h)),
        out_shape=jax.ShapeDtypeStruct((B, S, SEQ_WIDTH), BF16),
        scratch_shapes=[pltpu.VMEM((2, 1, tq), F32), pltpu.VMEM((2, LANES, tq), F32)],
        compiler_params=_params("parallel", "parallel", "arbitrary"),
        name="dsa_attention",
    )(qt, k, vt, bias)


def _mem_attn_kernel(qm_ref, mem_ref, wkv_ref, o_ref, k_scr, v_scr):
    @pl.when(pl.program_id(1) == 0)
    def _():
        kv = _dot(mem_ref[0].astype(BF16), wkv_ref[...])
        k_scr[...] = kv[:, :MEM_WIDTH].astype(BF16)
        v_scr[...] = kv[:, MEM_WIDTH:].astype(BF16)

    lane = lax.broadcasted_iota(I32, (1, LANES), 1)
    low = lane < HEAD_DIM
    scale = HEAD_DIM ** -0.5
    for pair in range(MEM_WIDTH // LANES):
        cols = slice(pair * LANES, (pair + 1) * LANES)
        q = qm_ref[0, :, cols]
        zero = jnp.zeros_like(q)
        kp = k_scr[:, cols]
        vp = v_scr[:, cols]
        outs = []
        for qh in (jnp.where(low, q, zero), jnp.where(low, zero, q)):
            s = _dot_nt(qh, kp) * scale
            s = s - jnp.max(s, axis=1, keepdims=True)
            p = jnp.exp(s)
            p = p / jnp.sum(p, axis=1, keepdims=True)
            outs.append(_dot(p.astype(BF16), vp))
        o_ref[0, :, cols] = jnp.where(low, outs[0], outs[1]).astype(o_ref.dtype)


def _mem_attn(qm, mem, wkv, tq):
    B, S, _ = qm.shape
    M = mem.shape[1]
    return pl.pallas_call(
        _mem_attn_kernel,
        grid=(B, S // tq),
        in_specs=[pl.BlockSpec((1, tq, MEM_WIDTH), lambda b, i: (b, i, 0)),
                  pl.BlockSpec((1, M, D_MODEL), lambda b, i: (b, 0, 0)),
                  pl.BlockSpec(wkv.shape, lambda b, i: (0, 0))],
        out_specs=pl.BlockSpec((1, tq, MEM_WIDTH), lambda b, i: (b, i, 0)),
        out_shape=jax.ShapeDtypeStruct((B, S, MEM_WIDTH), BF16),
        scratch_shapes=[pltpu.VMEM((M, MEM_WIDTH), BF16), pltpu.VMEM((M, MEM_WIDTH), BF16)],
        compiler_params=_params("parallel", "arbitrary"),
        name="memory_attention",
    )(qm, mem, wkv)


def _gla_kernel(q_ref, k_ref, v_ref, r_ref, a_ref, wg_ref, bg_ref, ng_ref, o_ref,
                qb_ref, kb_ref, kd_ref, dl_ref, oi_ref, kv_ref, *, nchunk):
    C = GLA_CHUNK
    S = nchunk * C
    z = _dot_f32(a_ref[0], wg_ref[0]) + bg_ref[0]
    b = (jnp.minimum(z, 0.0) - jnp.log1p(jnp.exp(-jnp.abs(z)))) * (1.0 / GLA_TAU)
    pos = lax.broadcasted_iota(I32, (S, 1), 0) % C
    shift = 1
    while shift < C:
        b = b + jnp.where(pos >= shift, pltpu.roll(b, shift, 0), 0.0)
        shift *= 2
    b3 = b.reshape(nchunk, C, GLA_DKP)
    b_last = b3[:, C - 1:C, :]
    q = q_ref[0].astype(F32) * (GLA_DK ** -0.5)
    k = k_ref[0].astype(F32)
    qb_ref[...] = (q * jnp.exp(b)).astype(BF16)
    kb_ref[...] = (k * jnp.exp(-b)).astype(BF16)
    kd_ref[...] = (k.reshape(nchunk, C, GLA_DKP) * jnp.exp(b_last - b3)).reshape(S, GLA_DKP).astype(BF16)
    dl_ref[...] = jnp.exp(b_last)
    causal = lax.broadcasted_iota(I32, (C, C), 0) >= lax.broadcasted_iota(I32, (C, C), 1)

    def intra(c, carry):
        rows = pl.ds(pl.multiple_of(c * C, C), C)
        v = v_ref[0, rows, :]
        attn = jnp.where(causal, _dot_nt(qb_ref[rows, :], kb_ref[rows, :]), 0.0)
        oi_ref[rows, :] = _dot(attn.astype(BF16), v)
        kv_ref[c] = _dot_tn(v, kd_ref[rows, :])
        return carry

    lax.fori_loop(0, nchunk, intra, 0, unroll=4)
    ng = ng_ref[...]

    def inter(c, st):
        rows = pl.ds(pl.multiple_of(c * C, C), C)
        o = oi_ref[rows, :] + _dot_nt(qb_ref[rows, :], st.astype(BF16))
        ms = jnp.sum(o * o, axis=1, keepdims=True) * (1.0 / GLA_DV)
        o = o * lax.rsqrt(ms + RMS_EPS) * ng
        r = r_ref[0, rows, :].astype(F32)
        o_ref[0, rows, :] = (o * (r * jax.nn.sigmoid(r))).astype(o_ref.dtype)
        return st * dl_ref[c] + kv_ref[c]

    lax.fori_loop(0, nchunk, inter, jnp.zeros((GLA_DVP, GLA_DKP), F32), unroll=4)


def _gla(q, k, v, r, a1, wg, bg, ng):
    B, S, _ = q.shape
    nchunk = S // GLA_CHUNK
    kern = functools.partial(_gla_kernel, nchunk=nchunk)
    kspec = pl.BlockSpec((1, S, GLA_DKP), lambda b, h: (b, 0, h))
    vspec = pl.BlockSpec((1, S, GLA_DVP), lambda b, h: (b, 0, h))
    return pl.pallas_call(
        kern,
        grid=(B, GLA_HEADS),
        in_specs=[kspec, kspec, vspec, vspec,
                  pl.BlockSpec((1, S, LANES), lambda b, h: (b, 0, 0)),
                  pl.BlockSpec((1, LANES, GLA_DKP), lambda b, h: (h, 0, 0)),
                  pl.BlockSpec((1, 1, GLA_DKP), lambda b, h: (h, 0, 0)),
                  pl.BlockSpec((1, GLA_DVP), lambda b, h: (0, 0))],
        out_specs=vspec,
        out_shape=jax.ShapeDtypeStruct((B, S, GLA_HEADS * GLA_DVP), BF16),
        scratch_shapes=[pltpu.VMEM((S, GLA_DKP), BF16), pltpu.VMEM((S, GLA_DKP), BF16), pltpu.VMEM((S, GLA_DKP), BF16),
                        pltpu.VMEM((nchunk, 1, GLA_DKP), F32), pltpu.VMEM((S, GLA_DVP), F32),
                        pltpu.VMEM((nchunk, GLA_DVP, GLA_DKP), F32)],
        compiler_params=_params("parallel", "parallel"),
        name="gla",
    )(q, k, v, r, a1, wg, bg, ng)


def _mix_router_kernel(seq_ref, memo_ref, wa_ref, wb_ref, x_ref, g_ref, b_ref, wr_ref, br_ref,
                       x1_ref, comb_ref, *, alpha):
    mixed = _dot(seq_ref[...], wa_ref[...]) + _dot(memo_ref[...], wb_ref[...])
    x1 = _layer_norm(alpha * x_ref[...] + mixed, g_ref[...], b_ref[...])
    x1_ref[...] = x1
    logits = _dot_f32(x1, wr_ref[...]) + br_ref[...]
    lane = lax.broadcasted_iota(I32, logits.shape, 1)
    neg = -jnp.inf
    glog = jnp.where(lane < N_GROUPS, logits, neg)
    gmax = jnp.max(glog, axis=1, keepdims=True)
    gsel = jnp.min(jnp.where(glog == gmax, lane, LANES), axis=1, keepdims=True)
    pg = 1.0 / jnp.sum(jnp.exp(glog - gmax), axis=1, keepdims=True)
    lo = N_GROUPS + EXPERTS_PER_GROUP * gsel
    elog = jnp.where((lane >= lo) & (lane < lo + EXPERTS_PER_GROUP), logits, neg)
    v1 = jnp.max(elog, axis=1, keepdims=True)
    i1 = jnp.min(jnp.where(elog == v1, lane, LANES), axis=1, keepdims=True)
    elog2 = jnp.where(lane == i1, neg, elog)
    v2 = jnp.max(elog2, axis=1, keepdims=True)
    i2 = jnp.min(jnp.where(elog2 == v2, lane, LANES), axis=1, keepdims=True)
    e2 = jnp.exp(v2 - v1)
    den = 1.0 + e2
    comb_ref[...] = (jnp.where(lane == i1 - lo, pg / den, 0.0) + jnp.where(lane == i2 - lo, pg * e2 / den, 0.0)
                     + jnp.where(lane == EXPERTS_PER_GROUP, (gsel.astype(F32)), 0.0))


def _mix_router(seq, memo, wa, wb, x2d, g, b, wr, br, alpha, tm):
    T = x2d.shape[0]
    row = lambda n: pl.BlockSpec((tm, n), lambda i: (i, 0))
    full = lambda a: pl.BlockSpec(a.shape, lambda i: (0,) * a.ndim)
    kern = functools.partial(_mix_router_kernel, alpha=alpha)
    return pl.pallas_call(
        kern,
        grid=(T // tm,),
        in_specs=[row(seq.shape[1]), row(MEM_WIDTH), full(wa), full(wb), row(D_MODEL), full(g), full(b),
                  full(wr), full(br)],
        out_specs=[row(D_MODEL), row(LANES)],
        out_shape=[jax.ShapeDtypeStruct((T, D_MODEL), F32), jax.ShapeDtypeStruct((T, LANES), F32)],
        compiler_params=_params("parallel"),
        name="outproj_ln_router",
    )(seq, memo, wa, wb, x2d, g, b, wr, br)


MOE_BLK = 256
MOE_HALF = EXPERTS_PER_GROUP // 2


def _moe_kernel(x_ref, route_ref, w13_ref, w2_ref, g_ref, b_ref, o_ref,
                xs_ref, ys_ref, cs_ref, dest_ref, yt_ref, blk_ref, *, alpha, tm, nblk):
    g = pl.program_id(1)
    hf = pl.program_id(2)
    slot = lax.broadcasted_iota(I32, (MOE_BLK, tm), 0)

    def perm_block(j):
        return (slot + j * MOE_BLK == dest_ref[...]).astype(BF16)

    @pl.when((g == 0) & (hf == 0))
    def _():
        rt = route_ref[...].T
        gid = rt[EXPERTS_PER_GROUP:EXPERTS_PER_GROUP + 1, :]
        grp = lax.broadcasted_iota(I32, (8, 1), 0).astype(F32)
        onehot = (gid == grp).astype(F32)
        earlier = (lax.broadcasted_iota(I32, (tm, tm), 0) < lax.broadcasted_iota(I32, (tm, tm), 1)).astype(BF16)
        rank = _dot(onehot.astype(BF16), earlier)
        cnt = jnp.sum(onehot, axis=1, keepdims=True)
        nb = jnp.floor((cnt + (MOE_BLK - 1)) * (1.0 / MOE_BLK))
        sb = [jnp.zeros((1, 1), F32)]
        for k in range(1, N_GROUPS):
            sb.append(sb[-1] + nb[k - 1:k, :])
        for k in range(N_GROUPS):
            blk_ref[k] = jnp.max(sb[k]).astype(I32)
            blk_ref[N_GROUPS + k] = jnp.max(nb[k:k + 1, :]).astype(I32)
        start = jnp.concatenate(sb + [jnp.zeros((8 - N_GROUPS, 1), F32)], axis=0) * MOE_BLK
        dest_ref[...] = jnp.sum(onehot * (start + rank), axis=0, keepdims=True).astype(I32)
        xt = x_ref[...].T.astype(BF16)
        w = rt[0:EXPERTS_PER_GROUP, :]
        w_hi, w_lo = _split_bf16(w)
        w_lo2 = (w - w_hi.astype(F32) - w_lo.astype(F32)).astype(BF16)
        used = blk_ref[N_GROUPS - 1] + blk_ref[2 * N_GROUPS - 1]
        for j in range(nblk):
            @pl.when(j < used)
            def _():
                p = perm_block(j)
                xs_ref[j] = _dot_nt(xt, p).astype(BF16)
                cs_ref[j] = _dot_nt(w_hi, p) + (_dot_nt(w_lo, p) + _dot_nt(w_lo2, p))
                ys_ref[j] = jnp.zeros((D_MODEL, MOE_BLK), F32)

    first = blk_ref[g]

    def block_body(jj, carry):
        j = first + jj
        h = _dot(w13_ref[0], xs_ref[j])
        cw = cs_ref[j]
        acts = []
        for e in range(MOE_HALF):
            a = h[e * 2 * EXPERT_FF:e * 2 * EXPERT_FF + EXPERT_FF]
            u = h[e * 2 * EXPERT_FF + EXPERT_FF:(e + 1) * 2 * EXPERT_FF]
            c = jnp.where(hf == 0, cw[e:e + 1, :], cw[MOE_HALF + e:MOE_HALF + e + 1, :])
            acts.append(((a * jax.nn.sigmoid(a)) * u * c).astype(BF16))
        ys_ref[j] += _dot(w2_ref[0], jnp.concatenate(acts, axis=0))
        return carry

    lax.fori_loop(0, blk_ref[N_GROUPS + g], block_body, 0)

    @pl.when((g == N_GROUPS - 1) & (hf == 1))
    def _():
        used = blk_ref[N_GROUPS - 1] + blk_ref[2 * N_GROUPS - 1]
        yt_ref[...] = jnp.zeros((D_MODEL, tm), F32)
        for j in range(nblk):
            @pl.when(j < used)
            def _():
                yt_ref[...] += _dot(ys_ref[j].astype(BF16), perm_block(j))
        o_ref[...] = _layer_norm(alpha * x_ref[...] + yt_ref[...].T, g_ref[...], b_ref[...])


def _moe(x1, route, w13t, w2t, g, b, alpha, tm):
    T = x1.shape[0]
    nblk = tm // MOE_BLK + N_GROUPS
    kern = functools.partial(_moe_kernel, alpha=alpha, tm=tm, nblk=nblk)
    return pl.pallas_call(
        kern,
        grid=(T // tm, N_GROUPS, 2),
        in_specs=[pl.BlockSpec((tm, D_MODEL), lambda i, gi, h: (i, 0)),
                  pl.BlockSpec((tm, LANES), lambda i, gi, h: (i, 0)),
                  pl.BlockSpec((1, MOE_HALF * 2 * EXPERT_FF, D_MODEL), lambda i, gi, h: (gi * 2 + h, 0, 0)),
                  pl.BlockSpec((1, D_MODEL, MOE_HALF * EXPERT_FF), lambda i, gi, h: (gi * 2 + h, 0, 0)),
                  pl.BlockSpec((1, D_MODEL), lambda i, gi, h: (0, 0)),
                  pl.BlockSpec((1, D_MODEL), lambda i, gi, h: (0, 0))],
        out_specs=pl.BlockSpec((tm, D_MODEL), lambda i, gi, h: (i, 0)),
        out_shape=jax.ShapeDtypeStruct((T, D_MODEL), F32),
        scratch_shapes=[pltpu.VMEM((nblk, D_MODEL, MOE_BLK), BF16),
                        pltpu.VMEM((nblk, D_MODEL, MOE_BLK), F32),
                        pltpu.VMEM((nblk, EXPERTS_PER_GROUP, MOE_BLK), F32),
                        pltpu.VMEM((1, tm), I32),
                        pltpu.VMEM((D_MODEL, tm), F32),
                        pltpu.SMEM((2 * N_GROUPS,), I32)],
        compiler_params=_params("parallel", "arbitrary", "arbitrary"),
        name="moe_experts_ln",
    )(x1, route, w13t, w2t, g, b)


def _pad_cols(a, width):
    return jnp.pad(a, ((0, 0), (0, width - a.shape[1])))


def _pad_heads(w, heads, dim, dim_pad):
    rows = w.shape[0]
    return jnp.pad(w.reshape(rows, heads, dim), ((0, 0), (0, 0), (0, dim_pad - dim))).reshape(rows, heads * dim_pad)


def _tile(n, pref):
    t = pref
    while n % t:
        t //= 2
    return t


def kernel(x, mem, positions, dsa_w_in, dsa_idx_k_g, dsa_idx_k_b, gla_w_in, gla_w_gate, gla_b_gate, gla_norm_g,
           w_mem_kv, w_out, ln1_g, ln1_b, ln2_g, ln2_b, moe_w_group, moe_b_group, moe_w_router, moe_b_router,
           moe_w13, moe_w2):
    B, S, D = x.shape
    T = B * S
    depth = w_out.shape[0]
    alpha = (2 * depth) ** 0.25
    tm = _tile(T, 512)
    tq = _tile(S, 512)
    tq_idx = _tile(S, 256)
    topk = min(DSA_MAX_TOPK, S // 4)

    inv = ROPE_THETA ** (-jnp.arange(0, HEAD_DIM, 2, dtype=F32) / HEAD_DIM)
    ang = positions.astype(F32).reshape(T, 1) * inv
    cos, sin = jnp.cos(ang), jnp.sin(ang)
    cosf = jnp.concatenate([cos, cos, cos, cos], axis=1)
    sinf = jnp.concatenate([-sin, sin, -sin, sin], axis=1)

    xc = x.reshape(T, D)
    ia = ib = 0
    for i in range(depth):
        if i % 2 == 0:
            w = dsa_w_in[ia]
            wq, wk, wv, wqi, wki, wwi, wqm = jnp.split(w, [768, 1536, 2304, 2816, 2880, 2888], axis=1)
            w_tok = jnp.concatenate([wk, wqm, _pad_cols(wki, LANES)], axis=1).astype(BF16)
            w_feat = jnp.concatenate([wq, wv, wqi, _pad_cols(wwi, 16)], axis=1).T.astype(BF16)
            lng = _pad_cols(dsa_idx_k_g[ia][None, :], LANES)
            lnb = _pad_cols(dsa_idx_k_b[ia][None, :], LANES)
            k, qm, ki, q_t, v_t, qi_t, wi_t = _proj_dsa(xc, w_tok, w_feat, cosf, sinf, cosf.T, sinf.T, lng, lnb,
                                                         tm, tq, tq_idx)
            bias = _idx_mask(qi_t, ki.reshape(B, S, LANES), wi_t, B, S, topk, tq_idx)
            seq = _dsa_attn(q_t, k.reshape(B, S, SEQ_WIDTH), v_t, bias, B, S, tq).reshape(T, SEQ_WIDTH)
            wa = w_out[i][:SEQ_WIDTH].astype(BF16)
            ia += 1
        else:
            w = gla_w_in[ib]
            wq, wk, wv, wr_, wa1, wqm = jnp.split(w, [384, 768, 1536, 2304, 2320], axis=1)
            w_all = jnp.concatenate([
                _pad_heads(wq, GLA_HEADS, GLA_DK, GLA_DKP), _pad_heads(wk, GLA_HEADS, GLA_DK, GLA_DKP),
                _pad_heads(wv, GLA_HEADS, GLA_DV, GLA_DVP), _pad_heads(wr_, GLA_HEADS, GLA_DV, GLA_DVP),
                wqm, _pad_cols(wa1, LANES)], axis=1).astype(BF16)
            q, k, v, r, qm, a1 = _proj_gla(xc, w_all, tm)
            r3 = lambda a: a.reshape(B, S, a.shape[1])
            wg = _pad_heads(gla_w_gate[ib], GLA_HEADS, GLA_DK, GLA_DKP)
            wg = jnp.pad(wg, ((0, LANES - GLA_GATE_RANK), (0, 0)))
            wg = wg.reshape(LANES, GLA_HEADS, GLA_DKP).transpose(1, 0, 2)
            bg = _pad_heads(gla_b_gate[ib][None, :], GLA_HEADS, GLA_DK, GLA_DKP).reshape(GLA_HEADS, 1, GLA_DKP)
            ng = _pad_cols(gla_norm_g[ib][None, :], GLA_DVP)
            seq = _gla(r3(q), r3(k), r3(v), r3(r), r3(a1), wg, bg, ng).reshape(T, GLA_HEADS * GLA_DVP)
            wa = w_out[i][:SEQ_WIDTH].reshape(GLA_HEADS, GLA_DV, D)
            wa = jnp.pad(wa, ((0, 0), (0, GLA_DVP - GLA_DV), (0, 0))).reshape(GLA_HEADS * GLA_DVP, D).astype(BF16)
            ib += 1
        memo = _mem_attn(qm.reshape(B, S, MEM_WIDTH), mem, w_mem_kv[i].astype(BF16), tq).reshape(T, MEM_WIDTH)
        wb = w_out[i][SEQ_WIDTH:].astype(BF16)
        wr = jnp.concatenate([moe_w_group[i], moe_w_router[i].transpose(1, 0, 2).reshape(D, N_EXPERTS)], axis=1)
        wr = _pad_cols(wr, LANES)
        br = _pad_cols(jnp.concatenate([moe_b_group[i], moe_b_router[i].reshape(-1)])[None, :], LANES)
        x1, route = _mix_router(seq, memo, wa, wb, xc, ln1_g[i][None, :], ln1_b[i][None, :], wr, br, alpha, tm)
        w13t = moe_w13[i].transpose(0, 1, 3, 2).reshape(N_GROUPS * 2, MOE_HALF * 2 * EXPERT_FF, D).astype(BF16)
        w2t = moe_w2[i].reshape(N_GROUPS * 2, MOE_HALF * EXPERT_FF, D).transpose(0, 2, 1).astype(BF16)
        xc = _moe(x1, route, w13t, w2t, ln2_g[i][None, :], ln2_b[i][None, :], alpha, _tile(T, 1024))
    return xc.reshape(B, S, D)
```

```python
import functools
import math

import jax
import jax.numpy as jnp
from jax import lax
from jax.experimental import pallas as pl
from jax.experimental.pallas import tpu as pltpu

F32 = jnp.float32
BF16 = jnp.bfloat16
I32 = jnp.int32
I16 = jnp.int16

LANES = 128
D_MODEL = 1024
HEAD_DIM = 64
N_MEM_HEADS = 4
MEM_WIDTH = N_MEM_HEADS * HEAD_DIM
SEQ_WIDTH = D_MODEL - MEM_WIDTH
ROPE_THETA = 10000.0
DSA_HEADS = SEQ_WIDTH // HEAD_DIM
IDX_HEADS = 8
IDX_DIM = 64
DSA_MAX_TOPK = 256
GLA_HEADS = 4
GLA_DV = SEQ_WIDTH // GLA_HEADS
GLA_DK = GLA_DV // 2
GLA_DKP = 128
GLA_DVP = 256
GLA_GATE_RANK = 16
GLA_TAU = 16.0
GLA_CHUNK = 64
N_GROUPS = 4
EXPERTS_PER_GROUP = 8
N_EXPERTS = N_GROUPS * EXPERTS_PER_GROUP
EXPERT_FF = 256
LN_EPS = 1e-5
RMS_EPS = 1e-6
MASK_BIAS = -1e30
INT_MIN = -2 ** 31
VMEM_LIMIT = 56 * 1024 * 1024


def _dot(a, b):
    return jnp.dot(a, b, preferred_element_type=F32)


def _dot_nt(a, b):
    return lax.dot_general(a, b, (((1,), (1,)), ((), ())), preferred_element_type=F32)


def _dot_tn(a, b):
    return lax.dot_general(a, b, (((0,), (0,)), ((), ())), preferred_element_type=F32)


def _split_bf16(a):
    hi = a.astype(BF16)
    lo = (a - hi.astype(F32)).astype(BF16)
    return hi, lo


def _dot_f32(a, b):
    ah, al = _split_bf16(a)
    bh, bl = _split_bf16(b)
    return _dot(ah, bh) + (_dot(ah, bl) + _dot(al, bh))


def _layer_norm(y, g, b):
    mu = jnp.mean(y, axis=-1, keepdims=True)
    yc = y - mu
    var = jnp.mean(yc * yc, axis=-1, keepdims=True)
    return yc * lax.rsqrt(var + LN_EPS) * g + b


def _params(*sem):
    return pltpu.CompilerParams(dimension_semantics=sem, vmem_limit_bytes=VMEM_LIMIT)


def _rope_fn(cos, sin, axis):
    shape = (1, LANES) if axis == 1 else (LANES, 1)
    pos = lax.broadcasted_iota(I32, shape, axis)
    first_half = (pos % HEAD_DIM) < (HEAD_DIM // 2)

    def rope(a):
        swapped = jnp.where(first_half, pltpu.roll(a, LANES - HEAD_DIM // 2, axis),
                            pltpu.roll(a, HEAD_DIM // 2, axis))
        return a * cos + swapped * sin
    return rope


def _proj_dsa_kernel(x_ref, wa_ref, wb_ref, cos_ref, sin_ref, cost_ref, sint_ref, lng_ref, lnb_ref,
                     k_ref, qm_ref, ki_ref, qt_ref, vt_ref, qit_ref, wit_ref):
    xb = x_ref[...].astype(BF16)
    rope = _rope_fn(cos_ref[...], sin_ref[...], 1)
    rope_t = _rope_fn(cost_ref[...], sint_ref[...], 0)
    tok = _dot(xb, wa_ref[...])
    for c in range(SEQ_WIDTH // LANES):
        k_ref[:, c * LANES:(c + 1) * LANES] = rope(tok[:, c * LANES:(c + 1) * LANES]).astype(BF16)
    qm_ref[...] = tok[:, SEQ_WIDTH:SEQ_WIDTH + MEM_WIDTH].astype(BF16)
    acc = tok[:, D_MODEL:D_MODEL + LANES]
    lane = lax.broadcasted_iota(I32, (1, LANES), 1)
    is_ki = lane < IDX_DIM
    mu = jnp.sum(jnp.where(is_ki, acc, 0.0), axis=1, keepdims=True) * (1.0 / IDX_DIM)
    d = jnp.where(is_ki, acc - mu, 0.0)
    var = jnp.sum(d * d, axis=1, keepdims=True) * (1.0 / IDX_DIM)
    ki_ref[...] = rope(d * lax.rsqrt(var + LN_EPS) * lng_ref[...] + lnb_ref[...]).astype(BF16)

    def store_t(out_ref, r, val):
        nblk, _, tb = out_ref.shape
        rows = val.shape[0]
        for j in range(nblk):
            out_ref[j, r * rows:(r + 1) * rows, :] = val[:, j * tb:(j + 1) * tb].astype(out_ref.dtype)

    q_scale = HEAD_DIM ** -0.5 * math.log2(math.e)
    feat = _dot_nt(wb_ref[...], xb)
    for r in range(SEQ_WIDTH // LANES):
        store_t(qt_ref, r, rope_t(feat[r * LANES:(r + 1) * LANES]) * q_scale)
    off = SEQ_WIDTH
    for r in range(SEQ_WIDTH // LANES):
        store_t(vt_ref, r, feat[off + r * LANES:off + (r + 1) * LANES])
    off = 2 * SEQ_WIDTH
    for r in range(IDX_HEADS * IDX_DIM // LANES):
        store_t(qit_ref, r, rope_t(feat[off + r * LANES:off + (r + 1) * LANES]))
    off = 2 * SEQ_WIDTH + IDX_HEADS * IDX_DIM
    store_t(wit_ref, 0, feat[off:off + IDX_HEADS] * (IDX_HEADS ** -0.5 * IDX_DIM ** -0.5))


def _proj_dsa(x2d, wa, wb, cosf, sinf, cost, sint, lng, lnb, tm, tq_att, tq_idx):
    T = x2d.shape[0]
    row = lambda n: pl.BlockSpec((tm, n), lambda i: (i, 0))
    col = pl.BlockSpec((LANES, tm), lambda i: (0, i))
    full = lambda a: pl.BlockSpec(a.shape, lambda i: (0,) * a.ndim)
    featmaj = lambda n, tb: pl.BlockSpec((tm // tb, n, tb), lambda i: (i, 0, 0))
    fshape = lambda n, tb, dt: jax.ShapeDtypeStruct((T // tb, n, tb), dt)
    return pl.pallas_call(
        _proj_dsa_kernel,
        grid=(T // tm,),
        in_specs=[row(D_MODEL), full(wa), full(wb), row(LANES), row(LANES), col, col, full(lng), full(lnb)],
        out_specs=[row(SEQ_WIDTH), row(MEM_WIDTH), row(LANES),
                   featmaj(SEQ_WIDTH, tq_att), featmaj(SEQ_WIDTH, tq_att),
                   featmaj(IDX_HEADS * IDX_DIM, tq_idx), featmaj(IDX_HEADS, tq_idx)],
        out_shape=[jax.ShapeDtypeStruct((T, SEQ_WIDTH), BF16), jax.ShapeDtypeStruct((T, MEM_WIDTH), BF16),
                   jax.ShapeDtypeStruct((T, LANES), BF16),
                   fshape(SEQ_WIDTH, tq_att, BF16), fshape(SEQ_WIDTH, tq_att, BF16),
                   fshape(IDX_HEADS * IDX_DIM, tq_idx, BF16), fshape(IDX_HEADS, tq_idx, F32)],
        compiler_params=_params("parallel"),
        name="proj_dsa",
    )(x2d, wa, wb, cosf, sinf, cost, sint, lng, lnb)


def _proj_gla_kernel(x_ref, w_ref, q_ref, k_ref, v_ref, r_ref, qm_ref, a_ref):
    xb = x_ref[...].astype(BF16)
    segments = ((0, 512, q_ref), (512, 512, k_ref), (1024, 1024, v_ref), (2048, 1024, r_ref),
                (3072, 256, qm_ref), (3328, 128, a_ref))
    for off, width, out_ref in segments:
        out_ref[...] = _dot(xb, w_ref[:, off:off + width]).astype(out_ref.dtype)


def _proj_gla(x2d, w, tm):
    T = x2d.shape[0]
    row = lambda n: pl.BlockSpec((tm, n), lambda i: (i, 0))
    outs = [(512, BF16), (512, BF16), (1024, BF16), (1024, BF16), (256, BF16), (LANES, F32)]
    return pl.pallas_call(
        _proj_gla_kernel,
        grid=(T // tm,),
        in_specs=[row(D_MODEL), pl.BlockSpec(w.shape, lambda i: (0, 0))],
        out_specs=[row(n) for n, _ in outs],
        out_shape=[jax.ShapeDtypeStruct((T, n), dt) for n, dt in outs],
        compiler_params=_params("parallel"),
        name="proj_gla",
    )(x2d, w)


def _idx_kernel(qi_ref, ki_ref, wi_ref, bias_ref, keys_ref, hi_ref, lo_ref, cut_ref, *, tq, tk, nkb, topk, seq):
    qt = pl.program_id(1)
    n_act = qt + 1
    key_l = lax.broadcasted_iota(I32, (tk, tq), 0)
    qry_g = qt * tq + lax.broadcasted_iota(I32, (tk, tq), 1)
    wi = wi_ref[0]

    def score_body(kb, carry):
        kblk = ki_ref[0, pl.ds(pl.multiple_of(kb * tk, tk), tk), :][:, :IDX_DIM]
        sc = jnp.zeros((tk, tq), F32)
        for h in range(IDX_HEADS):
            qh = qi_ref[0, h * IDX_DIM:(h + 1) * IDX_DIM, :]
            sc = sc + jnp.maximum(_dot(kblk, qh), 0.0) * wi[h:h + 1, :]
        bits = lax.bitcast_convert_type(sc, I32)
        key = bits ^ ((bits >> 31) & 0x7FFFFFFF)
        key = jnp.where(kb * tk + key_l > qry_g, INT_MIN, key)
        keys_ref[kb] = key
        hi_ref[kb] = (key >> 16).astype(I16)
        lo_ref[kb] = (key ^ 0x8000).astype(I16)
        return carry

    lax.fori_loop(0, n_act, score_body, 0)

    def count16(ref, cand, strict=False):
        cand16 = cand.astype(I16)

        def body(kb, acc):
            v = ref[kb]
            m = jnp.where((v > cand16) if strict else (v >= cand16), jnp.int16(1), jnp.int16(0))
            parts = [m[j * 16:(j + 1) * 16] for j in range(4)]
            for j in range(4, tk // 16):
                parts[j % 4] = parts[j % 4] + m[j * 16:(j + 1) * 16]
            return acc + ((parts[0] + parts[1]) + (parts[2] + parts[3]))
        acc = lax.fori_loop(0, n_act, body, jnp.zeros((16, tq), I16))
        return jnp.sum(acc.astype(I32).astype(F32), axis=0, keepdims=True)

    def search16(ref, need):
        def bit_body(i, base_u):
            cand_u = base_u | lax.shift_left(jnp.int32(1), lax.convert_element_type(15 - i, I32))
            cnt = count16(ref, cand_u - 32768)
            return jnp.where(cnt >= need, cand_u, base_u)
        return lax.fori_loop(0, 16, bit_body, jnp.zeros((1, tq), I32)) - 32768

    base_hi = search16(hi_ref, topk)
    above = count16(hi_ref, base_hi, strict=True)
    base_hi16 = base_hi.astype(I16)

    def bucket_body(kb, carry):
        lo_ref[kb] = jnp.where(hi_ref[kb] == base_hi16, lo_ref[kb], jnp.int16(-32768))
        return carry

    lax.fori_loop(0, n_act, bucket_body, 0)
    base_lo = search16(lo_ref, topk - above)
    base = lax.shift_left(base_hi, jnp.int32(16)) | (base_lo + 32768)

    def count(pred):
        def body(kb, acc):
            m = pred(keys_ref[kb], kb).astype(F32)
            parts = [m[j * 8:(j + 1) * 8] for j in range(4)]
            for j in range(4, tk // 8):
                parts[j % 4] = parts[j % 4] + m[j * 8:(j + 1) * 8]
            return acc + ((parts[0] + parts[1]) + (parts[2] + parts[3]))
        acc = lax.fori_loop(0, n_act, body, jnp.zeros((8, tq), F32))
        return jnp.sum(acc, axis=0, keepdims=True)

    cnt_gt = count(lambda kk, kb: kk > base)
    cnt_ge = count(lambda kk, kb: kk >= base)
    need = topk - cnt_gt
    tie = (cnt_ge > topk) & (base != INT_MIN)
    cut_ref[...] = jnp.full((1, tq), seq, I32)

    @pl.when(jnp.max(tie.astype(F32)) > 0.0)
    def _():
        nbits = int(math.log2(seq))

        def idx_body(i, m):
            cand = m | lax.shift_left(jnp.int32(1), lax.convert_element_type(nbits - 1 - i, I32))
            below = count(lambda kk, kb: (kk == base) & (kb * tk + key_l < cand))
            return jnp.where(below < need, cand, m)

        m = lax.fori_loop(0, nbits, idx_body, jnp.zeros((1, tq), I32))
        cut_ref[...] = jnp.where(tie, m, seq)

    cut = cut_ref[...]
    for kb in range(nkb):
        @pl.when(kb <= qt)
        def _():
            kk = keys_ref[kb]
            sel = ((kk > base) | ((kk == base) & (kb * tk + key_l <= cut))) & (kk != INT_MIN)
            bias_ref[0, 0, kb] = jnp.where(sel, F32(0.0), F32(MASK_BIAS)).astype(BF16)

        @pl.when(kb > qt)
        def _():
            bias_ref[0, 0, kb] = jnp.full((tk, tq), MASK_BIAS, BF16)


def _idx_mask(qit, ki, wit, B, S, topk, tq):
    nq = S // tq
    kern = functools.partial(_idx_kernel, tq=tq, tk=tq, nkb=nq, topk=topk, seq=S)
    return pl.pallas_call(
        kern,
        grid=(B, nq),
        in_specs=[pl.BlockSpec((1, IDX_HEADS * IDX_DIM, tq), lambda b, q: (b * nq + q, 0, 0)),
                  pl.BlockSpec((1, S, LANES), lambda b, q: (b, 0, 0)),
                  pl.BlockSpec((1, IDX_HEADS, tq), lambda b, q: (b * nq + q, 0, 0))],
        out_specs=pl.BlockSpec((1, 1, nq, tq, tq), lambda b, q: (b, q, 0, 0, 0)),
        out_shape=jax.ShapeDtypeStruct((B, nq, nq, tq, tq), BF16),
        scratch_shapes=[pltpu.VMEM((nq, tq, tq), I32), pltpu.VMEM((nq, tq, tq), I16), pltpu.VMEM((nq, tq, tq), I16),
                        pltpu.VMEM((1, tq), I32)],
        compiler_params=_params("parallel", "arbitrary"),
        name="dsa_index_select",
    )(qit, ki, wit)


def _dsa_attn_kernel(q_ref, k_ref, v_ref, bias_ref, o_ref, m_ref, acc_ref, *, tq, tk, sub):
    qt = pl.program_id(2)
    nsub = tq // sub
    low = lax.broadcasted_iota(I32, (LANES, 1), 0) < HEAD_DIM
    q = q_ref[0]
    zero = jnp.zeros_like(q)
    q_heads = (jnp.where(low, q, zero), jnp.where(low, zero, q))
    m_ref[...] = jnp.full(m_ref.shape, -jnp.inf, F32)
    acc_ref[...] = jnp.zeros(acc_ref.shape, F32)

    def body(kb, carry):
        kblk = k_ref[0, pl.ds(pl.multiple_of(kb * tk, tk), tk), :]
        vt = v_ref[kb]
        one = jnp.ones_like(vt)
        v_heads = (jnp.where(low, vt, one), jnp.where(low, one, vt))
        bias = jnp.concatenate(
            [jnp.concatenate([bias_ref[0, c, kb * nsub + a] for c in range(nsub)], axis=1) for a in range(nsub)],
            axis=0).astype(F32)
        s = [_dot(kblk, q_heads[h]) + bias for h in range(2)]
        m_old = [m_ref[h] for h in range(2)]
        m_new = [jnp.maximum(m_old[h], jnp.max(s[h], axis=0, keepdims=True)) for h in range(2)]
        p = [jnp.exp2(s[h] - m_new[h]).astype(BF16) for h in range(2)]
        pv = [_dot(v_heads[h], p[h]) for h in range(2)]
        for h in range(2):
            acc_ref[h] = jnp.exp2(m_old[h] - m_new[h]) * acc_ref[h] + pv[h]
            m_ref[h] = m_new[h]
        return carry

    lax.fori_loop(0, qt + 1, body, 0)
    a0 = acc_ref[0]
    a1 = acc_ref[1]
    o = jnp.where(low, a0 / a0[HEAD_DIM:HEAD_DIM + 1, :], a1 / a1[0:1, :])
    o_ref[0] = o.T.astype(o_ref.dtype)


def _dsa_attn(qt, k, vt, bias, B, S, tq):
    nq = S // tq
    npair = SEQ_WIDTH // LANES
    nsb, sub = bias.shape[2], bias.shape[3]
    kern = functools.partial(_dsa_attn_kernel, tq=tq, tk=tq, sub=sub)
    return pl.pallas_call(
        kern,
        grid=(B, npair, nq),
        in_specs=[pl.BlockSpec((1, LANES, tq), lambda b, h, i: (b * nq + i, h, 0)),
                  pl.BlockSpec((1, S, LANES), lambda b, h, i: (b, 0, h)),
                  pl.BlockSpec((nq, LANES, tq), lambda b, h, i: (b, h, 0)),
                  pl.BlockSpec((1, tq // sub, nsb, sub, sub), lambda b, h, i: (b, i, 0, 0, 0))],
        out_specs=pl.BlockSpec((1, tq, LANES), lambda b, h, i: (b, i, h)),
        out_shape=jax.ShapeDtypeStruct((B, S, SEQ_WIDTH), BF16),
        scratch_shapes=[pltpu.VMEM((2, 1, tq), F32), pltpu.VMEM((2, LANES, tq), F32)],
        compiler_params=_params("parallel", "parallel", "arbitrary"),
        name="dsa_attention",
    )(qt, k, vt, bias)


def _mem_attn_kernel(qm_ref, mem_ref, wkv_ref, o_ref, k_scr, v_scr):
    @pl.when(pl.program_id(1) == 0)
    def _():
        kv = _dot(mem_ref[0].astype(BF16), wkv_ref[...])
        k_scr[...] = kv[:, :MEM_WIDTH].astype(BF16)
        v_scr[...] = kv[:, MEM_WIDTH:].astype(BF16)

    lane = lax.broadcasted_iota(I32, (1, LANES), 1)
    low = lane < HEAD_DIM
    scale = HEAD_DIM ** -0.5
    for pair in range(MEM_WIDTH // LANES):
        cols = slice(pair * LANES, (pair + 1) * LANES)
        q = qm_ref[0, :, cols]
        zero = jnp.zeros_like(q)
        kp = k_scr[:, cols]
        vp = v_scr[:, cols]
        outs = []
        for qh in (jnp.where(low, q, zero), jnp.where(low, zero, q)):
            s = _dot_nt(qh, kp) * scale
            s = s - jnp.max(s, axis=1, keepdims=True)
            p = jnp.exp(s)
            p = p / jnp.sum(p, axis=1, keepdims=True)
            outs.append(_dot(p.astype(BF16), vp))
        o_ref[0, :, cols] = jnp.where(low, outs[0], outs[1]).astype(o_ref.dtype)


def _mem_attn(qm, mem, wkv, tq):
    B, S, _ = qm.shape
    M = mem.shape[1]
    return pl.pallas_call(
        _mem_attn_kernel,
        grid=(B, S // tq),
        in_specs=[pl.BlockSpec((1, tq, MEM_WIDTH), lambda b, i: (b, i, 0)),
                  pl.BlockSpec((1, M, D_MODEL), lambda b, i: (b, 0, 0)),
                  pl.BlockSpec(wkv.shape, lambda b, i: (0, 0))],
        out_specs=pl.BlockSpec((1, tq, MEM_WIDTH), lambda b, i: (b, i, 0)),
        out_shape=jax.ShapeDtypeStruct((B, S, MEM_WIDTH), BF16),
        scratch_shapes=[pltpu.VMEM((M, MEM_WIDTH), BF16), pltpu.VMEM((M, MEM_WIDTH), BF16)],
        compiler_params=_params("parallel", "arbitrary"),
        name="memory_attention",
    )(qm, mem, wkv)


def _gla_kernel(q_ref, k_ref, v_ref, r_ref, a_ref, wg_ref, bg_ref, ng_ref, o_ref,
                qb_ref, kb_ref, kd_ref, dl_ref, oi_ref, kv_ref, *, nchunk):
    C = GLA_CHUNK
    S = nchunk * C
    z = _dot_f32(a_ref[0], wg_ref[0]) + bg_ref[0]
    b = (jnp.minimum(z, 0.0) - jnp.log1p(jnp.exp(-jnp.abs(z)))) * (1.0 / GLA_TAU)
    pos = lax.broadcasted_iota(I32, (S, 1), 0) % C
    shift = 1
    while shift < C:
        b = b + jnp.where(pos >= shift, pltpu.roll(b, shift, 0), 0.0)
        shift *= 2
    b3 = b.reshape(nchunk, C, GLA_DKP)
    b_last = b3[:, C - 1:C, :]
    q = q_ref[0].astype(F32) * (GLA_DK ** -0.5)
    k = k_ref[0].astype(F32)
    qb_ref[...] = (q * jnp.exp(b)).astype(BF16)
    kb_ref[...] = (k * jnp.exp(-b)).astype(BF16)
    kd_ref[...] = (k.reshape(nchunk, C, GLA_DKP) * jnp.exp(b_last - b3)).reshape(S, GLA_DKP).astype(BF16)
    dl_ref[...] = jnp.exp(b_last)
    causal = lax.broadcasted_iota(I32, (C, C), 0) >= lax.broadcasted_iota(I32, (C, C), 1)

    def intra(c, carry):
        rows = pl.ds(pl.multiple_of(c * C, C), C)
        v = v_ref[0, rows, :]
        attn = jnp.where(causal, _dot_nt(qb_ref[rows, :], kb_ref[rows, :]), 0.0)
        oi_ref[rows, :] = _dot(attn.astype(BF16), v)
        kv_ref[c] = _dot_tn(v, kd_ref[rows, :])
        return carry

    lax.fori_loop(0, nchunk, intra, 0, unroll=4)
    ng = ng_ref[...]

    def inter(c, st):
        rows = pl.ds(pl.multiple_of(c * C, C), C)
        o = oi_ref[rows, :] + _dot_nt(qb_ref[rows, :], st.astype(BF16))
        ms = jnp.sum(o * o, axis=1, keepdims=True) * (1.0 / GLA_DV)
        o = o * lax.rsqrt(ms + RMS_EPS) * ng
        r = r_ref[0, rows, :].astype(F32)
        o_ref[0, rows, :] = (o * (r * jax.nn.sigmoid(r))).astype(o_ref.dtype)
        return st * dl_ref[c] + kv_ref[c]

    lax.fori_loop(0, nchunk, inter, jnp.zeros((GLA_DVP, GLA_DKP), F32), unroll=4)


def _gla(q, k, v, r, a1, wg, bg, ng):
    B, S, _ = q.shape
    nchunk = S // GLA_CHUNK
    kern = functools.partial(_gla_kernel, nchunk=nchunk)
    kspec = pl.BlockSpec((1, S, GLA_DKP), lambda b, h: (b, 0, h))
    vspec = pl.BlockSpec((1, S, GLA_DVP), lambda b, h: (b, 0, h))
    return pl.pallas_call(
        kern,
        grid=(B, GLA_HEADS),
        in_specs=[kspec, kspec, vspec, vspec,
                  pl.BlockSpec((1, S, LANES), lambda b, h: (b, 0, 0)),
                  pl.BlockSpec((1, LANES, GLA_DKP), lambda b, h: (h, 0, 0)),
                  pl.BlockSpec((1, 1, GLA_DKP), lambda b, h: (h, 0, 0)),
                  pl.BlockSpec((1, GLA_DVP), lambda b, h: (0, 0))],
        out_specs=vspec,
        out_shape=jax.ShapeDtypeStruct((B, S, GLA_HEADS * GLA_DVP), BF16),
        scratch_shapes=[pltpu.VMEM((S, GLA_DKP), BF16), pltpu.VMEM((S, GLA_DKP), BF16), pltpu.VMEM((S, GLA_DKP), BF16),
                        pltpu.VMEM((nchunk, 1, GLA_DKP), F32), pltpu.VMEM((S, GLA_DVP), F32),
                        pltpu.VMEM((nchunk, GLA_DVP, GLA_DKP), F32)],
        compiler_params=_params("parallel", "parallel"),
        name="gla",
    )(q, k, v, r, a1, wg, bg, ng)


def _mix_router_kernel(seq_ref, memo_ref, wa_ref, wb_ref, x_ref, g_ref, b_ref, wr_ref, br_ref,
                       x1_ref, comb_ref, *, alpha):
    mixed = _dot(seq_ref[...], wa_ref[...]) + _dot(memo_ref[...], wb_ref[...])
    x1 = _layer_norm(alpha * x_ref[...] + mixed, g_ref[...], b_ref[...])
    x1_ref[...] = x1
    logits = _dot_f32(x1, wr_ref[...]) + br_ref[...]
    lane = lax.broadcasted_iota(I32, logits.shape, 1)
    neg = -jnp.inf
    glog = jnp.where(lane < N_GROUPS, logits, neg)
    gmax = jnp.max(glog, axis=1, keepdims=True)
    gsel = jnp.min(jnp.where(glog == gmax, lane, LANES), axis=1, keepdims=True)
    pg = 1.0 / jnp.sum(jnp.exp(glog - gmax), axis=1, keepdims=True)
    lo = N_GROUPS + EXPERTS_PER_GROUP * gsel
    elog = jnp.where((lane >= lo) & (lane < lo + EXPERTS_PER_GROUP), logits, neg)
    v1 = jnp.max(elog, axis=1, keepdims=True)
    i1 = jnp.min(jnp.where(elog == v1, lane, LANES), axis=1, keepdims=True)
    elog2 = jnp.where(lane == i1, neg, elog)
    v2 = jnp.max(elog2, axis=1, keepdims=True)
    i2 = jnp.min(jnp.where(elog2 == v2, lane, LANES), axis=1, keepdims=True)
    e2 = jnp.exp(v2 - v1)
    den = 1.0 + e2
    comb_ref[...] = (jnp.where(lane == i1 - lo, pg / den, 0.0) + jnp.where(lane == i2 - lo, pg * e2 / den, 0.0)
                     + jnp.where(lane == EXPERTS_PER_GROUP, (gsel.astype(F32)), 0.0))


def _mix_router(seq, memo, wa, wb, x2d, g, b, wr, br, alpha, tm):
    T = x2d.shape[0]
    row = lambda n: pl.BlockSpec((tm, n), lambda i: (i, 0))
    full = lambda a: pl.BlockSpec(a.shape, lambda i: (0,) * a.ndim)
    kern = functools.partial(_mix_router_kernel, alpha=alpha)
    return pl.pallas_call(
        kern,
        grid=(T // tm,),
        in_specs=[row(seq.shape[1]), row(MEM_WIDTH), full(wa), full(wb), row(D_MODEL), full(g), full(b),
                  full(wr), full(br)],
        out_specs=[row(D_MODEL), row(LANES)],
        out_shape=[jax.ShapeDtypeStruct((T, D_MODEL), F32), jax.ShapeDtypeStruct((T, LANES), F32)],
        compiler_params=_params("parallel"),
        name="outproj_ln_router",
    )(seq, memo, wa, wb, x2d, g, b, wr, br)


MOE_ALIGN = 16
MOE_TM = 1024
MOE_TR = 1024
MOE_XW = D_MODEL + LANES


def _ceil_to(v, m):
    return (v + (m - 1)) // m * m


def _moe_sort_kernel(x_ref, route_ref, xs_ref, meta_ref, bounds_ref, loc_ref, zero_ref, off_ref, seg_ref, sem,
                     *, tm, rloc, ntiles, cap):
    p = pl.program_id(0)
    i = pl.program_id(1)
    rt = route_ref[...].T
    gid = rt[EXPERTS_PER_GROUP:EXPERTS_PER_GROUP + 1, :]
    grp = lax.broadcasted_iota(I32, (8, 1), 0).astype(F32)
    onehot = (gid == grp).astype(F32)
    cnt = jnp.sum(onehot, axis=1, keepdims=True)
    npad = jnp.floor((cnt + (MOE_ALIGN - 1)) * (1.0 / MOE_ALIGN)) * MOE_ALIGN
    for k in range(N_GROUPS):
        seg_ref[N_GROUPS + k] = jnp.max(npad[k:k + 1, :]).astype(I32)

    @pl.when((p == 0) & (i == 0))
    def _():
        for k in range(N_GROUPS):
            off_ref[k] = 0
        zero_ref[...] = jnp.zeros(zero_ref.shape, BF16)

    @pl.when(p == 0)
    def _():
        for k in range(N_GROUPS):
            off_ref[k] = off_ref[k] + seg_ref[N_GROUPS + k]

    @pl.when((p == 1) & (i == 0))
    def _():
        base = 0
        for k in range(N_GROUPS):
            total = off_ref[k]
            bounds_ref[k] = base
            off_ref[k] = base
            base = base + _ceil_to(total, MOE_TR)

    def fill_zero(first, last):
        def tail(c):
            dst = xs_ref.at[pl.ds(pl.multiple_of(first + c * MOE_ALIGN, MOE_ALIGN), MOE_ALIGN), :]
            return pltpu.make_async_copy(zero_ref, dst, sem)
        n = (last - first) // MOE_ALIGN
        lax.fori_loop(0, n, lambda c, carry: (tail(c).start(), carry)[1], 0)
        lax.fori_loop(0, n, lambda c, carry: (tail(c).wait(), carry)[1], 0)

    @pl.when(p == 1)
    def _():
        earlier = (lax.broadcasted_iota(I32, (tm, tm), 0) < lax.broadcasted_iota(I32, (tm, tm), 1)).astype(BF16)
        rank = _dot(onehot.astype(BF16), earlier)
        starts = [jnp.zeros((1, 1), F32)]
        for k in range(1, N_GROUPS):
            starts.append(starts[-1] + npad[k - 1:k, :])
        for k in range(N_GROUPS):
            seg_ref[k] = jnp.max(starts[k]).astype(I32)
        start = jnp.concatenate(starts + [jnp.zeros((8 - N_GROUPS, 1), F32)], axis=0)
        dest = jnp.sum(onehot * (start + rank), axis=0, keepdims=True).astype(I32)
        perm = (lax.broadcasted_iota(I32, (rloc, tm), 0) == dest).astype(BF16)
        loc_ref[:, :D_MODEL] = _dot(perm, x_ref[...].astype(BF16)).astype(BF16)
        w = rt[0:EXPERTS_PER_GROUP, :]
        w_hi, w_lo = _split_bf16(w)
        w_lo2 = (w - w_hi.astype(F32) - w_lo.astype(F32)).astype(BF16)
        w_terms = jnp.concatenate([w_hi, w_lo, w_lo2, jnp.zeros((LANES - 3 * EXPERTS_PER_GROUP, tm), BF16)], axis=0)
        loc_ref[:, D_MODEL:] = _dot_nt(perm, w_terms).astype(BF16)

        def chunk(k, c):
            src = loc_ref.at[pl.ds(pl.multiple_of(seg_ref[k] + c * MOE_ALIGN, MOE_ALIGN), MOE_ALIGN), :]
            dst = xs_ref.at[pl.ds(pl.multiple_of(off_ref[k] + c * MOE_ALIGN, MOE_ALIGN), MOE_ALIGN), :]
            return pltpu.make_async_copy(src, dst, sem)

        def for_chunks(fn):
            for k in range(N_GROUPS):
                def body(c, carry, k=k):
                    fn(chunk(k, c))
                    return carry
                lax.fori_loop(0, seg_ref[N_GROUPS + k] // MOE_ALIGN, body, 0)

        for_chunks(lambda cp: cp.start())
        for_chunks(lambda cp: cp.wait())
        for k in range(N_GROUPS):
            meta_ref[i * 2 * N_GROUPS + k] = off_ref[k]
            meta_ref[i * 2 * N_GROUPS + N_GROUPS + k] = seg_ref[N_GROUPS + k]
            off_ref[k] = off_ref[k] + seg_ref[N_GROUPS + k]

    @pl.when((p == 1) & (i == ntiles - 1))
    def _():
        for k in range(N_GROUPS):
            bounds_ref[N_GROUPS + k] = off_ref[k]
            fill_zero(off_ref[k], bounds_ref[k + 1] if k + 1 < N_GROUPS else cap)


def _moe_sort(x1, route, tm):
    T = x1.shape[0]
    ntiles = T // tm
    rloc = _ceil_to(tm + N_GROUPS * MOE_ALIGN, LANES)
    cap = _ceil_to(T + MOE_ALIGN * N_GROUPS * ntiles, MOE_TR) + N_GROUPS * MOE_TR
    kern = functools.partial(_moe_sort_kernel, tm=tm, rloc=rloc, ntiles=ntiles, cap=cap)
    smem = pl.BlockSpec(memory_space=pltpu.SMEM)
    return pl.pallas_call(
        kern,
        grid=(2, ntiles),
        in_specs=[pl.BlockSpec((tm, D_MODEL), lambda p, i: (i * p, 0)),
                  pl.BlockSpec((tm, LANES), lambda p, i: (i, 0))],
        out_specs=[pl.BlockSpec(memory_space=pl.ANY), smem, smem],
        out_shape=[jax.ShapeDtypeStruct((cap, MOE_XW), BF16),
                   jax.ShapeDtypeStruct((ntiles * 2 * N_GROUPS,), I32),
                   jax.ShapeDtypeStruct((2 * N_GROUPS,), I32)],
        scratch_shapes=[pltpu.VMEM((rloc, MOE_XW), BF16), pltpu.VMEM((MOE_ALIGN, MOE_XW), BF16),
                        pltpu.SMEM((N_GROUPS,), I32), pltpu.SMEM((2 * N_GROUPS,), I32), pltpu.SemaphoreType.DMA],
        compiler_params=_params("arbitrary", "arbitrary"),
        name="moe_sort_dispatch",
    )(x1, route)


def _moe_tile_group(r, bounds_ref):
    row = r * MOE_TR
    g = 0
    for k in range(1, N_GROUPS):
        g = g + (row >= bounds_ref[k]).astype(I32)
    return g


def _moe_mlp_kernel(bounds_ref, xs_ref, w13_ref, w2_ref, ys_ref, acc_ref):
    r = pl.program_id(0)
    g = _moe_tile_group(r, bounds_ref)
    end = bounds_ref[N_GROUPS]
    for k in range(1, N_GROUPS):
        end = jnp.where(g == k, bounds_ref[N_GROUPS + k], end)
    used = r * MOE_TR < end

    @pl.when(used)
    def _():
        xb = xs_ref[:, :D_MODEL]
        terms = xs_ref[:, D_MODEL:].astype(F32)
        cw = terms + pltpu.roll(terms, LANES - EXPERTS_PER_GROUP, 1) + pltpu.roll(terms, LANES - 2 * EXPERTS_PER_GROUP, 1)
        for e in range(EXPERTS_PER_GROUP):
            h = _dot(xb, w13_ref[0, :, e * 2 * EXPERT_FF:(e + 1) * 2 * EXPERT_FF])
            a = h[:, :EXPERT_FF]
            u = h[:, EXPERT_FF:]
            act = ((a * jax.nn.sigmoid(a)) * u * cw[:, e:e + 1]).astype(BF16)
            y = _dot(act, w2_ref[0, e * EXPERT_FF:(e + 1) * EXPERT_FF, :])
            if e == 0:
                acc_ref[...] = y
            else:
                acc_ref[...] += y
        ys_ref[...] = acc_ref[...].astype(BF16)

    @pl.when(jnp.logical_not(used))
    def _():
        ys_ref[...] = jnp.zeros(ys_ref.shape, BF16)


def _moe_mlp(bounds, xs, w13, w2):
    cap = xs.shape[0]
    wmap = lambda r, b: (_moe_tile_group(r, b), 0, 0)
    grid_spec = pltpu.PrefetchScalarGridSpec(
        num_scalar_prefetch=1,
        grid=(cap // MOE_TR,),
        in_specs=[pl.BlockSpec((MOE_TR, MOE_XW), lambda r, b: (r, 0)),
                  pl.BlockSpec((1, D_MODEL, EXPERTS_PER_GROUP * 2 * EXPERT_FF), wmap),
                  pl.BlockSpec((1, EXPERTS_PER_GROUP * EXPERT_FF, D_MODEL), wmap)],
        out_specs=pl.BlockSpec((MOE_TR, D_MODEL), lambda r, b: (r, 0)),
        scratch_shapes=[pltpu.VMEM((MOE_TR, D_MODEL), F32)])
    return pl.pallas_call(
        _moe_mlp_kernel,
        grid_spec=grid_spec,
        out_shape=jax.ShapeDtypeStruct((cap, D_MODEL), BF16),
        compiler_params=_params("arbitrary"),
        name="moe_group_experts",
    )(bounds, xs, w13, w2)


def _moe_combine_kernel(meta_ref, x_ref, route_ref, ys_ref, g_ref, b_ref, o_ref, loc_ref, sem, *, alpha, tm, rloc):
    i = pl.program_id(0)
    offs = [meta_ref[i * 2 * N_GROUPS + k] for k in range(N_GROUPS)]
    rows = [meta_ref[i * 2 * N_GROUPS + N_GROUPS + k] for k in range(N_GROUPS)]
    segs = [0]
    for k in range(N_GROUPS):
        segs.append(segs[-1] + rows[k])

    def chunk(k, c):
        src = ys_ref.at[pl.ds(pl.multiple_of(offs[k] + c * MOE_ALIGN, MOE_ALIGN), MOE_ALIGN), :]
        dst = loc_ref.at[pl.ds(pl.multiple_of(segs[k] + c * MOE_ALIGN, MOE_ALIGN), MOE_ALIGN), :]
        return pltpu.make_async_copy(src, dst, sem)

    def for_chunks(fn):
        for k in range(N_GROUPS):
            def body(c, carry, k=k):
                fn(chunk(k, c))
                return carry
            lax.fori_loop(0, rows[k] // MOE_ALIGN, body, 0)

    for_chunks(lambda cp: cp.start())
    route = route_ref[...]
    lane = lax.broadcasted_iota(I32, (1, LANES), 1)
    onehot = ((route[:, EXPERTS_PER_GROUP:EXPERTS_PER_GROUP + 1] == lane.astype(F32)) & (lane < N_GROUPS)).astype(F32)
    before = (lax.broadcasted_iota(I32, (tm, tm), 1) < lax.broadcasted_iota(I32, (tm, tm), 0)).astype(BF16)
    rank = _dot(before, onehot.astype(BF16))
    seg_start = jnp.zeros((1, LANES), F32)
    for k in range(N_GROUPS):
        seg_start = seg_start + jnp.where(lane == k, lax.convert_element_type(segs[k], F32), F32(0.0))
    dest = jnp.sum(onehot * (seg_start + rank), axis=1, keepdims=True).astype(I32)
    unperm = (lax.broadcasted_iota(I32, (1, rloc), 1) == dest).astype(BF16)
    for_chunks(lambda cp: cp.wait())
    valid = lax.broadcasted_iota(I32, (rloc, 1), 0) < segs[N_GROUPS]
    ysl = jnp.where(valid, loc_ref[...], jnp.zeros((rloc, D_MODEL), BF16))
    o_ref[...] = _layer_norm(alpha * x_ref[...] + _dot(unperm, ysl), g_ref[...], b_ref[...])


def _moe_combine(meta, x1, route, ys, g, b, alpha, tm):
    T = x1.shape[0]
    rloc = _ceil_to(tm + N_GROUPS * MOE_ALIGN, LANES)
    kern = functools.partial(_moe_combine_kernel, alpha=alpha, tm=tm, rloc=rloc)
    grid_spec = pltpu.PrefetchScalarGridSpec(
        num_scalar_prefetch=1,
        grid=(T // tm,),
        in_specs=[pl.BlockSpec((tm, D_MODEL), lambda i, m: (i, 0)),
                  pl.BlockSpec((tm, LANES), lambda i, m: (i, 0)),
                  pl.BlockSpec(memory_space=pl.ANY),
                  pl.BlockSpec((1, D_MODEL), lambda i, m: (0, 0)),
                  pl.BlockSpec((1, D_MODEL), lambda i, m: (0, 0))],
        out_specs=pl.BlockSpec((tm, D_MODEL), lambda i, m: (i, 0)),
        scratch_shapes=[pltpu.VMEM((rloc, D_MODEL), BF16), pltpu.SemaphoreType.DMA])
    return pl.pallas_call(
        kern,
        grid_spec=grid_spec,
        out_shape=jax.ShapeDtypeStruct((T, D_MODEL), F32),
        compiler_params=_params("arbitrary"),
        name="moe_combine_ln",
    )(meta, x1, route, ys, g, b)


def _pad_cols(a, width):
    return jnp.pad(a, ((0, 0), (0, width - a.shape[1])))


def _pad_heads(w, heads, dim, dim_pad):
    rows = w.shape[0]
    return jnp.pad(w.reshape(rows, heads, dim), ((0, 0), (0, 0), (0, dim_pad - dim))).reshape(rows, heads * dim_pad)


def _tile(n, pref):
    t = pref
    while n % t:
        t //= 2
    return t


def kernel(x, mem, positions, dsa_w_in, dsa_idx_k_g, dsa_idx_k_b, gla_w_in, gla_w_gate, gla_b_gate, gla_norm_g,
           w_mem_kv, w_out, ln1_g, ln1_b, ln2_g, ln2_b, moe_w_group, moe_b_group, moe_w_router, moe_b_router,
           moe_w13, moe_w2):
    B, S, D = x.shape
    T = B * S
    depth = w_out.shape[0]
    alpha = (2 * depth) ** 0.25
    tm = _tile(T, 512)
    tq = _tile(S, 512)
    tq_idx = _tile(S, 256)
    topk = min(DSA_MAX_TOPK, S // 4)

    inv = ROPE_THETA ** (-jnp.arange(0, HEAD_DIM, 2, dtype=F32) / HEAD_DIM)
    ang = positions.astype(F32).reshape(T, 1) * inv
    cos, sin = jnp.cos(ang), jnp.sin(ang)
    cosf = jnp.concatenate([cos, cos, cos, cos], axis=1)
    sinf = jnp.concatenate([-sin, sin, -sin, sin], axis=1)

    xc = x.reshape(T, D)
    ia = ib = 0
    for i in range(depth):
        if i % 2 == 0:
            w = dsa_w_in[ia]
            wq, wk, wv, wqi, wki, wwi, wqm = jnp.split(w, [768, 1536, 2304, 2816, 2880, 2888], axis=1)
            w_tok = jnp.concatenate([wk, wqm, _pad_cols(wki, LANES)], axis=1).astype(BF16)
            w_feat = jnp.concatenate([wq, wv, wqi, _pad_cols(wwi, 16)], axis=1).T.astype(BF16)
            lng = _pad_cols(dsa_idx_k_g[ia][None, :], LANES)
            lnb = _pad_cols(dsa_idx_k_b[ia][None, :], LANES)
            k, qm, ki, q_t, v_t, qi_t, wi_t = _proj_dsa(xc, w_tok, w_feat, cosf, sinf, cosf.T, sinf.T, lng, lnb,
                                                         tm, tq, tq_idx)
            bias = _idx_mask(qi_t, ki.reshape(B, S, LANES), wi_t, B, S, topk, tq_idx)
            seq = _dsa_attn(q_t, k.reshape(B, S, SEQ_WIDTH), v_t, bias, B, S, tq).reshape(T, SEQ_WIDTH)
            wa = w_out[i][:SEQ_WIDTH].astype(BF16)
            ia += 1
        else:
            w = gla_w_in[ib]
            wq, wk, wv, wr_, wa1, wqm = jnp.split(w, [384, 768, 1536, 2304, 2320], axis=1)
            w_all = jnp.concatenate([
                _pad_heads(wq, GLA_HEADS, GLA_DK, GLA_DKP), _pad_heads(wk, GLA_HEADS, GLA_DK, GLA_DKP),
                _pad_heads(wv, GLA_HEADS, GLA_DV, GLA_DVP), _pad_heads(wr_, GLA_HEADS, GLA_DV, GLA_DVP),
                wqm, _pad_cols(wa1, LANES)], axis=1).astype(BF16)
            q, k, v, r, qm, a1 = _proj_gla(xc, w_all, tm)
            r3 = lambda a: a.reshape(B, S, a.shape[1])
            wg = _pad_heads(gla_w_gate[ib], GLA_HEADS, GLA_DK, GLA_DKP)
            wg = jnp.pad(wg, ((0, LANES - GLA_GATE_RANK), (0, 0)))
            wg = wg.reshape(LANES, GLA_HEADS, GLA_DKP).transpose(1, 0, 2)
            bg = _pad_heads(gla_b_gate[ib][None, :], GLA_HEADS, GLA_DK, GLA_DKP).reshape(GLA_HEADS, 1, GLA_DKP)
            ng = _pad_cols(gla_norm_g[ib][None, :], GLA_DVP)
            seq = _gla(r3(q), r3(k), r3(v), r3(r), r3(a1), wg, bg, ng).reshape(T, GLA_HEADS * GLA_DVP)
            wa = w_out[i][:SEQ_WIDTH].reshape(GLA_HEADS, GLA_DV, D)
            wa = jnp.pad(wa, ((0, 0), (0, GLA_DVP - GLA_DV), (0, 0))).reshape(GLA_HEADS * GLA_DVP, D).astype(BF16)
            ib += 1
        memo = _mem_attn(qm.reshape(B, S, MEM_WIDTH), mem, w_mem_kv[i].astype(BF16), tq).reshape(T, MEM_WIDTH)
        wb = w_out[i][SEQ_WIDTH:].astype(BF16)
        wr = jnp.concatenate([moe_w_group[i], moe_w_router[i].transpose(1, 0, 2).reshape(D, N_EXPERTS)], axis=1)
        wr = _pad_cols(wr, LANES)
        br = _pad_cols(jnp.concatenate([moe_b_group[i], moe_b_router[i].reshape(-1)])[None, :], LANES)
        x1, route = _mix_router(seq, memo, wa, wb, xc, ln1_g[i][None, :], ln1_b[i][None, :], wr, br, alpha, tm)
        w13g = moe_w13[i].transpose(0, 2, 1, 3).reshape(N_GROUPS, D, EXPERTS_PER_GROUP * 2 * EXPERT_FF).astype(BF16)
        w2g = moe_w2[i].reshape(N_GROUPS, EXPERTS_PER_GROUP * EXPERT_FF, D).astype(BF16)
        tmoe = _tile(T, MOE_TM)
        xs, meta, bounds = _moe_sort(x1, route, tmoe)
        ys = _moe_mlp(bounds, xs, w13g, w2g)
        xc = _moe_combine(meta, x1, route, ys, ln2_g[i][None, :], ln2_b[i][None, :], alpha, tmoe)
    return xc.reshape(B, S, D)
```

```python
import functools
import math

import jax
import jax.numpy as jnp
from jax import lax
from jax.experimental import pallas as pl
from jax.experimental.pallas import tpu as pltpu

F32 = jnp.float32
BF16 = jnp.bfloat16
I32 = jnp.int32
I16 = jnp.int16

LANES = 128
D_MODEL = 1024
HEAD_DIM = 64
N_MEM_HEADS = 4
MEM_WIDTH = N_MEM_HEADS * HEAD_DIM
SEQ_WIDTH = D_MODEL - MEM_WIDTH
ROPE_THETA = 10000.0
DSA_HEADS = SEQ_WIDTH // HEAD_DIM
IDX_HEADS = 8
IDX_DIM = 64
DSA_MAX_TOPK = 256
GLA_HEADS = 4
GLA_DV = SEQ_WIDTH // GLA_HEADS
GLA_DK = GLA_DV // 2
GLA_DKP = 128
GLA_DVP = 256
GLA_GATE_RANK = 16
GLA_TAU = 16.0
GLA_CHUNK = 64
N_GROUPS = 4
EXPERTS_PER_GROUP = 8
N_EXPERTS = N_GROUPS * EXPERTS_PER_GROUP
EXPERT_FF = 256
LN_EPS = 1e-5
RMS_EPS = 1e-6
MASK_BIAS = -1e30
INT_MIN = -2 ** 31
VMEM_LIMIT = 56 * 1024 * 1024


def _dot(a, b):
    return jnp.dot(a, b, preferred_element_type=F32)


def _dot_nt(a, b):
    return lax.dot_general(a, b, (((1,), (1,)), ((), ())), preferred_element_type=F32)


def _dot_tn(a, b):
    return lax.dot_general(a, b, (((0,), (0,)), ((), ())), preferred_element_type=F32)


def _split_bf16(a):
    hi = a.astype(BF16)
    lo = (a - hi.astype(F32)).astype(BF16)
    return hi, lo


def _dot_f32(a, b):
    ah, al = _split_bf16(a)
    bh, bl = _split_bf16(b)
    return _dot(ah, bh) + (_dot(ah, bl) + _dot(al, bh))


def _layer_norm(y, g, b):
    mu = jnp.mean(y, axis=-1, keepdims=True)
    yc = y - mu
    var = jnp.mean(yc * yc, axis=-1, keepdims=True)
    return yc * lax.rsqrt(var + LN_EPS) * g + b


def _params(*sem):
    return pltpu.CompilerParams(dimension_semantics=sem, vmem_limit_bytes=VMEM_LIMIT)


def _rope_fn(cos, sin, axis):
    shape = (1, LANES) if axis == 1 else (LANES, 1)
    pos = lax.broadcasted_iota(I32, shape, axis)
    first_half = (pos % HEAD_DIM) < (HEAD_DIM // 2)

    def rope(a):
        swapped = jnp.where(first_half, pltpu.roll(a, LANES - HEAD_DIM // 2, axis),
                            pltpu.roll(a, HEAD_DIM // 2, axis))
        return a * cos + swapped * sin
    return rope


def _proj_dsa_kernel(x_ref, wa_ref, wb_ref, cos_ref, sin_ref, cost_ref, sint_ref, lng_ref, lnb_ref,
                     k_ref, qm_ref, ki_ref, qt_ref, vt_ref, qit_ref, wit_ref):
    xb = x_ref[...].astype(BF16)
    rope = _rope_fn(cos_ref[...], sin_ref[...], 1)
    rope_t = _rope_fn(cost_ref[...], sint_ref[...], 0)
    tok = _dot(xb, wa_ref[...])
    for c in range(SEQ_WIDTH // LANES):
        k_ref[:, c * LANES:(c + 1) * LANES] = rope(tok[:, c * LANES:(c + 1) * LANES]).astype(BF16)
    qm_ref[...] = tok[:, SEQ_WIDTH:SEQ_WIDTH + MEM_WIDTH].astype(BF16)
    acc = tok[:, D_MODEL:D_MODEL + LANES]
    lane = lax.broadcasted_iota(I32, (1, LANES), 1)
    is_ki = lane < IDX_DIM
    mu = jnp.sum(jnp.where(is_ki, acc, 0.0), axis=1, keepdims=True) * (1.0 / IDX_DIM)
    d = jnp.where(is_ki, acc - mu, 0.0)
    var = jnp.sum(d * d, axis=1, keepdims=True) * (1.0 / IDX_DIM)
    ki_ref[...] = rope(d * lax.rsqrt(var + LN_EPS) * lng_ref[...] + lnb_ref[...]).astype(BF16)

    def store_t(out_ref, r, val):
        nblk, _, tb = out_ref.shape
        rows = val.shape[0]
        for j in range(nblk):
            out_ref[j, r * rows:(r + 1) * rows, :] = val[:, j * tb:(j + 1) * tb].astype(out_ref.dtype)

    q_scale = HEAD_DIM ** -0.5 * math.log2(math.e)
    feat = _dot_nt(wb_ref[...], xb)
    for r in range(SEQ_WIDTH // LANES):
        store_t(qt_ref, r, rope_t(feat[r * LANES:(r + 1) * LANES]) * q_scale)
    off = SEQ_WIDTH
    for r in range(SEQ_WIDTH // LANES):
        store_t(vt_ref, r, feat[off + r * LANES:off + (r + 1) * LANES])
    off = 2 * SEQ_WIDTH
    for r in range(IDX_HEADS * IDX_DIM // LANES):
        store_t(qit_ref, r, rope_t(feat[off + r * LANES:off + (r + 1) * LANES]))
    off = 2 * SEQ_WIDTH + IDX_HEADS * IDX_DIM
    store_t(wit_ref, 0, feat[off:off + IDX_HEADS] * (IDX_HEADS ** -0.5 * IDX_DIM ** -0.5))


def _proj_dsa(x2d, wa, wb, cosf, sinf, cost, sint, lng, lnb, tm, tq_att, tq_idx):
    T = x2d.shape[0]
    row = lambda n: pl.BlockSpec((tm, n), lambda i: (i, 0))
    col = pl.BlockSpec((LANES, tm), lambda i: (0, i))
    full = lambda a: pl.BlockSpec(a.shape, lambda i: (0,) * a.ndim)
    featmaj = lambda n, tb: pl.BlockSpec((tm // tb, n, tb), lambda i: (i, 0, 0))
    fshape = lambda n, tb, dt: jax.ShapeDtypeStruct((T // tb, n, tb), dt)
    return pl.pallas_call(
        _proj_dsa_kernel,
        grid=(T // tm,),
        in_specs=[row(D_MODEL), full(wa), full(wb), row(LANES), row(LANES), col, col, full(lng), full(lnb)],
        out_specs=[row(SEQ_WIDTH), row(MEM_WIDTH), row(LANES),
                   featmaj(SEQ_WIDTH, tq_att), featmaj(SEQ_WIDTH, tq_att),
                   featmaj(IDX_HEADS * IDX_DIM, tq_idx), featmaj(IDX_HEADS, tq_idx)],
        out_shape=[jax.ShapeDtypeStruct((T, SEQ_WIDTH), BF16), jax.ShapeDtypeStruct((T, MEM_WIDTH), BF16),
                   jax.ShapeDtypeStruct((T, LANES), BF16),
                   fshape(SEQ_WIDTH, tq_att, BF16), fshape(SEQ_WIDTH, tq_att, BF16),
                   fshape(IDX_HEADS * IDX_DIM, tq_idx, BF16), fshape(IDX_HEADS, tq_idx, F32)],
        compiler_params=_params("parallel"),
        name="proj_dsa",
    )(x2d, wa, wb, cosf, sinf, cost, sint, lng, lnb)


def _proj_gla_kernel(x_ref, w_ref, q_ref, k_ref, v_ref, r_ref, qm_ref, a_ref):
    xb = x_ref[...].astype(BF16)
    segments = ((0, 512, q_ref), (512, 512, k_ref), (1024, 1024, v_ref), (2048, 1024, r_ref),
                (3072, 256, qm_ref), (3328, 128, a_ref))
    for off, width, out_ref in segments:
        out_ref[...] = _dot(xb, w_ref[:, off:off + width]).astype(out_ref.dtype)


def _proj_gla(x2d, w, tm):
    T = x2d.shape[0]
    row = lambda n: pl.BlockSpec((tm, n), lambda i: (i, 0))
    outs = [(512, BF16), (512, BF16), (1024, BF16), (1024, BF16), (256, BF16), (LANES, F32)]
    return pl.pallas_call(
        _proj_gla_kernel,
        grid=(T // tm,),
        in_specs=[row(D_MODEL), pl.BlockSpec(w.shape, lambda i: (0, 0))],
        out_specs=[row(n) for n, _ in outs],
        out_shape=[jax.ShapeDtypeStruct((T, n), dt) for n, dt in outs],
        compiler_params=_params("parallel"),
        name="proj_gla",
    )(x2d, w)


def _idx_kernel(qi_ref, ki_ref, wi_ref, bias_ref, keys_ref, hi_ref, lo_ref, cut_ref, *, tq, tk, nkb, topk, seq):
    qt = pl.program_id(1)
    n_act = qt + 1
    key_l = lax.broadcasted_iota(I32, (tk, tq), 0)
    qry_g = qt * tq + lax.broadcasted_iota(I32, (tk, tq), 1)
    wi = wi_ref[0]

    def score_body(kb, carry):
        kblk = ki_ref[0, pl.ds(pl.multiple_of(kb * tk, tk), tk), :][:, :IDX_DIM]
        sc = jnp.zeros((tk, tq), F32)
        for h in range(IDX_HEADS):
            qh = qi_ref[0, h * IDX_DIM:(h + 1) * IDX_DIM, :]
            sc = sc + jnp.maximum(_dot(kblk, qh), 0.0) * wi[h:h + 1, :]
        bits = lax.bitcast_convert_type(sc, I32)
        key = bits ^ ((bits >> 31) & 0x7FFFFFFF)
        key = jnp.where(kb * tk + key_l > qry_g, INT_MIN, key)
        keys_ref[kb] = key
        hi_ref[kb] = (key >> 16).astype(I16)
        lo_ref[kb] = (key ^ 0x8000).astype(I16)
        return carry

    lax.fori_loop(0, n_act, score_body, 0)

    def count16(ref, cand, strict=False):
        cand16 = cand.astype(I16)

        def body(kb, acc):
            v = ref[kb]
            m = jnp.where((v > cand16) if strict else (v >= cand16), jnp.int16(1), jnp.int16(0))
            parts = [m[j * 16:(j + 1) * 16] for j in range(4)]
            for j in range(4, tk // 16):
                parts[j % 4] = parts[j % 4] + m[j * 16:(j + 1) * 16]
            return acc + ((parts[0] + parts[1]) + (parts[2] + parts[3]))
        acc = lax.fori_loop(0, n_act, body, jnp.zeros((16, tq), I16))
        return jnp.sum(acc.astype(I32).astype(F32), axis=0, keepdims=True)

    def search16(ref, need):
        def bit_body(i, base_u):
            cand_u = base_u | lax.shift_left(jnp.int32(1), lax.convert_element_type(15 - i, I32))
            cnt = count16(ref, cand_u - 32768)
            return jnp.where(cnt >= need, cand_u, base_u)
        return lax.fori_loop(0, 16, bit_body, jnp.zeros((1, tq), I32)) - 32768

    base_hi = search16(hi_ref, topk)
    above = count16(hi_ref, base_hi, strict=True)
    base_hi16 = base_hi.astype(I16)

    def bucket_body(kb, carry):
        lo_ref[kb] = jnp.where(hi_ref[kb] == base_hi16, lo_ref[kb], jnp.int16(-32768))
        return carry

    lax.fori_loop(0, n_act, bucket_body, 0)
    base_lo = search16(lo_ref, topk - above)
    base = lax.shift_left(base_hi, jnp.int32(16)) | (base_lo + 32768)

    def count(pred):
        def body(kb, acc):
            m = pred(keys_ref[kb], kb).astype(F32)
            parts = [m[j * 8:(j + 1) * 8] for j in range(4)]
            for j in range(4, tk // 8):
                parts[j % 4] = parts[j % 4] + m[j * 8:(j + 1) * 8]
            return acc + ((parts[0] + parts[1]) + (parts[2] + parts[3]))
        acc = lax.fori_loop(0, n_act, body, jnp.zeros((8, tq), F32))
        return jnp.sum(acc, axis=0, keepdims=True)

    cnt_gt = count(lambda kk, kb: kk > base)
    cnt_ge = count(lambda kk, kb: kk >= base)
    need = topk - cnt_gt
    tie = (cnt_ge > topk) & (base != INT_MIN)
    cut_ref[...] = jnp.full((1, tq), seq, I32)

    @pl.when(jnp.max(tie.astype(F32)) > 0.0)
    def _():
        nbits = int(math.log2(seq))

        def idx_body(i, m):
            cand = m | lax.shift_left(jnp.int32(1), lax.convert_element_type(nbits - 1 - i, I32))
            below = count(lambda kk, kb: (kk == base) & (kb * tk + key_l < cand))
            return jnp.where(below < need, cand, m)

        m = lax.fori_loop(0, nbits, idx_body, jnp.zeros((1, tq), I32))
        cut_ref[...] = jnp.where(tie, m, seq)

    cut = cut_ref[...]
    for kb in range(nkb):
        @pl.when(kb <= qt)
        def _():
            kk = keys_ref[kb]
            sel = ((kk > base) | ((kk == base) & (kb * tk + key_l <= cut))) & (kk != INT_MIN)
            bias_ref[0, 0, kb] = jnp.where(sel, F32(0.0), F32(MASK_BIAS)).astype(BF16)

        @pl.when(kb > qt)
        def _():
            bias_ref[0, 0, kb] = jnp.full((tk, tq), MASK_BIAS, BF16)


def _idx_mask(qit, ki, wit, B, S, topk, tq):
    nq = S // tq
    kern = functools.partial(_idx_kernel, tq=tq, tk=tq, nkb=nq, topk=topk, seq=S)
    return pl.pallas_call(
        kern,
        grid=(B, nq),
        in_specs=[pl.BlockSpec((1, IDX_HEADS * IDX_DIM, tq), lambda b, q: (b * nq + q, 0, 0)),
                  pl.BlockSpec((1, S, LANES), lambda b, q: (b, 0, 0)),
                  pl.BlockSpec((1, IDX_HEADS, tq), lambda b, q: (b * nq + q, 0, 0))],
        out_specs=pl.BlockSpec((1, 1, nq, tq, tq), lambda b, q: (b, q, 0, 0, 0)),
        out_shape=jax.ShapeDtypeStruct((B, nq, nq, tq, tq), BF16),
        scratch_shapes=[pltpu.VMEM((nq, tq, tq), I32), pltpu.VMEM((nq, tq, tq), I16), pltpu.VMEM((nq, tq, tq), I16),
                        pltpu.VMEM((1, tq), I32)],
        compiler_params=_params("parallel", "arbitrary"),
        name="dsa_index_select",
    )(qit, ki, wit)


def _dsa_attn_kernel(q_ref, k_ref, v_ref, bias_ref, o_ref, m_ref, acc_ref, *, tq, tk, sub):
    qt = pl.program_id(2)
    nsub = tq // sub
    npair = q_ref.shape[1] // LANES
    nh = 2 * npair
    low = lax.broadcasted_iota(I32, (LANES, 1), 0) < HEAD_DIM
    q_heads = []
    for pr in range(npair):
        q = q_ref[0, pr * LANES:(pr + 1) * LANES, :]
        zero = jnp.zeros_like(q)
        q_heads += [jnp.where(low, q, zero), jnp.where(low, zero, q)]
    m_ref[...] = jnp.full(m_ref.shape, -jnp.inf, F32)
    acc_ref[...] = jnp.zeros(acc_ref.shape, F32)

    def body(kb, carry):
        kblk = k_ref[0, pl.ds(pl.multiple_of(kb * tk, tk), tk), :]
        v_heads = []
        for pr in range(npair):
            vt = v_ref[kb, pr * LANES:(pr + 1) * LANES, :]
            one = jnp.ones_like(vt)
            v_heads += [jnp.where(low, vt, one), jnp.where(low, one, vt)]
        bias = jnp.concatenate(
            [jnp.concatenate([bias_ref[0, c, kb * nsub + a] for c in range(nsub)], axis=1) for a in range(nsub)],
            axis=0).astype(F32)
        s = [_dot(kblk[:, (h // 2) * LANES:(h // 2 + 1) * LANES], q_heads[h]) + bias for h in range(nh)]
        m_old = [m_ref[h] for h in range(nh)]
        m_new = [jnp.maximum(m_old[h], jnp.max(s[h], axis=0, keepdims=True)) for h in range(nh)]
        p = [jnp.exp2(s[h] - m_new[h]).astype(BF16) for h in range(nh)]
        pv = [_dot(v_heads[h], p[h]) for h in range(nh)]
        for h in range(nh):
            acc_ref[h] = jnp.exp2(m_old[h] - m_new[h]) * acc_ref[h] + pv[h]
            m_ref[h] = m_new[h]
        return carry

    lax.fori_loop(0, qt + 1, body, 0)
    for pr in range(npair):
        a0 = acc_ref[2 * pr]
        a1 = acc_ref[2 * pr + 1]
        o = jnp.where(low, a0 / a0[HEAD_DIM:HEAD_DIM + 1, :], a1 / a1[0:1, :])
        o_ref[0, :, pr * LANES:(pr + 1) * LANES] = o.T.astype(o_ref.dtype)


def _dsa_attn(qt, k, vt, bias, B, S, tq, npair):
    nq = S // tq
    width = npair * LANES
    nsb, sub = bias.shape[2], bias.shape[3]
    kern = functools.partial(_dsa_attn_kernel, tq=tq, tk=tq, sub=sub)
    return pl.pallas_call(
        kern,
        grid=(B, SEQ_WIDTH // width, nq),
        in_specs=[pl.BlockSpec((1, width, tq), lambda b, h, i: (b * nq + i, h, 0)),
                  pl.BlockSpec((1, S, width), lambda b, h, i: (b, 0, h)),
                  pl.BlockSpec((nq, width, tq), lambda b, h, i: (b, h, 0)),
                  pl.BlockSpec((1, tq // sub, nsb, sub, sub), lambda b, h, i: (b, i, 0, 0, 0))],
        out_specs=pl.BlockSpec((1, tq, width), lambda b, h, i: (b, i, h)),
        out_shape=jax.ShapeDtypeStruct((B, S, SEQ_WIDTH), BF16),
        scratch_shapes=[pltpu.VMEM((2 * npair, 1, tq), F32), pltpu.VMEM((2 * npair, LANES, tq), F32)],
        compiler_params=_params("parallel", "parallel", "arbitrary"),
        name="dsa_attention",
    )(qt, k, vt, bias)


def _mem_attn_kernel(qm_ref, mem_ref, wkv_ref, o_ref, k_scr, v_scr):
    @pl.when(pl.program_id(1) == 0)
    def _():
        kv = _dot(mem_ref[0].astype(BF16), wkv_ref[...])
        k_scr[...] = kv[:, :MEM_WIDTH].astype(BF16)
        v_scr[...] = kv[:, MEM_WIDTH:].astype(BF16)

    lane = lax.broadcasted_iota(I32, (1, LANES), 1)
    low = lane < HEAD_DIM
    scale = HEAD_DIM ** -0.5
    for pair in range(MEM_WIDTH // LANES):
        cols = slice(pair * LANES, (pair + 1) * LANES)
        q = qm_ref[0, :, cols]
        zero = jnp.zeros_like(q)
        kp = k_scr[:, cols]
        vp = v_scr[:, cols]
        outs = []
        for qh in (jnp.where(low, q, zero), jnp.where(low, zero, q)):
            s = _dot_nt(qh, kp) * scale
            s = s - jnp.max(s, axis=1, keepdims=True)
            p = jnp.exp(s)
            p = p / jnp.sum(p, axis=1, keepdims=True)
            outs.append(_dot(p.astype(BF16), vp))
        o_ref[0, :, cols] = jnp.where(low, outs[0], outs[1]).astype(o_ref.dtype)


def _mem_attn(qm, mem, wkv, tq):
    B, S, _ = qm.shape
    M = mem.shape[1]
    return pl.pallas_call(
        _mem_attn_kernel,
        grid=(B, S // tq),
        in_specs=[pl.BlockSpec((1, tq, MEM_WIDTH), lambda b, i: (b, i, 0)),
                  pl.BlockSpec((1, M, D_MODEL), lambda b, i: (b, 0, 0)),
                  pl.BlockSpec(wkv.shape, lambda b, i: (0, 0))],
        out_specs=pl.BlockSpec((1, tq, MEM_WIDTH), lambda b, i: (b, i, 0)),
        out_shape=jax.ShapeDtypeStruct((B, S, MEM_WIDTH), BF16),
        scratch_shapes=[pltpu.VMEM((M, MEM_WIDTH), BF16), pltpu.VMEM((M, MEM_WIDTH), BF16)],
        compiler_params=_params("parallel", "arbitrary"),
        name="memory_attention",
    )(qm, mem, wkv)


def _gla_kernel(q_ref, k_ref, v_ref, r_ref, a_ref, wg_ref, bg_ref, ng_ref, o_ref,
                qb_ref, kb_ref, kd_ref, dl_ref, oi_ref, kv_ref, *, nchunk):
    C = GLA_CHUNK
    S = nchunk * C
    z = _dot_f32(a_ref[0], wg_ref[0]) + bg_ref[0]
    b = (jnp.minimum(z, 0.0) - jnp.log1p(jnp.exp(-jnp.abs(z)))) * (1.0 / GLA_TAU)
    pos = lax.broadcasted_iota(I32, (S, 1), 0) % C
    shift = 1
    while shift < C:
        b = b + jnp.where(pos >= shift, pltpu.roll(b, shift, 0), 0.0)
        shift *= 2
    b3 = b.reshape(nchunk, C, GLA_DKP)
    b_last = b3[:, C - 1:C, :]
    q = q_ref[0].astype(F32) * (GLA_DK ** -0.5)
    k = k_ref[0].astype(F32)
    qb_ref[...] = (q * jnp.exp(b)).astype(BF16)
    kb_ref[...] = (k * jnp.exp(-b)).astype(BF16)
    kd_ref[...] = (k.reshape(nchunk, C, GLA_DKP) * jnp.exp(b_last - b3)).reshape(S, GLA_DKP).astype(BF16)
    dl_ref[...] = jnp.exp(b_last)
    causal = lax.broadcasted_iota(I32, (C, C), 0) >= lax.broadcasted_iota(I32, (C, C), 1)

    def intra(c, carry):
        rows = pl.ds(pl.multiple_of(c * C, C), C)
        v = v_ref[0, rows, :]
        attn = jnp.where(causal, _dot_nt(qb_ref[rows, :], kb_ref[rows, :]), 0.0)
        oi_ref[rows, :] = _dot(attn.astype(BF16), v)
        kv_ref[c] = _dot_tn(v, kd_ref[rows, :])
        return carry

    lax.fori_loop(0, nchunk, intra, 0, unroll=8)
    ng = ng_ref[...]

    def inter(c, st):
        rows = pl.ds(pl.multiple_of(c * C, C), C)
        o = oi_ref[rows, :] + _dot_nt(qb_ref[rows, :], st.astype(BF16))
        ms = jnp.sum(o * o, axis=1, keepdims=True) * (1.0 / GLA_DV)
        o = o * lax.rsqrt(ms + RMS_EPS) * ng
        r = r_ref[0, rows, :].astype(F32)
        o_ref[0, rows, :] = (o * (r * jax.nn.sigmoid(r))).astype(o_ref.dtype)
        return st * dl_ref[c] + kv_ref[c]

    lax.fori_loop(0, nchunk, inter, jnp.zeros((GLA_DVP, GLA_DKP), F32), unroll=4)


def _gla(q, k, v, r, a1, wg, bg, ng):
    B, S, _ = q.shape
    nchunk = S // GLA_CHUNK
    kern = functools.partial(_gla_kernel, nchunk=nchunk)
    kspec = pl.BlockSpec((1, S, GLA_DKP), lambda b, h: (b, 0, h))
    vspec = pl.BlockSpec((1, S, GLA_DVP), lambda b, h: (b, 0, h))
    return pl.pallas_call(
        kern,
        grid=(B, GLA_HEADS),
        in_specs=[kspec, kspec, vspec, vspec,
                  pl.BlockSpec((1, S, LANES), lambda b, h: (b, 0, 0)),
                  pl.BlockSpec((1, LANES, GLA_DKP), lambda b, h: (h, 0, 0)),
                  pl.BlockSpec((1, 1, GLA_DKP), lambda b, h: (h, 0, 0)),
                  pl.BlockSpec((1, GLA_DVP), lambda b, h: (0, 0))],
        out_specs=vspec,
        out_shape=jax.ShapeDtypeStruct((B, S, GLA_HEADS * GLA_DVP), BF16),
        scratch_shapes=[pltpu.VMEM((S, GLA_DKP), BF16), pltpu.VMEM((S, GLA_DKP), BF16), pltpu.VMEM((S, GLA_DKP), BF16),
                        pltpu.VMEM((nchunk, 1, GLA_DKP), F32), pltpu.VMEM((S, GLA_DVP), F32),
                        pltpu.VMEM((nchunk, GLA_DVP, GLA_DKP), F32)],
        compiler_params=_params("parallel", "parallel"),
        name="gla",
    )(q, k, v, r, a1, wg, bg, ng)


def _mix_router_kernel(seq_ref, memo_ref, wa_ref, wb_ref, x_ref, g_ref, b_ref, wr_ref, br_ref,
                       x1_ref, comb_ref, *, alpha):
    mixed = _dot(seq_ref[...], wa_ref[...]) + _dot(memo_ref[...], wb_ref[...])
    x1 = _layer_norm(alpha * x_ref[...] + mixed, g_ref[...], b_ref[...])
    x1_ref[...] = x1
    logits = _dot_f32(x1, wr_ref[...]) + br_ref[...]
    lane = lax.broadcasted_iota(I32, logits.shape, 1)
    neg = -jnp.inf
    glog = jnp.where(lane < N_GROUPS, logits, neg)
    gmax = jnp.max(glog, axis=1, keepdims=True)
    gsel = jnp.min(jnp.where(glog == gmax, lane, LANES), axis=1, keepdims=True)
    pg = 1.0 / jnp.sum(jnp.exp(glog - gmax), axis=1, keepdims=True)
    lo = N_GROUPS + EXPERTS_PER_GROUP * gsel
    elog = jnp.where((lane >= lo) & (lane < lo + EXPERTS_PER_GROUP), logits, neg)
    v1 = jnp.max(elog, axis=1, keepdims=True)
    i1 = jnp.min(jnp.where(elog == v1, lane, LANES), axis=1, keepdims=True)
    elog2 = jnp.where(lane == i1, neg, elog)
    v2 = jnp.max(elog2, axis=1, keepdims=True)
    i2 = jnp.min(jnp.where(elog2 == v2, lane, LANES), axis=1, keepdims=True)
    e2 = jnp.exp(v2 - v1)
    den = 1.0 + e2
    comb_ref[...] = (jnp.where(lane == i1 - lo, pg / den, 0.0) + jnp.where(lane == i2 - lo, pg * e2 / den, 0.0)
                     + jnp.where(lane == EXPERTS_PER_GROUP, (gsel.astype(F32)), 0.0))


def _mix_router(seq, memo, wa, wb, x2d, g, b, wr, br, alpha, tm):
    T = x2d.shape[0]
    row = lambda n: pl.BlockSpec((tm, n), lambda i: (i, 0))
    full = lambda a: pl.BlockSpec(a.shape, lambda i: (0,) * a.ndim)
    kern = functools.partial(_mix_router_kernel, alpha=alpha)
    return pl.pallas_call(
        kern,
        grid=(T // tm,),
        in_specs=[row(seq.shape[1]), row(MEM_WIDTH), full(wa), full(wb), row(D_MODEL), full(g), full(b),
                  full(wr), full(br)],
        out_specs=[row(D_MODEL), row(LANES)],
        out_shape=[jax.ShapeDtypeStruct((T, D_MODEL), F32), jax.ShapeDtypeStruct((T, LANES), F32)],
        compiler_params=_params("parallel"),
        name="outproj_ln_router",
    )(seq, memo, wa, wb, x2d, g, b, wr, br)


MOE_ALIGN = 16
MOE_BIG = 64
MOE_TM = 1024
MOE_TR = 1024
MOE_XW = D_MODEL + LANES


def _ceil_to(v, m):
    return (v + (m - 1)) // m * m


def _for_row_chunks(rows, make_copy, fn):
    nbig = rows // MOE_BIG

    def big(c, carry):
        fn(make_copy(pl.multiple_of(c * MOE_BIG, MOE_ALIGN), MOE_BIG))
        return carry

    def small(c, carry):
        fn(make_copy(pl.multiple_of(nbig * MOE_BIG + c * MOE_ALIGN, MOE_ALIGN), MOE_ALIGN))
        return carry

    lax.fori_loop(0, nbig, big, 0)
    lax.fori_loop(0, (rows - nbig * MOE_BIG) // MOE_ALIGN, small, 0)


def _moe_sort_kernel(x_ref, route_ref, xs_ref, meta_ref, bounds_ref, loc_ref, zero_ref, tri_ref, off_ref, seg_ref, sem,
                     *, tm, rloc, ntiles, cap):
    p = pl.program_id(0)
    i = pl.program_id(1)
    rt = route_ref[...].T
    gid = rt[EXPERTS_PER_GROUP:EXPERTS_PER_GROUP + 1, :]
    grp = lax.broadcasted_iota(I32, (8, 1), 0).astype(F32)
    onehot = (gid == grp).astype(F32)
    cnt = jnp.sum(onehot, axis=1, keepdims=True)
    npad = jnp.floor((cnt + (MOE_ALIGN - 1)) * (1.0 / MOE_ALIGN)) * MOE_ALIGN
    for k in range(N_GROUPS):
        seg_ref[N_GROUPS + k] = jnp.max(npad[k:k + 1, :]).astype(I32)

    @pl.when((p == 0) & (i == 0))
    def _():
        for k in range(N_GROUPS):
            off_ref[k] = 0
        zero_ref[...] = jnp.zeros(zero_ref.shape, BF16)
        tri_ref[...] = (lax.broadcasted_iota(I32, (tm, tm), 0) < lax.broadcasted_iota(I32, (tm, tm), 1)).astype(BF16)

    @pl.when(p == 0)
    def _():
        for k in range(N_GROUPS):
            off_ref[k] = off_ref[k] + seg_ref[N_GROUPS + k]

    @pl.when((p == 1) & (i == 0))
    def _():
        base = 0
        for k in range(N_GROUPS):
            total = off_ref[k]
            bounds_ref[k] = base
            off_ref[k] = base
            base = base + _ceil_to(total, MOE_TR)

    def fill_zero(first, last):
        def copy(off, size):
            return pltpu.make_async_copy(zero_ref.at[pl.ds(0, size), :],
                                         xs_ref.at[pl.ds(pl.multiple_of(first + off, MOE_ALIGN), size), :], sem)
        _for_row_chunks(last - first, copy, lambda cp: cp.start())
        _for_row_chunks(last - first, copy, lambda cp: cp.wait())

    @pl.when(p == 1)
    def _():
        rank = _dot(onehot.astype(BF16), tri_ref[...])
        starts = [jnp.zeros((1, 1), F32)]
        for k in range(1, N_GROUPS):
            starts.append(starts[-1] + npad[k - 1:k, :])
        for k in range(N_GROUPS):
            seg_ref[k] = jnp.max(starts[k]).astype(I32)
        start = jnp.concatenate(starts + [jnp.zeros((8 - N_GROUPS, 1), F32)], axis=0)
        dest = jnp.sum(onehot * (start + rank), axis=0, keepdims=True).astype(I32)
        perm = (lax.broadcasted_iota(I32, (rloc, tm), 0) == dest).astype(BF16)
        loc_ref[:, :D_MODEL] = _dot(perm, x_ref[...].astype(BF16)).astype(BF16)
        w = rt[0:EXPERTS_PER_GROUP, :]
        w_hi, w_lo = _split_bf16(w)
        w_lo2 = (w - w_hi.astype(F32) - w_lo.astype(F32)).astype(BF16)
        w_terms = jnp.concatenate([w_hi, w_lo, w_lo2, jnp.zeros((LANES - 3 * EXPERTS_PER_GROUP, tm), BF16)], axis=0)
        loc_ref[:, D_MODEL:] = _dot_nt(perm, w_terms).astype(BF16)

        def for_chunks(fn):
            for k in range(N_GROUPS):
                def copy(off, size, k=k):
                    src = loc_ref.at[pl.ds(pl.multiple_of(seg_ref[k] + off, MOE_ALIGN), size), :]
                    dst = xs_ref.at[pl.ds(pl.multiple_of(off_ref[k] + off, MOE_ALIGN), size), :]
                    return pltpu.make_async_copy(src, dst, sem)
                _for_row_chunks(seg_ref[N_GROUPS + k], copy, fn)

        for_chunks(lambda cp: cp.start())
        for_chunks(lambda cp: cp.wait())
        for k in range(N_GROUPS):
            meta_ref[i * 2 * N_GROUPS + k] = off_ref[k]
            meta_ref[i * 2 * N_GROUPS + N_GROUPS + k] = seg_ref[N_GROUPS + k]
            off_ref[k] = off_ref[k] + seg_ref[N_GROUPS + k]

    @pl.when((p == 1) & (i == ntiles - 1))
    def _():
        for k in range(N_GROUPS):
            bounds_ref[N_GROUPS + k] = off_ref[k]
            fill_zero(off_ref[k], bounds_ref[k + 1] if k + 1 < N_GROUPS else cap)


def _moe_sort(x1, route, tm):
    T = x1.shape[0]
    ntiles = T // tm
    rloc = _ceil_to(tm + N_GROUPS * MOE_ALIGN, LANES)
    cap = _ceil_to(T + MOE_ALIGN * N_GROUPS * ntiles, MOE_TR) + N_GROUPS * MOE_TR
    kern = functools.partial(_moe_sort_kernel, tm=tm, rloc=rloc, ntiles=ntiles, cap=cap)
    smem = pl.BlockSpec(memory_space=pltpu.SMEM)
    return pl.pallas_call(
        kern,
        grid=(2, ntiles),
        in_specs=[pl.BlockSpec((tm, D_MODEL), lambda p, i: (i * p, 0)),
                  pl.BlockSpec((tm, LANES), lambda p, i: (i, 0))],
        out_specs=[pl.BlockSpec(memory_space=pl.ANY), smem, smem],
        out_shape=[jax.ShapeDtypeStruct((cap, MOE_XW), BF16),
                   jax.ShapeDtypeStruct((ntiles * 2 * N_GROUPS,), I32),
                   jax.ShapeDtypeStruct((2 * N_GROUPS,), I32)],
        scratch_shapes=[pltpu.VMEM((rloc, MOE_XW), BF16), pltpu.VMEM((MOE_BIG, MOE_XW), BF16),
                        pltpu.VMEM((tm, tm), BF16),
                        pltpu.SMEM((N_GROUPS,), I32), pltpu.SMEM((2 * N_GROUPS,), I32), pltpu.SemaphoreType.DMA],
        compiler_params=_params("arbitrary", "arbitrary"),
        name="moe_sort_dispatch",
    )(x1, route)


def _moe_tile_group(r, bounds_ref):
    row = r * MOE_TR
    g = 0
    for k in range(1, N_GROUPS):
        g = g + (row >= bounds_ref[k]).astype(I32)
    return g


def _moe_mlp_kernel(bounds_ref, xs_ref, w13_ref, w2_ref, ys_ref, acc_ref):
    r = pl.program_id(0)
    g = _moe_tile_group(r, bounds_ref)
    end = bounds_ref[N_GROUPS]
    for k in range(1, N_GROUPS):
        end = jnp.where(g == k, bounds_ref[N_GROUPS + k], end)
    used = r * MOE_TR < end

    @pl.when(used)
    def _():
        xb = xs_ref[:, :D_MODEL]
        terms = xs_ref[:, D_MODEL:].astype(F32)
        cw = terms + pltpu.roll(terms, LANES - EXPERTS_PER_GROUP, 1) + pltpu.roll(terms, LANES - 2 * EXPERTS_PER_GROUP, 1)
        for e in range(EXPERTS_PER_GROUP):
            h = _dot(xb, w13_ref[0, :, e * 2 * EXPERT_FF:(e + 1) * 2 * EXPERT_FF])
            a = h[:, :EXPERT_FF]
            u = h[:, EXPERT_FF:]
            act = ((a * jax.nn.sigmoid(a)) * u * cw[:, e:e + 1]).astype(BF16)
            y = _dot(act, w2_ref[0, e * EXPERT_FF:(e + 1) * EXPERT_FF, :])
            if e == 0:
                acc_ref[...] = y
            else:
                acc_ref[...] += y
        ys_ref[...] = acc_ref[...].astype(BF16)

    @pl.when(jnp.logical_not(used))
    def _():
        ys_ref[...] = jnp.zeros(ys_ref.shape, BF16)


def _moe_mlp(bounds, xs, w13, w2):
    cap = xs.shape[0]
    wmap = lambda r, b: (_moe_tile_group(r, b), 0, 0)
    grid_spec = pltpu.PrefetchScalarGridSpec(
        num_scalar_prefetch=1,
        grid=(cap // MOE_TR,),
        in_specs=[pl.BlockSpec((MOE_TR, MOE_XW), lambda r, b: (r, 0)),
                  pl.BlockSpec((1, D_MODEL, EXPERTS_PER_GROUP * 2 * EXPERT_FF), wmap),
                  pl.BlockSpec((1, EXPERTS_PER_GROUP * EXPERT_FF, D_MODEL), wmap)],
        out_specs=pl.BlockSpec((MOE_TR, D_MODEL), lambda r, b: (r, 0)),
        scratch_shapes=[pltpu.VMEM((MOE_TR, D_MODEL), F32)])
    return pl.pallas_call(
        _moe_mlp_kernel,
        grid_spec=grid_spec,
        out_shape=jax.ShapeDtypeStruct((cap, D_MODEL), BF16),
        compiler_params=_params("arbitrary"),
        name="moe_group_experts",
    )(bounds, xs, w13, w2)


def _moe_combine_kernel(meta_ref, x_ref, route_ref, ys_ref, g_ref, b_ref, o_ref, loc_ref, tri_ref, sem,
                        *, alpha, tm, rloc):
    i = pl.program_id(0)
    offs = [meta_ref[i * 2 * N_GROUPS + k] for k in range(N_GROUPS)]
    rows = [meta_ref[i * 2 * N_GROUPS + N_GROUPS + k] for k in range(N_GROUPS)]
    segs = [0]
    for k in range(N_GROUPS):
        segs.append(segs[-1] + rows[k])

    def for_chunks(fn):
        for k in range(N_GROUPS):
            def copy(off, size, k=k):
                src = ys_ref.at[pl.ds(pl.multiple_of(offs[k] + off, MOE_ALIGN), size), :]
                dst = loc_ref.at[pl.ds(pl.multiple_of(segs[k] + off, MOE_ALIGN), size), :]
                return pltpu.make_async_copy(src, dst, sem)
            _for_row_chunks(rows[k], copy, fn)

    for_chunks(lambda cp: cp.start())
    route = route_ref[...]
    lane = lax.broadcasted_iota(I32, (1, LANES), 1)
    onehot = ((route[:, EXPERTS_PER_GROUP:EXPERTS_PER_GROUP + 1] == lane.astype(F32)) & (lane < N_GROUPS)).astype(F32)
    @pl.when(i == 0)
    def _():
        tri_ref[...] = (lax.broadcasted_iota(I32, (tm, tm), 1) < lax.broadcasted_iota(I32, (tm, tm), 0)).astype(BF16)

    rank = _dot(tri_ref[...], onehot.astype(BF16))
    seg_start = jnp.zeros((1, LANES), F32)
    for k in range(N_GROUPS):
        seg_start = seg_start + jnp.where(lane == k, lax.convert_element_type(segs[k], F32), F32(0.0))
    dest = jnp.sum(onehot * (seg_start + rank), axis=1, keepdims=True).astype(I32)
    unperm = (lax.broadcasted_iota(I32, (1, rloc), 1) == dest).astype(BF16)
    for_chunks(lambda cp: cp.wait())
    valid = lax.broadcasted_iota(I32, (rloc, 1), 0) < segs[N_GROUPS]
    ysl = jnp.where(valid, loc_ref[...], jnp.zeros((rloc, D_MODEL), BF16))
    o_ref[...] = _layer_norm(alpha * x_ref[...] + _dot(unperm, ysl), g_ref[...], b_ref[...])


def _moe_combine(meta, x1, route, ys, g, b, alpha, tm):
    T = x1.shape[0]
    rloc = _ceil_to(tm + N_GROUPS * MOE_ALIGN, LANES)
    kern = functools.partial(_moe_combine_kernel, alpha=alpha, tm=tm, rloc=rloc)
    grid_spec = pltpu.PrefetchScalarGridSpec(
        num_scalar_prefetch=1,
        grid=(T // tm,),
        in_specs=[pl.BlockSpec((tm, D_MODEL), lambda i, m: (i, 0)),
                  pl.BlockSpec((tm, LANES), lambda i, m: (i, 0)),
                  pl.BlockSpec(memory_space=pl.ANY),
                  pl.BlockSpec((1, D_MODEL), lambda i, m: (0, 0)),
                  pl.BlockSpec((1, D_MODEL), lambda i, m: (0, 0))],
        out_specs=pl.BlockSpec((tm, D_MODEL), lambda i, m: (i, 0)),
        scratch_shapes=[pltpu.VMEM((rloc, D_MODEL), BF16), pltpu.VMEM((tm, tm), BF16), pltpu.SemaphoreType.DMA])
    return pl.pallas_call(
        kern,
        grid_spec=grid_spec,
        out_shape=jax.ShapeDtypeStruct((T, D_MODEL), F32),
        compiler_params=_params("arbitrary"),
        name="moe_combine_ln",
    )(meta, x1, route, ys, g, b)


def _pad_cols(a, width):
    return jnp.pad(a, ((0, 0), (0, width - a.shape[1])))


def _pad_heads(w, heads, dim, dim_pad):
    rows = w.shape[0]
    return jnp.pad(w.reshape(rows, heads, dim), ((0, 0), (0, 0), (0, dim_pad - dim))).reshape(rows, heads * dim_pad)


def _tile(n, pref):
    t = pref
    while n % t:
        t //= 2
    return t


def kernel(x, mem, positions, dsa_w_in, dsa_idx_k_g, dsa_idx_k_b, gla_w_in, gla_w_gate, gla_b_gate, gla_norm_g,
           w_mem_kv, w_out, ln1_g, ln1_b, ln2_g, ln2_b, moe_w_group, moe_b_group, moe_w_router, moe_b_router,
           moe_w13, moe_w2):
    B, S, D = x.shape
    T = B * S
    depth = w_out.shape[0]
    alpha = (2 * depth) ** 0.25
    tm = _tile(T, 512)
    tq = _tile(S, 512)
    tq_idx = _tile(S, 256)
    topk = min(DSA_MAX_TOPK, S // 4)

    inv = ROPE_THETA ** (-jnp.arange(0, HEAD_DIM, 2, dtype=F32) / HEAD_DIM)
    ang = positions.astype(F32).reshape(T, 1) * inv
    cos, sin = jnp.cos(ang), jnp.sin(ang)
    cosf = jnp.concatenate([cos, cos, cos, cos], axis=1)
    sinf = jnp.concatenate([-sin, sin, -sin, sin], axis=1)

    xc = x.reshape(T, D)
    ia = ib = 0
    for i in range(depth):
        if i % 2 == 0:
            w = dsa_w_in[ia]
            wq, wk, wv, wqi, wki, wwi, wqm = jnp.split(w, [768, 1536, 2304, 2816, 2880, 2888], axis=1)
            w_tok = jnp.concatenate([wk, wqm, _pad_cols(wki, LANES)], axis=1).astype(BF16)
            w_feat = jnp.concatenate([wq, wv, wqi, _pad_cols(wwi, 16)], axis=1).T.astype(BF16)
            lng = _pad_cols(dsa_idx_k_g[ia][None, :], LANES)
            lnb = _pad_cols(dsa_idx_k_b[ia][None, :], LANES)
            k, qm, ki, q_t, v_t, qi_t, wi_t = _proj_dsa(xc, w_tok, w_feat, cosf, sinf, cosf.T, sinf.T, lng, lnb,
                                                         tm, tq, tq_idx)
            bias = _idx_mask(qi_t, ki.reshape(B, S, LANES), wi_t, B, S, topk, tq_idx)
            seq = _dsa_attn(q_t, k.reshape(B, S, SEQ_WIDTH), v_t, bias, B, S, tq, 2).reshape(T, SEQ_WIDTH)
            wa = w_out[i][:SEQ_WIDTH].astype(BF16)
            ia += 1
        else:
            w = gla_w_in[ib]
            wq, wk, wv, wr_, wa1, wqm = jnp.split(w, [384, 768, 1536, 2304, 2320], axis=1)
            w_all = jnp.concatenate([
                _pad_heads(wq, GLA_HEADS, GLA_DK, GLA_DKP), _pad_heads(wk, GLA_HEADS, GLA_DK, GLA_DKP),
                _pad_heads(wv, GLA_HEADS, GLA_DV, GLA_DVP), _pad_heads(wr_, GLA_HEADS, GLA_DV, GLA_DVP),
                wqm, _pad_cols(wa1, LANES)], axis=1).astype(BF16)
            q, k, v, r, qm, a1 = _proj_gla(xc, w_all, tm)
            r3 = lambda a: a.reshape(B, S, a.shape[1])
            wg = _pad_heads(gla_w_gate[ib], GLA_HEADS, GLA_DK, GLA_DKP)
            wg = jnp.pad(wg, ((0, LANES - GLA_GATE_RANK), (0, 0)))
            wg = wg.reshape(LANES, GLA_HEADS, GLA_DKP).transpose(1, 0, 2)
            bg = _pad_heads(gla_b_gate[ib][None, :], GLA_HEADS, GLA_DK, GLA_DKP).reshape(GLA_HEADS, 1, GLA_DKP)
            ng = _pad_cols(gla_norm_g[ib][None, :], GLA_DVP)
            seq = _gla(r3(q), r3(k), r3(v), r3(r), r3(a1), wg, bg, ng).reshape(T, GLA_HEADS * GLA_DVP)
            wa = w_out[i][:SEQ_WIDTH].reshape(GLA_HEADS, GLA_DV, D)
            wa = jnp.pad(wa, ((0, 0), (0, GLA_DVP - GLA_DV), (0, 0))).reshape(GLA_HEADS * GLA_DVP, D).astype(BF16)
            ib += 1
        memo = _mem_attn(qm.reshape(B, S, MEM_WIDTH), mem, w_mem_kv[i].astype(BF16), tq).reshape(T, MEM_WIDTH)
        wb = w_out[i][SEQ_WIDTH:].astype(BF16)
        wr = jnp.concatenate([moe_w_group[i], moe_w_router[i].transpose(1, 0, 2).reshape(D, N_EXPERTS)], axis=1)
        wr = _pad_cols(wr, LANES)
        br = _pad_cols(jnp.concatenate([moe_b_group[i], moe_b_router[i].reshape(-1)])[None, :], LANES)
        x1, route = _mix_router(seq, memo, wa, wb, xc, ln1_g[i][None, :], ln1_b[i][None, :], wr, br, alpha, tm)
        w13g = moe_w13[i].transpose(0, 2, 1, 3).reshape(N_GROUPS, D, EXPERTS_PER_GROUP * 2 * EXPERT_FF).astype(BF16)
        w2g = moe_w2[i].reshape(N_GROUPS, EXPERTS_PER_GROUP * EXPERT_FF, D).astype(BF16)
        tmoe = _tile(T, MOE_TM)
        xs, meta, bounds = _moe_sort(x1, route, tmoe)
        ys = _moe_mlp(bounds, xs, w13g, w2g)
        xc = _moe_combine(meta, x1, route, ys, ln2_g[i][None, :], ln2_b[i][None, :], alpha, tmoe)
    return xc.reshape(B, S, D)
```

```python
import functools
import math

import jax
import jax.numpy as jnp
from jax import lax
from jax.experimental import pallas as pl
from jax.experimental.pallas import tpu as pltpu

F32 = jnp.float32
BF16 = jnp.bfloat16
I32 = jnp.int32
I16 = jnp.int16

LANES = 128
D_MODEL = 1024
HEAD_DIM = 64
N_MEM_HEADS = 4
MEM_WIDTH = N_MEM_HEADS * HEAD_DIM
SEQ_WIDTH = D_MODEL - MEM_WIDTH
ROPE_THETA = 10000.0
DSA_HEADS = SEQ_WIDTH // HEAD_DIM
IDX_HEADS = 8
IDX_DIM = 64
DSA_MAX_TOPK = 256
GLA_HEADS = 4
GLA_DV = SEQ_WIDTH // GLA_HEADS
GLA_DK = GLA_DV // 2
GLA_DKP = 128
GLA_DVP = 256
GLA_GATE_RANK = 16
GLA_TAU = 16.0
GLA_CHUNK = 64
N_GROUPS = 4
EXPERTS_PER_GROUP = 8
N_EXPERTS = N_GROUPS * EXPERTS_PER_GROUP
EXPERT_FF = 256
LN_EPS = 1e-5
RMS_EPS = 1e-6
MASK_BIAS = -1e30
INT_MIN = -2 ** 31
VMEM_LIMIT = 56 * 1024 * 1024


def _dot(a, b):
    return jnp.dot(a, b, preferred_element_type=F32)


def _dot_nt(a, b):
    return lax.dot_general(a, b, (((1,), (1,)), ((), ())), preferred_element_type=F32)


def _dot_tn(a, b):
    return lax.dot_general(a, b, (((0,), (0,)), ((), ())), preferred_element_type=F32)


def _split_bf16(a):
    hi = a.astype(BF16)
    lo = (a - hi.astype(F32)).astype(BF16)
    return hi, lo


def _dot_f32(a, b):
    ah, al = _split_bf16(a)
    bh, bl = _split_bf16(b)
    return _dot(ah, bh) + (_dot(ah, bl) + _dot(al, bh))


def _layer_norm(y, g, b):
    mu = jnp.mean(y, axis=-1, keepdims=True)
    yc = y - mu
    var = jnp.mean(yc * yc, axis=-1, keepdims=True)
    return yc * lax.rsqrt(var + LN_EPS) * g + b


def _params(*sem):
    return pltpu.CompilerParams(dimension_semantics=sem, vmem_limit_bytes=VMEM_LIMIT)


def _rope_fn(cos, sin, axis):
    shape = (1, LANES) if axis == 1 else (LANES, 1)
    pos = lax.broadcasted_iota(I32, shape, axis)
    first_half = (pos % HEAD_DIM) < (HEAD_DIM // 2)

    def rope(a):
        swapped = jnp.where(first_half, pltpu.roll(a, LANES - HEAD_DIM // 2, axis),
                            pltpu.roll(a, HEAD_DIM // 2, axis))
        return a * cos + swapped * sin
    return rope


def _proj_dsa_kernel(x_ref, wa_ref, wb_ref, cos_ref, sin_ref, cost_ref, sint_ref, lng_ref, lnb_ref,
                     k_ref, qm_ref, ki_ref, qt_ref, vt_ref, qit_ref, wit_ref):
    xb = x_ref[...].astype(BF16)
    rope = _rope_fn(cos_ref[...], sin_ref[...], 1)
    rope_t = _rope_fn(cost_ref[...], sint_ref[...], 0)
    tok = _dot(xb, wa_ref[...])
    for c in range(SEQ_WIDTH // LANES):
        k_ref[:, c * LANES:(c + 1) * LANES] = rope(tok[:, c * LANES:(c + 1) * LANES]).astype(BF16)
    qm_ref[...] = tok[:, SEQ_WIDTH:SEQ_WIDTH + MEM_WIDTH].astype(BF16)
    acc = tok[:, D_MODEL:D_MODEL + LANES]
    lane = lax.broadcasted_iota(I32, (1, LANES), 1)
    is_ki = lane < IDX_DIM
    mu = jnp.sum(jnp.where(is_ki, acc, 0.0), axis=1, keepdims=True) * (1.0 / IDX_DIM)
    d = jnp.where(is_ki, acc - mu, 0.0)
    var = jnp.sum(d * d, axis=1, keepdims=True) * (1.0 / IDX_DIM)
    ki_ref[...] = rope(d * lax.rsqrt(var + LN_EPS) * lng_ref[...] + lnb_ref[...]).astype(BF16)

    def store_t(out_ref, r, val):
        nblk, _, tb = out_ref.shape
        rows = val.shape[0]
        for j in range(nblk):
            out_ref[j, r * rows:(r + 1) * rows, :] = val[:, j * tb:(j + 1) * tb].astype(out_ref.dtype)

    q_scale = HEAD_DIM ** -0.5 * math.log2(math.e)
    feat = _dot_nt(wb_ref[...], xb)
    for r in range(SEQ_WIDTH // LANES):
        store_t(qt_ref, r, rope_t(feat[r * LANES:(r + 1) * LANES]) * q_scale)
    off = SEQ_WIDTH
    for r in range(SEQ_WIDTH // LANES):
        store_t(vt_ref, r, feat[off + r * LANES:off + (r + 1) * LANES])
    off = 2 * SEQ_WIDTH
    for r in range(IDX_HEADS * IDX_DIM // LANES):
        store_t(qit_ref, r, rope_t(feat[off + r * LANES:off + (r + 1) * LANES]))
    off = 2 * SEQ_WIDTH + IDX_HEADS * IDX_DIM
    store_t(wit_ref, 0, feat[off:off + IDX_HEADS] * (IDX_HEADS ** -0.5 * IDX_DIM ** -0.5))


def _proj_dsa(x2d, wa, wb, cosf, sinf, cost, sint, lng, lnb, tm, tq_att, tq_idx):
    T = x2d.shape[0]
    row = lambda n: pl.BlockSpec((tm, n), lambda i: (i, 0))
    col = pl.BlockSpec((LANES, tm), lambda i: (0, i))
    full = lambda a: pl.BlockSpec(a.shape, lambda i: (0,) * a.ndim)
    featmaj = lambda n, tb: pl.BlockSpec((tm // tb, n, tb), lambda i: (i, 0, 0))
    fshape = lambda n, tb, dt: jax.ShapeDtypeStruct((T // tb, n, tb), dt)
    return pl.pallas_call(
        _proj_dsa_kernel,
        grid=(T // tm,),
        in_specs=[row(D_MODEL), full(wa), full(wb), row(LANES), row(LANES), col, col, full(lng), full(lnb)],
        out_specs=[row(SEQ_WIDTH), row(MEM_WIDTH), row(LANES),
                   featmaj(SEQ_WIDTH, tq_att), featmaj(SEQ_WIDTH, tq_att),
                   featmaj(IDX_HEADS * IDX_DIM, tq_idx), featmaj(IDX_HEADS, tq_idx)],
        out_shape=[jax.ShapeDtypeStruct((T, SEQ_WIDTH), BF16), jax.ShapeDtypeStruct((T, MEM_WIDTH), BF16),
                   jax.ShapeDtypeStruct((T, LANES), BF16),
                   fshape(SEQ_WIDTH, tq_att, BF16), fshape(SEQ_WIDTH, tq_att, BF16),
                   fshape(IDX_HEADS * IDX_DIM, tq_idx, BF16), fshape(IDX_HEADS, tq_idx, F32)],
        compiler_params=_params("parallel"),
        name="proj_dsa",
    )(x2d, wa, wb, cosf, sinf, cost, sint, lng, lnb)


def _proj_gla_kernel(x_ref, w_ref, q_ref, k_ref, v_ref, r_ref, qm_ref, a_ref):
    xb = x_ref[...].astype(BF16)
    segments = ((0, 512, q_ref), (512, 512, k_ref), (1024, 1024, v_ref), (2048, 1024, r_ref),
                (3072, 256, qm_ref), (3328, 128, a_ref))
    for off, width, out_ref in segments:
        out_ref[...] = _dot(xb, w_ref[:, off:off + width]).astype(out_ref.dtype)


def _proj_gla(x2d, w, tm):
    T = x2d.shape[0]
    row = lambda n: pl.BlockSpec((tm, n), lambda i: (i, 0))
    outs = [(512, BF16), (512, BF16), (1024, BF16), (1024, BF16), (256, BF16), (LANES, F32)]
    return pl.pallas_call(
        _proj_gla_kernel,
        grid=(T // tm,),
        in_specs=[row(D_MODEL), pl.BlockSpec(w.shape, lambda i: (0, 0))],
        out_specs=[row(n) for n, _ in outs],
        out_shape=[jax.ShapeDtypeStruct((T, n), dt) for n, dt in outs],
        compiler_params=_params("parallel"),
        name="proj_gla",
    )(x2d, w)


def _idx_kernel(qi_ref, ki_ref, wi_ref, bias_ref, keys_ref, hi_ref, lo_ref, cut_ref, *, tq, tk, nkb, topk, seq):
    qt = pl.program_id(1)
    n_act = qt + 1
    key_l = lax.broadcasted_iota(I32, (tk, tq), 0)
    qry_g = qt * tq + lax.broadcasted_iota(I32, (tk, tq), 1)
    wi = wi_ref[0]

    def score_body(kb, carry):
        kblk = ki_ref[0, pl.ds(pl.multiple_of(kb * tk, tk), tk), :][:, :IDX_DIM]
        sc = jnp.zeros((tk, tq), F32)
        for h in range(IDX_HEADS):
            qh = qi_ref[0, h * IDX_DIM:(h + 1) * IDX_DIM, :]
            sc = sc + jnp.maximum(_dot(kblk, qh), 0.0) * wi[h:h + 1, :]
        bits = lax.bitcast_convert_type(sc, I32)
        key = bits ^ ((bits >> 31) & 0x7FFFFFFF)
        key = jnp.where(kb * tk + key_l > qry_g, INT_MIN, key)
        keys_ref[kb] = key
        hi_ref[kb] = (key >> 16).astype(I16)
        lo_ref[kb] = (key ^ 0x8000).astype(I16)
        return carry

    lax.fori_loop(0, n_act, score_body, 0)

    def count16(ref, cand, strict=False):
        cand16 = cand.astype(I16)

        def body(kb, acc):
            v = ref[kb]
            m = jnp.where((v > cand16) if strict else (v >= cand16), jnp.int16(1), jnp.int16(0))
            parts = [m[j * 16:(j + 1) * 16] for j in range(4)]
            for j in range(4, tk // 16):
                parts[j % 4] = parts[j % 4] + m[j * 16:(j + 1) * 16]
            return acc + ((parts[0] + parts[1]) + (parts[2] + parts[3]))
        acc = lax.fori_loop(0, n_act, body, jnp.zeros((16, tq), I16))
        return jnp.sum(acc.astype(I32).astype(F32), axis=0, keepdims=True)

    def search16(ref, need):
        def bit_body(i, base_u):
            cand_u = base_u | lax.shift_left(jnp.int32(1), lax.convert_element_type(15 - i, I32))
            cnt = count16(ref, cand_u - 32768)
            return jnp.where(cnt >= need, cand_u, base_u)
        return lax.fori_loop(0, 16, bit_body, jnp.zeros((1, tq), I32)) - 32768

    base_hi = search16(hi_ref, topk)
    above = count16(hi_ref, base_hi, strict=True)
    base_hi16 = base_hi.astype(I16)

    def bucket_body(kb, carry):
        lo_ref[kb] = jnp.where(hi_ref[kb] == base_hi16, lo_ref[kb], jnp.int16(-32768))
        return carry

    lax.fori_loop(0, n_act, bucket_body, 0)
    base_lo = search16(lo_ref, topk - above)
    base = lax.shift_left(base_hi, jnp.int32(16)) | (base_lo + 32768)

    def count(pred):
        def body(kb, acc):
            m = pred(keys_ref[kb], kb).astype(F32)
            parts = [m[j * 8:(j + 1) * 8] for j in range(4)]
            for j in range(4, tk // 8):
                parts[j % 4] = parts[j % 4] + m[j * 8:(j + 1) * 8]
            return acc + ((parts[0] + parts[1]) + (parts[2] + parts[3]))
        acc = lax.fori_loop(0, n_act, body, jnp.zeros((8, tq), F32))
        return jnp.sum(acc, axis=0, keepdims=True)

    cnt_gt = count(lambda kk, kb: kk > base)
    cnt_ge = count(lambda kk, kb: kk >= base)
    need = topk - cnt_gt
    tie = (cnt_ge > topk) & (base != INT_MIN)
    cut_ref[...] = jnp.full((1, tq), seq, I32)

    @pl.when(jnp.max(tie.astype(F32)) > 0.0)
    def _():
        nbits = int(math.log2(seq))

        def idx_body(i, m):
            cand = m | lax.shift_left(jnp.int32(1), lax.convert_element_type(nbits - 1 - i, I32))
            below = count(lambda kk, kb: (kk == base) & (kb * tk + key_l < cand))
            return jnp.where(below < need, cand, m)

        m = lax.fori_loop(0, nbits, idx_body, jnp.zeros((1, tq), I32))
        cut_ref[...] = jnp.where(tie, m, seq)

    cut = cut_ref[...]
    for kb in range(nkb):
        @pl.when(kb <= qt)
        def _():
            kk = keys_ref[kb]
            sel = ((kk > base) | ((kk == base) & (kb * tk + key_l <= cut))) & (kk != INT_MIN)
            bias_ref[0, 0, kb] = jnp.where(sel, F32(0.0), F32(MASK_BIAS)).astype(BF16)

        @pl.when(kb > qt)
        def _():
            bias_ref[0, 0, kb] = jnp.full((tk, tq), MASK_BIAS, BF16)


def _idx_mask(qit, ki, wit, B, S, topk, tq):
    nq = S // tq
    kern = functools.partial(_idx_kernel, tq=tq, tk=tq, nkb=nq, topk=topk, seq=S)
    return pl.pallas_call(
        kern,
        grid=(B, nq),
        in_specs=[pl.BlockSpec((1, IDX_HEADS * IDX_DIM, tq), lambda b, q: (b * nq + q, 0, 0)),
                  pl.BlockSpec((1, S, LANES), lambda b, q: (b, 0, 0)),
                  pl.BlockSpec((1, IDX_HEADS, tq), lambda b, q: (b * nq + q, 0, 0))],
        out_specs=pl.BlockSpec((1, 1, nq, tq, tq), lambda b, q: (b, q, 0, 0, 0)),
        out_shape=jax.ShapeDtypeStruct((B, nq, nq, tq, tq), BF16),
        scratch_shapes=[pltpu.VMEM((nq, tq, tq), I32), pltpu.VMEM((nq, tq, tq), I16), pltpu.VMEM((nq, tq, tq), I16),
                        pltpu.VMEM((1, tq), I32)],
        compiler_params=_params("parallel", "arbitrary"),
        name="dsa_index_select",
    )(qit, ki, wit)


def _dsa_attn_kernel(q_ref, k_ref, v_ref, bias_ref, o_ref, m_ref, acc_ref, *, tq, tk, sub):
    qt = pl.program_id(2)
    nsub = tq // sub
    npair = q_ref.shape[1] // LANES
    nh = 2 * npair
    low = lax.broadcasted_iota(I32, (LANES, 1), 0) < HEAD_DIM
    q_heads = []
    for pr in range(npair):
        q = q_ref[0, pr * LANES:(pr + 1) * LANES, :]
        zero = jnp.zeros_like(q)
        q_heads += [jnp.where(low, q, zero), jnp.where(low, zero, q)]
    m_ref[...] = jnp.full(m_ref.shape, -jnp.inf, F32)
    acc_ref[...] = jnp.zeros(acc_ref.shape, F32)

    def body(kb, carry):
        kblk = k_ref[0, pl.ds(pl.multiple_of(kb * tk, tk), tk), :]
        v_heads = []
        for pr in range(npair):
            vt = v_ref[kb, pr * LANES:(pr + 1) * LANES, :]
            one = jnp.ones_like(vt)
            v_heads += [jnp.where(low, vt, one), jnp.where(low, one, vt)]
        bias = jnp.concatenate(
            [jnp.concatenate([bias_ref[0, c, kb * nsub + a] for c in range(nsub)], axis=1) for a in range(nsub)],
            axis=0).astype(F32)
        s = [_dot(kblk[:, (h // 2) * LANES:(h // 2 + 1) * LANES], q_heads[h]) + bias for h in range(nh)]
        m_old = [m_ref[h] for h in range(nh)]
        m_new = [jnp.maximum(m_old[h], jnp.max(s[h], axis=0, keepdims=True)) for h in range(nh)]
        p = [jnp.exp2(s[h] - m_new[h]).astype(BF16) for h in range(nh)]
        pv = [_dot(v_heads[h], p[h]) for h in range(nh)]
        for h in range(nh):
            acc_ref[h] = jnp.exp2(m_old[h] - m_new[h]) * acc_ref[h] + pv[h]
            m_ref[h] = m_new[h]
        return carry

    lax.fori_loop(0, qt + 1, body, 0)
    for pr in range(npair):
        a0 = acc_ref[2 * pr]
        a1 = acc_ref[2 * pr + 1]
        o = jnp.where(low, a0 / a0[HEAD_DIM:HEAD_DIM + 1, :], a1 / a1[0:1, :])
        o_ref[0, :, pr * LANES:(pr + 1) * LANES] = o.T.astype(o_ref.dtype)


def _dsa_attn(qt, k, vt, bias, B, S, tq, npair):
    nq = S // tq
    width = npair * LANES
    nsb, sub = bias.shape[2], bias.shape[3]
    kern = functools.partial(_dsa_attn_kernel, tq=tq, tk=tq, sub=sub)
    return pl.pallas_call(
        kern,
        grid=(B, SEQ_WIDTH // width, nq),
        in_specs=[pl.BlockSpec((1, width, tq), lambda b, h, i: (b * nq + i, h, 0)),
                  pl.BlockSpec((1, S, width), lambda b, h, i: (b, 0, h)),
                  pl.BlockSpec((nq, width, tq), lambda b, h, i: (b, h, 0)),
                  pl.BlockSpec((1, tq // sub, nsb, sub, sub), lambda b, h, i: (b, i, 0, 0, 0))],
        out_specs=pl.BlockSpec((1, tq, width), lambda b, h, i: (b, i, h)),
        out_shape=jax.ShapeDtypeStruct((B, S, SEQ_WIDTH), BF16),
        scratch_shapes=[pltpu.VMEM((2 * npair, 1, tq), F32), pltpu.VMEM((2 * npair, LANES, tq), F32)],
        compiler_params=_params("parallel", "parallel", "arbitrary"),
        name="dsa_attention",
    )(qt, k, vt, bias)


def _mem_attn_kernel(qm_ref, mem_ref, wkv_ref, o_ref, k_scr, v_scr):
    @pl.when(pl.program_id(1) == 0)
    def _():
        kv = _dot(mem_ref[0].astype(BF16), wkv_ref[...])
        k_scr[...] = kv[:, :MEM_WIDTH].astype(BF16)
        v_scr[...] = kv[:, MEM_WIDTH:].astype(BF16)

    lane = lax.broadcasted_iota(I32, (1, LANES), 1)
    low = lane < HEAD_DIM
    scale = HEAD_DIM ** -0.5
    for pair in range(MEM_WIDTH // LANES):
        cols = slice(pair * LANES, (pair + 1) * LANES)
        q = qm_ref[0, :, cols]
        zero = jnp.zeros_like(q)
        kp = k_scr[:, cols]
        vp = v_scr[:, cols]
        outs = []
        for qh in (jnp.where(low, q, zero), jnp.where(low, zero, q)):
            s = _dot_nt(qh, kp) * scale
            s = s - jnp.max(s, axis=1, keepdims=True)
            p = jnp.exp(s)
            p = p / jnp.sum(p, axis=1, keepdims=True)
            outs.append(_dot(p.astype(BF16), vp))
        o_ref[0, :, cols] = jnp.where(low, outs[0], outs[1]).astype(o_ref.dtype)


def _mem_attn(qm, mem, wkv, tq):
    B, S, _ = qm.shape
    M = mem.shape[1]
    return pl.pallas_call(
        _mem_attn_kernel,
        grid=(B, S // tq),
        in_specs=[pl.BlockSpec((1, tq, MEM_WIDTH), lambda b, i: (b, i, 0)),
                  pl.BlockSpec((1, M, D_MODEL), lambda b, i: (b, 0, 0)),
                  pl.BlockSpec(wkv.shape, lambda b, i: (0, 0))],
        out_specs=pl.BlockSpec((1, tq, MEM_WIDTH), lambda b, i: (b, i, 0)),
        out_shape=jax.ShapeDtypeStruct((B, S, MEM_WIDTH), BF16),
        scratch_shapes=[pltpu.VMEM((M, MEM_WIDTH), BF16), pltpu.VMEM((M, MEM_WIDTH), BF16)],
        compiler_params=_params("parallel", "arbitrary"),
        name="memory_attention",
    )(qm, mem, wkv)


def _gla_kernel(q_ref, k_ref, v_ref, r_ref, a_ref, wg_ref, bg_ref, ng_ref, o_ref,
                qb_ref, kb_ref, kd_ref, dl_ref, oi_ref, kv_ref, *, nchunk):
    C = GLA_CHUNK
    S = nchunk * C
    z = _dot_f32(a_ref[0], wg_ref[0]) + bg_ref[0]
    b = (jnp.minimum(z, 0.0) - jnp.log1p(jnp.exp(-jnp.abs(z)))) * (1.0 / GLA_TAU)
    pos = lax.broadcasted_iota(I32, (S, 1), 0) % C
    shift = 1
    while shift < C:
        b = b + jnp.where(pos >= shift, pltpu.roll(b, shift, 0), 0.0)
        shift *= 2
    b3 = b.reshape(nchunk, C, GLA_DKP)
    b_last = b3[:, C - 1:C, :]
    q = q_ref[0].astype(F32) * (GLA_DK ** -0.5)
    k = k_ref[0].astype(F32)
    qb_ref[...] = (q * jnp.exp(b)).astype(BF16)
    kb_ref[...] = (k * jnp.exp(-b)).astype(BF16)
    kd_ref[...] = (k.reshape(nchunk, C, GLA_DKP) * jnp.exp(b_last - b3)).reshape(S, GLA_DKP).astype(BF16)
    dl_ref[...] = jnp.exp(b_last)
    causal = lax.broadcasted_iota(I32, (C, C), 0) >= lax.broadcasted_iota(I32, (C, C), 1)

    def intra(c, carry):
        rows = pl.ds(pl.multiple_of(c * C, C), C)
        v = v_ref[0, rows, :]
        attn = jnp.where(causal, _dot_nt(qb_ref[rows, :], kb_ref[rows, :]), 0.0)
        oi_ref[rows, :] = _dot(attn.astype(BF16), v)
        kv_ref[c] = _dot_tn(v, kd_ref[rows, :])
        return carry

    lax.fori_loop(0, nchunk, intra, 0, unroll=8)
    ng = ng_ref[...]

    def inter(c, st):
        rows = pl.ds(pl.multiple_of(c * C, C), C)
        o = oi_ref[rows, :] + _dot_nt(qb_ref[rows, :], st.astype(BF16))
        ms = jnp.sum(o * o, axis=1, keepdims=True) * (1.0 / GLA_DV)
        o = o * lax.rsqrt(ms + RMS_EPS) * ng
        r = r_ref[0, rows, :].astype(F32)
        o_ref[0, rows, :] = (o * (r * jax.nn.sigmoid(r))).astype(o_ref.dtype)
        return st * dl_ref[c] + kv_ref[c]

    lax.fori_loop(0, nchunk, inter, jnp.zeros((GLA_DVP, GLA_DKP), F32), unroll=4)


def _gla(q, k, v, r, a1, wg, bg, ng):
    B, S, _ = q.shape
    nchunk = S // GLA_CHUNK
    kern = functools.partial(_gla_kernel, nchunk=nchunk)
    kspec = pl.BlockSpec((1, S, GLA_DKP), lambda b, h: (b, 0, h))
    vspec = pl.BlockSpec((1, S, GLA_DVP), lambda b, h: (b, 0, h))
    return pl.pallas_call(
        kern,
        grid=(B, GLA_HEADS),
        in_specs=[kspec, kspec, vspec, vspec,
                  pl.BlockSpec((1, S, LANES), lambda b, h: (b, 0, 0)),
                  pl.BlockSpec((1, LANES, GLA_DKP), lambda b, h: (h, 0, 0)),
                  pl.BlockSpec((1, 1, GLA_DKP), lambda b, h: (h, 0, 0)),
                  pl.BlockSpec((1, GLA_DVP), lambda b, h: (0, 0))],
        out_specs=vspec,
        out_shape=jax.ShapeDtypeStruct((B, S, GLA_HEADS * GLA_DVP), BF16),
        scratch_shapes=[pltpu.VMEM((S, GLA_DKP), BF16), pltpu.VMEM((S, GLA_DKP), BF16), pltpu.VMEM((S, GLA_DKP), BF16),
                        pltpu.VMEM((nchunk, 1, GLA_DKP), F32), pltpu.VMEM((S, GLA_DVP), F32),
                        pltpu.VMEM((nchunk, GLA_DVP, GLA_DKP), F32)],
        compiler_params=_params("parallel", "parallel"),
        name="gla",
    )(q, k, v, r, a1, wg, bg, ng)


def _mix_router_kernel(seq_ref, memo_ref, wa_ref, wb_ref, x_ref, g_ref, b_ref, wr_ref, br_ref,
                       x1_ref, comb_ref, *, alpha):
    mixed = _dot(seq_ref[...], wa_ref[...]) + _dot(memo_ref[...], wb_ref[...])
    x1 = _layer_norm(alpha * x_ref[...] + mixed, g_ref[...], b_ref[...])
    x1_ref[...] = x1
    logits = _dot_f32(x1, wr_ref[...]) + br_ref[...]
    lane = lax.broadcasted_iota(I32, logits.shape, 1).astype(F32)
    neg = -jnp.inf
    none = F32(LANES)
    glog = jnp.where(lane < N_GROUPS, logits, neg)
    gmax = jnp.max(glog, axis=1, keepdims=True)
    gsel = jnp.min(jnp.where(glog == gmax, lane, none), axis=1, keepdims=True)
    pg = 1.0 / jnp.sum(jnp.exp(glog - gmax), axis=1, keepdims=True)
    lo = N_GROUPS + EXPERTS_PER_GROUP * gsel
    elog = jnp.where((lane >= lo) & (lane < lo + EXPERTS_PER_GROUP), logits, neg)
    v1 = jnp.max(elog, axis=1, keepdims=True)
    i1 = jnp.min(jnp.where(elog == v1, lane, none), axis=1, keepdims=True)
    elog2 = jnp.where(lane == i1, neg, elog)
    v2 = jnp.max(elog2, axis=1, keepdims=True)
    i2 = jnp.min(jnp.where(elog2 == v2, lane, none), axis=1, keepdims=True)
    e2 = jnp.exp(v2 - v1)
    den = 1.0 + e2
    comb_ref[...] = (jnp.where(lane == i1 - lo, pg / den, 0.0) + jnp.where(lane == i2 - lo, pg * e2 / den, 0.0)
                     + jnp.where(lane == EXPERTS_PER_GROUP, gsel, 0.0))


def _mix_router(seq, memo, wa, wb, x2d, g, b, wr, br, alpha, tm):
    T = x2d.shape[0]
    row = lambda n: pl.BlockSpec((tm, n), lambda i: (i, 0))
    full = lambda a: pl.BlockSpec(a.shape, lambda i: (0,) * a.ndim)
    kern = functools.partial(_mix_router_kernel, alpha=alpha)
    return pl.pallas_call(
        kern,
        grid=(T // tm,),
        in_specs=[row(seq.shape[1]), row(MEM_WIDTH), full(wa), full(wb), row(D_MODEL), full(g), full(b),
                  full(wr), full(br)],
        out_specs=[row(D_MODEL), row(LANES)],
        out_shape=[jax.ShapeDtypeStruct((T, D_MODEL), F32), jax.ShapeDtypeStruct((T, LANES), F32)],
        compiler_params=_params("parallel"),
        name="outproj_ln_router",
    )(seq, memo, wa, wb, x2d, g, b, wr, br)


MOE_ALIGN = 16
MOE_BIG = 64
MOE_TM = 1024
MOE_TR = 1024
MOE_XW = D_MODEL + LANES


def _ceil_to(v, m):
    return (v + (m - 1)) // m * m


def _for_row_chunks(rows, make_copy, fn):
    nbig = rows // MOE_BIG

    def big(c, carry):
        fn(make_copy(pl.multiple_of(c * MOE_BIG, MOE_ALIGN), MOE_BIG))
        return carry

    def small(c, carry):
        fn(make_copy(pl.multiple_of(nbig * MOE_BIG + c * MOE_ALIGN, MOE_ALIGN), MOE_ALIGN))
        return carry

    lax.fori_loop(0, nbig, big, 0)
    lax.fori_loop(0, (rows - nbig * MOE_BIG) // MOE_ALIGN, small, 0)


def _moe_sort_kernel(x_ref, route_ref, xs_ref, meta_ref, bounds_ref, loc_ref, zero_ref, tri_ref, off_ref, seg_ref, sem,
                     *, tm, rloc, ntiles, cap):
    p = pl.program_id(0)
    i = pl.program_id(1)
    rt = route_ref[...].T
    gid = rt[EXPERTS_PER_GROUP:EXPERTS_PER_GROUP + 1, :]
    grp = lax.broadcasted_iota(I32, (8, 1), 0).astype(F32)
    onehot = (gid == grp).astype(F32)
    cnt = jnp.sum(onehot, axis=1, keepdims=True)
    npad = jnp.floor((cnt + (MOE_ALIGN - 1)) * (1.0 / MOE_ALIGN)) * MOE_ALIGN
    for k in range(N_GROUPS):
        seg_ref[N_GROUPS + k] = jnp.max(npad[k:k + 1, :]).astype(I32)

    @pl.when((p == 0) & (i == 0))
    def _():
        for k in range(N_GROUPS):
            off_ref[k] = 0
        zero_ref[...] = jnp.zeros(zero_ref.shape, BF16)
        tri_ref[...] = (lax.broadcasted_iota(I32, (tm, tm), 0) < lax.broadcasted_iota(I32, (tm, tm), 1)).astype(BF16)

    @pl.when(p == 0)
    def _():
        for k in range(N_GROUPS):
            off_ref[k] = off_ref[k] + seg_ref[N_GROUPS + k]

    @pl.when((p == 1) & (i == 0))
    def _():
        base = 0
        for k in range(N_GROUPS):
            total = off_ref[k]
            bounds_ref[k] = base
            off_ref[k] = base
            base = base + _ceil_to(total, MOE_TR)

    def fill_zero(first, last):
        def copy(off, size):
            return pltpu.make_async_copy(zero_ref.at[pl.ds(0, size), :],
                                         xs_ref.at[pl.ds(pl.multiple_of(first + off, MOE_ALIGN), size), :], sem)
        _for_row_chunks(last - first, copy, lambda cp: cp.start())
        _for_row_chunks(last - first, copy, lambda cp: cp.wait())

    @pl.when(p == 1)
    def _():
        rank = _dot(onehot.astype(BF16), tri_ref[...])
        starts = [jnp.zeros((1, 1), F32)]
        for k in range(1, N_GROUPS):
            starts.append(starts[-1] + npad[k - 1:k, :])
        for k in range(N_GROUPS):
            seg_ref[k] = jnp.max(starts[k]).astype(I32)
        start = jnp.concatenate(starts + [jnp.zeros((8 - N_GROUPS, 1), F32)], axis=0)
        dest = jnp.sum(onehot * (start + rank), axis=0, keepdims=True).astype(I32)
        perm = (lax.broadcasted_iota(I32, (rloc, tm), 0) == dest).astype(BF16)
        loc_ref[:, :D_MODEL] = _dot(perm, x_ref[...].astype(BF16)).astype(BF16)
        w = rt[0:EXPERTS_PER_GROUP, :]
        w_hi, w_lo = _split_bf16(w)
        w_lo2 = (w - w_hi.astype(F32) - w_lo.astype(F32)).astype(BF16)
        w_terms = jnp.concatenate([w_hi, w_lo, w_lo2, jnp.zeros((LANES - 3 * EXPERTS_PER_GROUP, tm), BF16)], axis=0)
        loc_ref[:, D_MODEL:] = _dot_nt(perm, w_terms).astype(BF16)

        def for_chunks(fn):
            for k in range(N_GROUPS):
                def copy(off, size, k=k):
                    src = loc_ref.at[pl.ds(pl.multiple_of(seg_ref[k] + off, MOE_ALIGN), size), :]
                    dst = xs_ref.at[pl.ds(pl.multiple_of(off_ref[k] + off, MOE_ALIGN), size), :]
                    return pltpu.make_async_copy(src, dst, sem)
                _for_row_chunks(seg_ref[N_GROUPS + k], copy, fn)

        for_chunks(lambda cp: cp.start())
        for_chunks(lambda cp: cp.wait())
        for k in range(N_GROUPS):
            meta_ref[i * 2 * N_GROUPS + k] = off_ref[k]
            meta_ref[i * 2 * N_GROUPS + N_GROUPS + k] = seg_ref[N_GROUPS + k]
            off_ref[k] = off_ref[k] + seg_ref[N_GROUPS + k]

    @pl.when((p == 1) & (i == ntiles - 1))
    def _():
        for k in range(N_GROUPS):
            bounds_ref[N_GROUPS + k] = off_ref[k]
            fill_zero(off_ref[k], bounds_ref[k + 1] if k + 1 < N_GROUPS else cap)


def _moe_sort(x1, route, tm):
    T = x1.shape[0]
    ntiles = T // tm
    rloc = _ceil_to(tm + N_GROUPS * MOE_ALIGN, LANES)
    cap = _ceil_to(T + MOE_ALIGN * N_GROUPS * ntiles, MOE_TR) + N_GROUPS * MOE_TR
    kern = functools.partial(_moe_sort_kernel, tm=tm, rloc=rloc, ntiles=ntiles, cap=cap)
    smem = pl.BlockSpec(memory_space=pltpu.SMEM)
    return pl.pallas_call(
        kern,
        grid=(2, ntiles),
        in_specs=[pl.BlockSpec((tm, D_MODEL), lambda p, i: (i * p, 0)),
                  pl.BlockSpec((tm, LANES), lambda p, i: (i, 0))],
        out_specs=[pl.BlockSpec(memory_space=pl.ANY), smem, smem],
        out_shape=[jax.ShapeDtypeStruct((cap, MOE_XW), BF16),
                   jax.ShapeDtypeStruct((ntiles * 2 * N_GROUPS,), I32),
                   jax.ShapeDtypeStruct((2 * N_GROUPS,), I32)],
        scratch_shapes=[pltpu.VMEM((rloc, MOE_XW), BF16), pltpu.VMEM((MOE_BIG, MOE_XW), BF16),
                        pltpu.VMEM((tm, tm), BF16),
                        pltpu.SMEM((N_GROUPS,), I32), pltpu.SMEM((2 * N_GROUPS,), I32), pltpu.SemaphoreType.DMA],
        compiler_params=_params("arbitrary", "arbitrary"),
        name="moe_sort_dispatch",
    )(x1, route)


def _moe_tile_group(r, bounds_ref):
    row = r * MOE_TR
    g = 0
    for k in range(1, N_GROUPS):
        g = g + (row >= bounds_ref[k]).astype(I32)
    return g


def _moe_mlp_kernel(bounds_ref, xs_ref, w13_ref, w2_ref, ys_ref, acc_ref):
    r = pl.program_id(0)
    g = _moe_tile_group(r, bounds_ref)
    end = bounds_ref[N_GROUPS]
    for k in range(1, N_GROUPS):
        end = jnp.where(g == k, bounds_ref[N_GROUPS + k], end)
    used = r * MOE_TR < end

    @pl.when(used)
    def _():
        xb = xs_ref[:, :D_MODEL]
        terms = xs_ref[:, D_MODEL:].astype(F32)
        cw = terms + pltpu.roll(terms, LANES - EXPERTS_PER_GROUP, 1) + pltpu.roll(terms, LANES - 2 * EXPERTS_PER_GROUP, 1)
        for e in range(EXPERTS_PER_GROUP):
            h = _dot(xb, w13_ref[0, :, e * 2 * EXPERT_FF:(e + 1) * 2 * EXPERT_FF])
            a = h[:, :EXPERT_FF]
            u = h[:, EXPERT_FF:]
            act = ((a * jax.nn.sigmoid(a)) * u * cw[:, e:e + 1]).astype(BF16)
            y = _dot(act, w2_ref[0, e * EXPERT_FF:(e + 1) * EXPERT_FF, :])
            if e == 0:
                acc_ref[...] = y
            else:
                acc_ref[...] += y
        ys_ref[...] = acc_ref[...].astype(BF16)

    @pl.when(jnp.logical_not(used))
    def _():
        ys_ref[...] = jnp.zeros(ys_ref.shape, BF16)


def _moe_mlp(bounds, xs, w13, w2):
    cap = xs.shape[0]
    wmap = lambda r, b: (_moe_tile_group(r, b), 0, 0)
    grid_spec = pltpu.PrefetchScalarGridSpec(
        num_scalar_prefetch=1,
        grid=(cap // MOE_TR,),
        in_specs=[pl.BlockSpec((MOE_TR, MOE_XW), lambda r, b: (r, 0)),
                  pl.BlockSpec((1, D_MODEL, EXPERTS_PER_GROUP * 2 * EXPERT_FF), wmap),
                  pl.BlockSpec((1, EXPERTS_PER_GROUP * EXPERT_FF, D_MODEL), wmap)],
        out_specs=pl.BlockSpec((MOE_TR, D_MODEL), lambda r, b: (r, 0)),
        scratch_shapes=[pltpu.VMEM((MOE_TR, D_MODEL), F32)])
    return pl.pallas_call(
        _moe_mlp_kernel,
        grid_spec=grid_spec,
        out_shape=jax.ShapeDtypeStruct((cap, D_MODEL), BF16),
        compiler_params=_params("arbitrary"),
        name="moe_group_experts",
    )(bounds, xs, w13, w2)


def _moe_combine_kernel(meta_ref, x_ref, route_ref, ys_ref, g_ref, b_ref, o_ref, loc_ref, tri_ref, sem,
                        *, alpha, tm, rloc, ntiles):
    i = pl.program_id(0)

    def segments(t):
        offs = [meta_ref[t * 2 * N_GROUPS + k] for k in range(N_GROUPS)]
        rows = [meta_ref[t * 2 * N_GROUPS + N_GROUPS + k] for k in range(N_GROUPS)]
        segs = [0]
        for k in range(N_GROUPS):
            segs.append(segs[-1] + rows[k])
        return offs, rows, segs

    def for_chunks(t, fn):
        offs, rows, segs = segments(t)
        slot = t % 2
        for k in range(N_GROUPS):
            def copy(off, size, k=k):
                src = ys_ref.at[pl.ds(pl.multiple_of(offs[k] + off, MOE_ALIGN), size), :]
                dst = loc_ref.at[slot, pl.ds(pl.multiple_of(segs[k] + off, MOE_ALIGN), size), :]
                return pltpu.make_async_copy(src, dst, sem.at[slot])
            _for_row_chunks(rows[k], copy, fn)

    @pl.when(i == 0)
    def _():
        for_chunks(i, lambda cp: cp.start())

    @pl.when(i + 1 < ntiles)
    def _():
        for_chunks(i + 1, lambda cp: cp.start())

    segs = segments(i)[2]
    route = route_ref[...]
    lane = lax.broadcasted_iota(I32, (1, LANES), 1)
    onehot = ((route[:, EXPERTS_PER_GROUP:EXPERTS_PER_GROUP + 1] == lane.astype(F32)) & (lane < N_GROUPS)).astype(F32)
    @pl.when(i == 0)
    def _():
        tri_ref[...] = (lax.broadcasted_iota(I32, (tm, tm), 1) < lax.broadcasted_iota(I32, (tm, tm), 0)).astype(BF16)

    rank = _dot(tri_ref[...], onehot.astype(BF16))
    seg_start = jnp.zeros((1, LANES), F32)
    for k in range(N_GROUPS):
        seg_start = seg_start + jnp.where(lane == k, lax.convert_element_type(segs[k], F32), F32(0.0))
    dest = jnp.sum(onehot * (seg_start + rank), axis=1, keepdims=True).astype(I32)
    unperm = (lax.broadcasted_iota(I32, (1, rloc), 1) == dest).astype(BF16)
    for_chunks(i, lambda cp: cp.wait())
    valid = lax.broadcasted_iota(I32, (rloc, 1), 0) < segs[N_GROUPS]
    ysl = jnp.where(valid, loc_ref[i % 2], jnp.zeros((rloc, D_MODEL), BF16))
    o_ref[...] = _layer_norm(alpha * x_ref[...] + _dot(unperm, ysl), g_ref[...], b_ref[...])


def _moe_combine(meta, x1, route, ys, g, b, alpha, tm):
    T = x1.shape[0]
    rloc = _ceil_to(tm + N_GROUPS * MOE_ALIGN, LANES)
    kern = functools.partial(_moe_combine_kernel, alpha=alpha, tm=tm, rloc=rloc, ntiles=T // tm)
    grid_spec = pltpu.PrefetchScalarGridSpec(
        num_scalar_prefetch=1,
        grid=(T // tm,),
        in_specs=[pl.BlockSpec((tm, D_MODEL), lambda i, m: (i, 0)),
                  pl.BlockSpec((tm, LANES), lambda i, m: (i, 0)),
                  pl.BlockSpec(memory_space=pl.ANY),
                  pl.BlockSpec((1, D_MODEL), lambda i, m: (0, 0)),
                  pl.BlockSpec((1, D_MODEL), lambda i, m: (0, 0))],
        out_specs=pl.BlockSpec((tm, D_MODEL), lambda i, m: (i, 0)),
        scratch_shapes=[pltpu.VMEM((2, rloc, D_MODEL), BF16), pltpu.VMEM((tm, tm), BF16),
                        pltpu.SemaphoreType.DMA((2,))])
    return pl.pallas_call(
        kern,
        grid_spec=grid_spec,
        out_shape=jax.ShapeDtypeStruct((T, D_MODEL), F32),
        compiler_params=_params("arbitrary"),
        name="moe_combine_ln",
    )(meta, x1, route, ys, g, b)


def _pad_cols(a, width):
    return jnp.pad(a, ((0, 0), (0, width - a.shape[1])))


def _pad_heads(w, heads, dim, dim_pad):
    rows = w.shape[0]
    return jnp.pad(w.reshape(rows, heads, dim), ((0, 0), (0, 0), (0, dim_pad - dim))).reshape(rows, heads * dim_pad)


def _tile(n, pref):
    t = pref
    while n % t:
        t //= 2
    return t


def kernel(x, mem, positions, dsa_w_in, dsa_idx_k_g, dsa_idx_k_b, gla_w_in, gla_w_gate, gla_b_gate, gla_norm_g,
           w_mem_kv, w_out, ln1_g, ln1_b, ln2_g, ln2_b, moe_w_group, moe_b_group, moe_w_router, moe_b_router,
           moe_w13, moe_w2):
    B, S, D = x.shape
    T = B * S
    depth = w_out.shape[0]
    alpha = (2 * depth) ** 0.25
    tm = _tile(T, 512)
    tq = _tile(S, 512)
    tq_idx = _tile(S, 256)
    topk = min(DSA_MAX_TOPK, S // 4)

    inv = ROPE_THETA ** (-jnp.arange(0, HEAD_DIM, 2, dtype=F32) / HEAD_DIM)
    ang = positions.astype(F32).reshape(T, 1) * inv
    cos, sin = jnp.cos(ang), jnp.sin(ang)
    cosf = jnp.concatenate([cos, cos, cos, cos], axis=1)
    sinf = jnp.concatenate([-sin, sin, -sin, sin], axis=1)

    xc = x.reshape(T, D)
    ia = ib = 0
    for i in range(depth):
        if i % 2 == 0:
            w = dsa_w_in[ia]
            wq, wk, wv, wqi, wki, wwi, wqm = jnp.split(w, [768, 1536, 2304, 2816, 2880, 2888], axis=1)
            w_tok = jnp.concatenate([wk, wqm, _pad_cols(wki, LANES)], axis=1).astype(BF16)
            w_feat = jnp.concatenate([wq, wv, wqi, _pad_cols(wwi, 16)], axis=1).T.astype(BF16)
            lng = _pad_cols(dsa_idx_k_g[ia][None, :], LANES)
            lnb = _pad_cols(dsa_idx_k_b[ia][None, :], LANES)
            k, qm, ki, q_t, v_t, qi_t, wi_t = _proj_dsa(xc, w_tok, w_feat, cosf, sinf, cosf.T, sinf.T, lng, lnb,
                                                         tm, tq, tq_idx)
            bias = _idx_mask(qi_t, ki.reshape(B, S, LANES), wi_t, B, S, topk, tq_idx)
            seq = _dsa_attn(q_t, k.reshape(B, S, SEQ_WIDTH), v_t, bias, B, S, tq, 2).reshape(T, SEQ_WIDTH)
            wa = w_out[i][:SEQ_WIDTH].astype(BF16)
            ia += 1
        else:
            w = gla_w_in[ib]
            wq, wk, wv, wr_, wa1, wqm = jnp.split(w, [384, 768, 1536, 2304, 2320], axis=1)
            w_all = jnp.concatenate([
                _pad_heads(wq, GLA_HEADS, GLA_DK, GLA_DKP), _pad_heads(wk, GLA_HEADS, GLA_DK, GLA_DKP),
                _pad_heads(wv, GLA_HEADS, GLA_DV, GLA_DVP), _pad_heads(wr_, GLA_HEADS, GLA_DV, GLA_DVP),
                wqm, _pad_cols(wa1, LANES)], axis=1).astype(BF16)
            q, k, v, r, qm, a1 = _proj_gla(xc, w_all, tm)
            r3 = lambda a: a.reshape(B, S, a.shape[1])
            wg = _pad_heads(gla_w_gate[ib], GLA_HEADS, GLA_DK, GLA_DKP)
            wg = jnp.pad(wg, ((0, LANES - GLA_GATE_RANK), (0, 0)))
            wg = wg.reshape(LANES, GLA_HEADS, GLA_DKP).transpose(1, 0, 2)
            bg = _pad_heads(gla_b_gate[ib][None, :], GLA_HEADS, GLA_DK, GLA_DKP).reshape(GLA_HEADS, 1, GLA_DKP)
            ng = _pad_cols(gla_norm_g[ib][None, :], GLA_DVP)
            seq = _gla(r3(q), r3(k), r3(v), r3(r), r3(a1), wg, bg, ng).reshape(T, GLA_HEADS * GLA_DVP)
            wa = w_out[i][:SEQ_WIDTH].reshape(GLA_HEADS, GLA_DV, D)
            wa = jnp.pad(wa, ((0, 0), (0, GLA_DVP - GLA_DV), (0, 0))).reshape(GLA_HEADS * GLA_DVP, D).astype(BF16)
            ib += 1
        memo = _mem_attn(qm.reshape(B, S, MEM_WIDTH), mem, w_mem_kv[i].astype(BF16), tq).reshape(T, MEM_WIDTH)
        wb = w_out[i][SEQ_WIDTH:].astype(BF16)
        wr = jnp.concatenate([moe_w_group[i], moe_w_router[i].transpose(1, 0, 2).reshape(D, N_EXPERTS)], axis=1)
        wr = _pad_cols(wr, LANES)
        br = _pad_cols(jnp.concatenate([moe_b_group[i], moe_b_router[i].reshape(-1)])[None, :], LANES)
        x1, route = _mix_router(seq, memo, wa, wb, xc, ln1_g[i][None, :], ln1_b[i][None, :], wr, br, alpha, tm)
        w13g = moe_w13[i].transpose(0, 2, 1, 3).reshape(N_GROUPS, D, EXPERTS_PER_GROUP * 2 * EXPERT_FF).astype(BF16)
        w2g = moe_w2[i].reshape(N_GROUPS, EXPERTS_PER_GROUP * EXPERT_FF, D).astype(BF16)
        tmoe = _tile(T, MOE_TM)
        xs, meta, bounds = _moe_sort(x1, route, tmoe)
        ys = _moe_mlp(bounds, xs, w13g, w2g)
        xc = _moe_combine(meta, x1, route, ys, ln2_g[i][None, :], ln2_b[i][None, :], alpha, tmoe)
    return xc.reshape(B, S, D)
```

```python
import functools
import math

import jax
import jax.numpy as jnp
from jax import lax
from jax.experimental import pallas as pl
from jax.experimental.pallas import tpu as pltpu

F32 = jnp.float32
BF16 = jnp.bfloat16
I32 = jnp.int32
I16 = jnp.int16

LANES = 128
D_MODEL = 1024
HEAD_DIM = 64
N_MEM_HEADS = 4
MEM_WIDTH = N_MEM_HEADS * HEAD_DIM
SEQ_WIDTH = D_MODEL - MEM_WIDTH
ROPE_THETA = 10000.0
DSA_HEADS = SEQ_WIDTH // HEAD_DIM
IDX_HEADS = 8
IDX_DIM = 64
DSA_MAX_TOPK = 256
GLA_HEADS = 4
GLA_DV = SEQ_WIDTH // GLA_HEADS
GLA_DK = GLA_DV // 2
GLA_DKP = 128
GLA_DVP = 256
GLA_GATE_RANK = 16
GLA_TAU = 16.0
GLA_CHUNK = 64
N_GROUPS = 4
EXPERTS_PER_GROUP = 8
N_EXPERTS = N_GROUPS * EXPERTS_PER_GROUP
EXPERT_FF = 256
LN_EPS = 1e-5
RMS_EPS = 1e-6
MASK_BIAS = -1e30
INT_MIN = -2 ** 31
VMEM_LIMIT = 56 * 1024 * 1024


def _dot(a, b):
    return jnp.dot(a, b, preferred_element_type=F32)


def _dot_nt(a, b):
    return lax.dot_general(a, b, (((1,), (1,)), ((), ())), preferred_element_type=F32)


def _dot_tn(a, b):
    return lax.dot_general(a, b, (((0,), (0,)), ((), ())), preferred_element_type=F32)


def _split_bf16(a):
    hi = a.astype(BF16)
    lo = (a - hi.astype(F32)).astype(BF16)
    return hi, lo


def _dot_f32(a, b):
    ah, al = _split_bf16(a)
    bh, bl = _split_bf16(b)
    return _dot(ah, bh) + (_dot(ah, bl) + _dot(al, bh))


def _layer_norm(y, g, b):
    mu = jnp.mean(y, axis=-1, keepdims=True)
    yc = y - mu
    var = jnp.mean(yc * yc, axis=-1, keepdims=True)
    return yc * lax.rsqrt(var + LN_EPS) * g + b


def _params(*sem):
    return pltpu.CompilerParams(dimension_semantics=sem, vmem_limit_bytes=VMEM_LIMIT)


def _rope_fn(cos, sin, axis):
    shape = (1, LANES) if axis == 1 else (LANES, 1)
    pos = lax.broadcasted_iota(I32, shape, axis)
    first_half = (pos % HEAD_DIM) < (HEAD_DIM // 2)

    def rope(a):
        swapped = jnp.where(first_half, pltpu.roll(a, LANES - HEAD_DIM // 2, axis),
                            pltpu.roll(a, HEAD_DIM // 2, axis))
        return a * cos + swapped * sin
    return rope


def _proj_dsa_kernel(x_ref, wa_ref, wb_ref, cos_ref, sin_ref, cost_ref, sint_ref, lng_ref, lnb_ref,
                     k_ref, qm_ref, ki_ref, qt_ref, vt_ref, qit_ref, wit_ref):
    xb = x_ref[...].astype(BF16)
    rope = _rope_fn(cos_ref[...], sin_ref[...], 1)
    rope_t = _rope_fn(cost_ref[...], sint_ref[...], 0)
    tok = _dot(xb, wa_ref[...])
    for c in range(SEQ_WIDTH // LANES):
        k_ref[:, c * LANES:(c + 1) * LANES] = rope(tok[:, c * LANES:(c + 1) * LANES]).astype(BF16)
    qm_ref[...] = tok[:, SEQ_WIDTH:SEQ_WIDTH + MEM_WIDTH].astype(BF16)
    acc = tok[:, D_MODEL:D_MODEL + LANES]
    lane = lax.broadcasted_iota(I32, (1, LANES), 1)
    is_ki = lane < IDX_DIM
    mu = jnp.sum(jnp.where(is_ki, acc, 0.0), axis=1, keepdims=True) * (1.0 / IDX_DIM)
    d = jnp.where(is_ki, acc - mu, 0.0)
    var = jnp.sum(d * d, axis=1, keepdims=True) * (1.0 / IDX_DIM)
    ki_ref[...] = rope(d * lax.rsqrt(var + LN_EPS) * lng_ref[...] + lnb_ref[...]).astype(BF16)

    def store_t(out_ref, r, val):
        nblk, _, tb = out_ref.shape
        rows = val.shape[0]
        for j in range(nblk):
            out_ref[j, r * rows:(r + 1) * rows, :] = val[:, j * tb:(j + 1) * tb].astype(out_ref.dtype)

    q_scale = HEAD_DIM ** -0.5 * math.log2(math.e)
    feat = _dot_nt(wb_ref[...], xb)
    for r in range(SEQ_WIDTH // LANES):
        store_t(qt_ref, r, rope_t(feat[r * LANES:(r + 1) * LANES]) * q_scale)
    off = SEQ_WIDTH
    for r in range(SEQ_WIDTH // LANES):
        store_t(vt_ref, r, feat[off + r * LANES:off + (r + 1) * LANES])
    off = 2 * SEQ_WIDTH
    for r in range(IDX_HEADS * IDX_DIM // LANES):
        store_t(qit_ref, r, rope_t(feat[off + r * LANES:off + (r + 1) * LANES]))
    off = 2 * SEQ_WIDTH + IDX_HEADS * IDX_DIM
    store_t(wit_ref, 0, feat[off:off + IDX_HEADS] * (IDX_HEADS ** -0.5 * IDX_DIM ** -0.5))


def _proj_dsa(x2d, wa, wb, cosf, sinf, cost, sint, lng, lnb, tm, tq_att, tq_idx):
    T = x2d.shape[0]
    row = lambda n: pl.BlockSpec((tm, n), lambda i: (i, 0))
    col = pl.BlockSpec((LANES, tm), lambda i: (0, i))
    full = lambda a: pl.BlockSpec(a.shape, lambda i: (0,) * a.ndim)
    featmaj = lambda n, tb: pl.BlockSpec((tm // tb, n, tb), lambda i: (i, 0, 0))
    fshape = lambda n, tb, dt: jax.ShapeDtypeStruct((T // tb, n, tb), dt)
    return pl.pallas_call(
        _proj_dsa_kernel,
        grid=(T // tm,),
        in_specs=[row(D_MODEL), full(wa), full(wb), row(LANES), row(LANES), col, col, full(lng), full(lnb)],
        out_specs=[row(SEQ_WIDTH), row(MEM_WIDTH), row(LANES),
                   featmaj(SEQ_WIDTH, tq_att), featmaj(SEQ_WIDTH, tq_att),
                   featmaj(IDX_HEADS * IDX_DIM, tq_idx), featmaj(IDX_HEADS, tq_idx)],
        out_shape=[jax.ShapeDtypeStruct((T, SEQ_WIDTH), BF16), jax.ShapeDtypeStruct((T, MEM_WIDTH), BF16),
                   jax.ShapeDtypeStruct((T, LANES), BF16),
                   fshape(SEQ_WIDTH, tq_att, BF16), fshape(SEQ_WIDTH, tq_att, BF16),
                   fshape(IDX_HEADS * IDX_DIM, tq_idx, BF16), fshape(IDX_HEADS, tq_idx, F32)],
        compiler_params=_params("parallel"),
        name="proj_dsa",
    )(x2d, wa, wb, cosf, sinf, cost, sint, lng, lnb)


def _proj_gla_kernel(x_ref, w_ref, q_ref, k_ref, v_ref, r_ref, qm_ref, a_ref):
    xb = x_ref[...].astype(BF16)
    segments = ((0, 512, q_ref), (512, 512, k_ref), (1024, 1024, v_ref), (2048, 1024, r_ref),
                (3072, 256, qm_ref), (3328, 128, a_ref))
    for off, width, out_ref in segments:
        out_ref[...] = _dot(xb, w_ref[:, off:off + width]).astype(out_ref.dtype)


def _proj_gla(x2d, w, tm):
    T = x2d.shape[0]
    row = lambda n: pl.BlockSpec((tm, n), lambda i: (i, 0))
    outs = [(512, BF16), (512, BF16), (1024, BF16), (1024, BF16), (256, BF16), (LANES, F32)]
    return pl.pallas_call(
        _proj_gla_kernel,
        grid=(T // tm,),
        in_specs=[row(D_MODEL), pl.BlockSpec(w.shape, lambda i: (0, 0))],
        out_specs=[row(n) for n, _ in outs],
        out_shape=[jax.ShapeDtypeStruct((T, n), dt) for n, dt in outs],
        compiler_params=_params("parallel"),
        name="proj_gla",
    )(x2d, w)


def _idx_kernel(qi_ref, ki_ref, wi_ref, bias_ref, keys_ref, hi_ref, lo_ref, cut_ref, *, tq, tk, nkb, topk, seq):
    qt = pl.program_id(1)
    n_act = qt + 1
    key_l = lax.broadcasted_iota(I32, (tk, tq), 0)
    qry_g = qt * tq + lax.broadcasted_iota(I32, (tk, tq), 1)
    wi = wi_ref[0]

    def score_body(kb, carry):
        kblk = ki_ref[0, pl.ds(pl.multiple_of(kb * tk, tk), tk), :][:, :IDX_DIM]
        sc = jnp.zeros((tk, tq), F32)
        for h in range(IDX_HEADS):
            qh = qi_ref[0, h * IDX_DIM:(h + 1) * IDX_DIM, :]
            sc = sc + jnp.maximum(_dot(kblk, qh), 0.0) * wi[h:h + 1, :]
        bits = lax.bitcast_convert_type(sc, I32)
        key = bits ^ ((bits >> 31) & 0x7FFFFFFF)
        key = jnp.where(kb * tk + key_l > qry_g, INT_MIN, key)
        keys_ref[kb] = key
        hi_ref[kb] = (key >> 16).astype(I16)
        lo_ref[kb] = (key ^ 0x8000).astype(I16)
        return carry

    lax.fori_loop(0, n_act, score_body, 0)

    def count16(ref, cand, strict=False):
        cand16 = cand.astype(I16)

        def body(kb, acc):
            v = ref[kb]
            m = jnp.where((v > cand16) if strict else (v >= cand16), jnp.int16(1), jnp.int16(0))
            parts = [m[j * 16:(j + 1) * 16] for j in range(4)]
            for j in range(4, tk // 16):
                parts[j % 4] = parts[j % 4] + m[j * 16:(j + 1) * 16]
            return acc + ((parts[0] + parts[1]) + (parts[2] + parts[3]))
        acc = lax.fori_loop(0, n_act, body, jnp.zeros((16, tq), I16))
        return jnp.sum(acc.astype(I32).astype(F32), axis=0, keepdims=True)

    def search16(ref, need):
        def bit_body(i, base_u):
            cand_u = base_u | lax.shift_left(jnp.int32(1), lax.convert_element_type(15 - i, I32))
            cnt = count16(ref, cand_u - 32768)
            return jnp.where(cnt >= need, cand_u, base_u)
        return lax.fori_loop(0, 16, bit_body, jnp.zeros((1, tq), I32)) - 32768

    base_hi = search16(hi_ref, topk)
    above = count16(hi_ref, base_hi, strict=True)
    base_hi16 = base_hi.astype(I16)

    def bucket_body(kb, carry):
        lo_ref[kb] = jnp.where(hi_ref[kb] == base_hi16, lo_ref[kb], jnp.int16(-32768))
        return carry

    lax.fori_loop(0, n_act, bucket_body, 0)
    base_lo = search16(lo_ref, topk - above)
    base = lax.shift_left(base_hi, jnp.int32(16)) | (base_lo + 32768)

    def count(pred):
        def body(kb, acc):
            m = pred(keys_ref[kb], kb).astype(F32)
            parts = [m[j * 8:(j + 1) * 8] for j in range(4)]
            for j in range(4, tk // 8):
                parts[j % 4] = parts[j % 4] + m[j * 8:(j + 1) * 8]
            return acc + ((parts[0] + parts[1]) + (parts[2] + parts[3]))
        acc = lax.fori_loop(0, n_act, body, jnp.zeros((8, tq), F32))
        return jnp.sum(acc, axis=0, keepdims=True)

    cnt_gt = count(lambda kk, kb: kk > base)
    cnt_ge = count(lambda kk, kb: kk >= base)
    need = topk - cnt_gt
    tie = (cnt_ge > topk) & (base != INT_MIN)
    cut_ref[...] = jnp.full((1, tq), seq, I32)

    @pl.when(jnp.max(tie.astype(F32)) > 0.0)
    def _():
        nbits = int(math.log2(seq))

        def idx_body(i, m):
            cand = m | lax.shift_left(jnp.int32(1), lax.convert_element_type(nbits - 1 - i, I32))
            below = count(lambda kk, kb: (kk == base) & (kb * tk + key_l < cand))
            return jnp.where(below < need, cand, m)

        m = lax.fori_loop(0, nbits, idx_body, jnp.zeros((1, tq), I32))
        cut_ref[...] = jnp.where(tie, m, seq)

    cut = cut_ref[...]
    for kb in range(nkb):
        @pl.when(kb <= qt)
        def _():
            kk = keys_ref[kb]
            sel = ((kk > base) | ((kk == base) & (kb * tk + key_l <= cut))) & (kk != INT_MIN)
            bias_ref[0, 0, kb] = jnp.where(sel, F32(0.0), F32(MASK_BIAS)).astype(BF16)

        @pl.when(kb > qt)
        def _():
            bias_ref[0, 0, kb] = jnp.full((tk, tq), MASK_BIAS, BF16)


def _idx_mask(qit, ki, wit, B, S, topk, tq):
    nq = S // tq
    kern = functools.partial(_idx_kernel, tq=tq, tk=tq, nkb=nq, topk=topk, seq=S)
    return pl.pallas_call(
        kern,
        grid=(B, nq),
        in_specs=[pl.BlockSpec((1, IDX_HEADS * IDX_DIM, tq), lambda b, q: (b * nq + q, 0, 0)),
                  pl.BlockSpec((1, S, LANES), lambda b, q: (b, 0, 0)),
                  pl.BlockSpec((1, IDX_HEADS, tq), lambda b, q: (b * nq + q, 0, 0))],
        out_specs=pl.BlockSpec((1, 1, nq, tq, tq), lambda b, q: (b, q, 0, 0, 0)),
        out_shape=jax.ShapeDtypeStruct((B, nq, nq, tq, tq), BF16),
        scratch_shapes=[pltpu.VMEM((nq, tq, tq), I32), pltpu.VMEM((nq, tq, tq), I16), pltpu.VMEM((nq, tq, tq), I16),
                        pltpu.VMEM((1, tq), I32)],
        compiler_params=_params("parallel", "arbitrary"),
        name="dsa_index_select",
    )(qit, ki, wit)


def _dsa_attn_kernel(q_ref, k_ref, v_ref, bias_ref, o_ref, m_ref, acc_ref, *, tq, tk, sub):
    qt = pl.program_id(2)
    nsub = tq // sub
    npair = q_ref.shape[1] // LANES
    nh = 2 * npair
    low = lax.broadcasted_iota(I32, (LANES, 1), 0) < HEAD_DIM
    q_heads = []
    for pr in range(npair):
        q = q_ref[0, pr * LANES:(pr + 1) * LANES, :]
        zero = jnp.zeros_like(q)
        q_heads += [jnp.where(low, q, zero), jnp.where(low, zero, q)]
    m_ref[...] = jnp.full(m_ref.shape, -jnp.inf, F32)
    acc_ref[...] = jnp.zeros(acc_ref.shape, F32)

    def body(kb, carry):
        kblk = k_ref[0, pl.ds(pl.multiple_of(kb * tk, tk), tk), :]
        v_heads = []
        for pr in range(npair):
            vt = v_ref[kb, pr * LANES:(pr + 1) * LANES, :]
            one = jnp.ones_like(vt)
            v_heads += [jnp.where(low, vt, one), jnp.where(low, one, vt)]
        bias = jnp.concatenate(
            [jnp.concatenate([bias_ref[0, c, kb * nsub + a] for c in range(nsub)], axis=1) for a in range(nsub)],
            axis=0).astype(F32)
        s = [_dot(kblk[:, (h // 2) * LANES:(h // 2 + 1) * LANES], q_heads[h]) + bias for h in range(nh)]
        m_old = [m_ref[h] for h in range(nh)]
        m_new = [jnp.maximum(m_old[h], jnp.max(s[h], axis=0, keepdims=True)) for h in range(nh)]
        p = [jnp.exp2(s[h] - m_new[h]).astype(BF16) for h in range(nh)]
        pv = [_dot(v_heads[h], p[h]) for h in range(nh)]
        for h in range(nh):
            acc_ref[h] = jnp.exp2(m_old[h] - m_new[h]) * acc_ref[h] + pv[h]
            m_ref[h] = m_new[h]
        return carry

    lax.fori_loop(0, qt + 1, body, 0)
    for pr in range(npair):
        a0 = acc_ref[2 * pr]
        a1 = acc_ref[2 * pr + 1]
        o = jnp.where(low, a0 / a0[HEAD_DIM:HEAD_DIM + 1, :], a1 / a1[0:1, :])
        o_ref[0, :, pr * LANES:(pr + 1) * LANES] = o.T.astype(o_ref.dtype)


def _dsa_attn(qt, k, vt, bias, B, S, tq, npair):
    nq = S // tq
    width = npair * LANES
    nsb, sub = bias.shape[2], bias.shape[3]
    kern = functools.partial(_dsa_attn_kernel, tq=tq, tk=tq, sub=sub)
    return pl.pallas_call(
        kern,
        grid=(B, SEQ_WIDTH // width, nq),
        in_specs=[pl.BlockSpec((1, width, tq), lambda b, h, i: (b * nq + i, h, 0)),
                  pl.BlockSpec((1, S, width), lambda b, h, i: (b, 0, h)),
                  pl.BlockSpec((nq, width, tq), lambda b, h, i: (b, h, 0)),
                  pl.BlockSpec((1, tq // sub, nsb, sub, sub), lambda b, h, i: (b, i, 0, 0, 0))],
        out_specs=pl.BlockSpec((1, tq, width), lambda b, h, i: (b, i, h)),
        out_shape=jax.ShapeDtypeStruct((B, S, SEQ_WIDTH), BF16),
        scratch_shapes=[pltpu.VMEM((2 * npair, 1, tq), F32), pltpu.VMEM((2 * npair, LANES, tq), F32)],
        compiler_params=_params("parallel", "parallel", "arbitrary"),
        name="dsa_attention",
    )(qt, k, vt, bias)


def _mem_attn_kernel(qm_ref, mem_ref, wkv_ref, o_ref, k_scr, v_scr):
    @pl.when(pl.program_id(1) == 0)
    def _():
        kv = _dot(mem_ref[0].astype(BF16), wkv_ref[...])
        k_scr[...] = kv[:, :MEM_WIDTH].astype(BF16)
        v_scr[...] = kv[:, MEM_WIDTH:].astype(BF16)

    lane = lax.broadcasted_iota(I32, (1, LANES), 1)
    low = lane < HEAD_DIM
    scale = HEAD_DIM ** -0.5
    for pair in range(MEM_WIDTH // LANES):
        cols = slice(pair * LANES, (pair + 1) * LANES)
        q = qm_ref[0, :, cols]
        zero = jnp.zeros_like(q)
        kp = k_scr[:, cols]
        vp = v_scr[:, cols]
        outs = []
        for qh in (jnp.where(low, q, zero), jnp.where(low, zero, q)):
            s = _dot_nt(qh, kp) * scale
            s = s - jnp.max(s, axis=1, keepdims=True)
            p = jnp.exp(s)
            p = p / jnp.sum(p, axis=1, keepdims=True)
            outs.append(_dot(p.astype(BF16), vp))
        o_ref[0, :, cols] = jnp.where(low, outs[0], outs[1]).astype(o_ref.dtype)


def _mem_attn(qm, mem, wkv, tq):
    B, S, _ = qm.shape
    M = mem.shape[1]
    return pl.pallas_call(
        _mem_attn_kernel,
        grid=(B, S // tq),
        in_specs=[pl.BlockSpec((1, tq, MEM_WIDTH), lambda b, i: (b, i, 0)),
                  pl.BlockSpec((1, M, D_MODEL), lambda b, i: (b, 0, 0)),
                  pl.BlockSpec(wkv.shape, lambda b, i: (0, 0))],
        out_specs=pl.BlockSpec((1, tq, MEM_WIDTH), lambda b, i: (b, i, 0)),
        out_shape=jax.ShapeDtypeStruct((B, S, MEM_WIDTH), BF16),
        scratch_shapes=[pltpu.VMEM((M, MEM_WIDTH), BF16), pltpu.VMEM((M, MEM_WIDTH), BF16)],
        compiler_params=_params("parallel", "arbitrary"),
        name="memory_attention",
    )(qm, mem, wkv)


def _gla_kernel(q_ref, k_ref, v_ref, r_ref, a_ref, wg_ref, bg_ref, ng_ref, o_ref,
                qb_ref, kb_ref, kd_ref, dl_ref, oi_ref, kv_ref, *, nchunk):
    C = GLA_CHUNK
    S = nchunk * C
    z = _dot_f32(a_ref[0], wg_ref[0]) + bg_ref[0]
    b = (jnp.minimum(z, 0.0) - jnp.log1p(jnp.exp(-jnp.abs(z)))) * (1.0 / GLA_TAU)
    pos = lax.broadcasted_iota(I32, (S, 1), 0) % C
    shift = 1
    while shift < C:
        b = b + jnp.where(pos >= shift, pltpu.roll(b, shift, 0), 0.0)
        shift *= 2
    b3 = b.reshape(nchunk, C, GLA_DKP)
    b_last = b3[:, C - 1:C, :]
    q = q_ref[0].astype(F32) * (GLA_DK ** -0.5)
    k = k_ref[0].astype(F32)
    qb_ref[...] = (q * jnp.exp(b)).astype(BF16)
    kb_ref[...] = (k * jnp.exp(-b)).astype(BF16)
    kd_ref[...] = (k.reshape(nchunk, C, GLA_DKP) * jnp.exp(b_last - b3)).reshape(S, GLA_DKP).astype(BF16)
    dl_ref[...] = jnp.exp(b_last)
    causal = lax.broadcasted_iota(I32, (C, C), 0) >= lax.broadcasted_iota(I32, (C, C), 1)

    def intra(c, carry):
        rows = pl.ds(pl.multiple_of(c * C, C), C)
        v = v_ref[0, rows, :]
        attn = jnp.where(causal, _dot_nt(qb_ref[rows, :], kb_ref[rows, :]), 0.0)
        oi_ref[rows, :] = _dot(attn.astype(BF16), v)
        kv_ref[c] = _dot_tn(v, kd_ref[rows, :])
        return carry

    lax.fori_loop(0, nchunk, intra, 0, unroll=8)
    ng = ng_ref[...]

    def inter(c, st):
        rows = pl.ds(pl.multiple_of(c * C, C), C)
        o = oi_ref[rows, :] + _dot_nt(qb_ref[rows, :], st.astype(BF16))
        ms = jnp.sum(o * o, axis=1, keepdims=True) * (1.0 / GLA_DV)
        o = o * lax.rsqrt(ms + RMS_EPS) * ng
        r = r_ref[0, rows, :].astype(F32)
        o_ref[0, rows, :] = (o * (r * jax.nn.sigmoid(r))).astype(o_ref.dtype)
        return st * dl_ref[c] + kv_ref[c]

    lax.fori_loop(0, nchunk, inter, jnp.zeros((GLA_DVP, GLA_DKP), F32), unroll=4)


def _gla(q, k, v, r, a1, wg, bg, ng):
    B, S, _ = q.shape
    nchunk = S // GLA_CHUNK
    kern = functools.partial(_gla_kernel, nchunk=nchunk)
    kspec = pl.BlockSpec((1, S, GLA_DKP), lambda b, h: (b, 0, h))
    vspec = pl.BlockSpec((1, S, GLA_DVP), lambda b, h: (b, 0, h))
    return pl.pallas_call(
        kern,
        grid=(B, GLA_HEADS),
        in_specs=[kspec, kspec, vspec, vspec,
                  pl.BlockSpec((1, S, LANES), lambda b, h: (b, 0, 0)),
                  pl.BlockSpec((1, LANES, GLA_DKP), lambda b, h: (h, 0, 0)),
                  pl.BlockSpec((1, 1, GLA_DKP), lambda b, h: (h, 0, 0)),
                  pl.BlockSpec((1, GLA_DVP), lambda b, h: (0, 0))],
        out_specs=vspec,
        out_shape=jax.ShapeDtypeStruct((B, S, GLA_HEADS * GLA_DVP), BF16),
        scratch_shapes=[pltpu.VMEM((S, GLA_DKP), BF16), pltpu.VMEM((S, GLA_DKP), BF16), pltpu.VMEM((S, GLA_DKP), BF16),
                        pltpu.VMEM((nchunk, 1, GLA_DKP), F32), pltpu.VMEM((S, GLA_DVP), F32),
                        pltpu.VMEM((nchunk, GLA_DVP, GLA_DKP), F32)],
        compiler_params=_params("parallel", "parallel"),
        name="gla",
    )(q, k, v, r, a1, wg, bg, ng)


def _mix_router_kernel(seq_ref, memo_ref, wa_ref, wb_ref, x_ref, g_ref, b_ref, wr_ref, br_ref,
                       x1_ref, comb_ref, *, alpha):
    mixed = _dot(seq_ref[...], wa_ref[...]) + _dot(memo_ref[...], wb_ref[...])
    x1 = _layer_norm(alpha * x_ref[...] + mixed, g_ref[...], b_ref[...])
    x1_ref[...] = x1
    logits = _dot_f32(x1, wr_ref[...]) + br_ref[...]
    lane = lax.broadcasted_iota(I32, logits.shape, 1).astype(F32)
    neg = -jnp.inf
    none = F32(LANES)
    glog = jnp.where(lane < N_GROUPS, logits, neg)
    gmax = jnp.max(glog, axis=1, keepdims=True)
    gsel = jnp.min(jnp.where(glog == gmax, lane, none), axis=1, keepdims=True)
    pg = 1.0 / jnp.sum(jnp.exp(glog - gmax), axis=1, keepdims=True)
    lo = N_GROUPS + EXPERTS_PER_GROUP * gsel
    elog = jnp.where((lane >= lo) & (lane < lo + EXPERTS_PER_GROUP), logits, neg)
    v1 = jnp.max(elog, axis=1, keepdims=True)
    i1 = jnp.min(jnp.where(elog == v1, lane, none), axis=1, keepdims=True)
    elog2 = jnp.where(lane == i1, neg, elog)
    v2 = jnp.max(elog2, axis=1, keepdims=True)
    i2 = jnp.min(jnp.where(elog2 == v2, lane, none), axis=1, keepdims=True)
    e2 = jnp.exp(v2 - v1)
    den = 1.0 + e2
    comb_ref[...] = (jnp.where(lane == i1 - lo, pg / den, 0.0) + jnp.where(lane == i2 - lo, pg * e2 / den, 0.0)
                     + jnp.where(lane == EXPERTS_PER_GROUP, gsel, 0.0))


def _mix_router(seq, memo, wa, wb, x2d, g, b, wr, br, alpha, tm):
    T = x2d.shape[0]
    row = lambda n: pl.BlockSpec((tm, n), lambda i: (i, 0))
    full = lambda a: pl.BlockSpec(a.shape, lambda i: (0,) * a.ndim)
    kern = functools.partial(_mix_router_kernel, alpha=alpha)
    return pl.pallas_call(
        kern,
        grid=(T // tm,),
        in_specs=[row(seq.shape[1]), row(MEM_WIDTH), full(wa), full(wb), row(D_MODEL), full(g), full(b),
                  full(wr), full(br)],
        out_specs=[row(D_MODEL), row(LANES)],
        out_shape=[jax.ShapeDtypeStruct((T, D_MODEL), F32), jax.ShapeDtypeStruct((T, LANES), F32)],
        compiler_params=_params("parallel"),
        name="outproj_ln_router",
    )(seq, memo, wa, wb, x2d, g, b, wr, br)


MOE_ALIGN = 16
MOE_BIG = 64
MOE_TM = 1024
MOE_TR = 1024
MOE_XW = D_MODEL + LANES


def _ceil_to(v, m):
    return (v + (m - 1)) // m * m


def _for_row_chunks(rows, make_copy, fn):
    nbig = rows // MOE_BIG

    def big(c, carry):
        fn(make_copy(pl.multiple_of(c * MOE_BIG, MOE_ALIGN), MOE_BIG))
        return carry

    def small(c, carry):
        fn(make_copy(pl.multiple_of(nbig * MOE_BIG + c * MOE_ALIGN, MOE_ALIGN), MOE_ALIGN))
        return carry

    lax.fori_loop(0, nbig, big, 0)
    lax.fori_loop(0, (rows - nbig * MOE_BIG) // MOE_ALIGN, small, 0)


def _moe_sort_kernel(x_ref, route_ref, xs_ref, meta_ref, bounds_ref,
                     loc_ref, zero_ref, tri_ref, off_ref, seg_ref, prev_ref, sem,
                     *, tm, rloc, ntiles, cap):
    p = pl.program_id(0)
    i = pl.program_id(1)
    rt = route_ref[...].T
    gid = rt[EXPERTS_PER_GROUP:EXPERTS_PER_GROUP + 1, :]
    grp = lax.broadcasted_iota(I32, (8, 1), 0).astype(F32)
    onehot = (gid == grp).astype(F32)
    cnt = jnp.sum(onehot, axis=1, keepdims=True)
    npad = jnp.floor((cnt + (MOE_ALIGN - 1)) * (1.0 / MOE_ALIGN)) * MOE_ALIGN
    for k in range(N_GROUPS):
        seg_ref[N_GROUPS + k] = jnp.max(npad[k:k + 1, :]).astype(I32)

    @pl.when((p == 0) & (i == 0))
    def _():
        for k in range(N_GROUPS):
            off_ref[k] = 0
        zero_ref[...] = jnp.zeros(zero_ref.shape, BF16)
        tri_ref[...] = (lax.broadcasted_iota(I32, (tm, tm), 0) < lax.broadcasted_iota(I32, (tm, tm), 1)).astype(BF16)

    @pl.when(p == 0)
    def _():
        for k in range(N_GROUPS):
            off_ref[k] = off_ref[k] + seg_ref[N_GROUPS + k]

    @pl.when((p == 1) & (i == 0))
    def _():
        base = 0
        for k in range(N_GROUPS):
            total = off_ref[k]
            bounds_ref[k] = base
            off_ref[k] = base
            base = base + _ceil_to(total, MOE_TR)

    def wait_tile(slot):
        for k in range(N_GROUPS):
            def copy(off, size):
                return pltpu.make_async_copy(loc_ref.at[slot, pl.ds(off, size), :], xs_ref.at[pl.ds(off, size), :],
                                             sem.at[slot])
            _for_row_chunks(prev_ref[k], copy, lambda cp: cp.wait())

    def fill_zero(first, last):
        def copy(off, size):
            return pltpu.make_async_copy(zero_ref.at[pl.ds(0, size), :],
                                         xs_ref.at[pl.ds(pl.multiple_of(first + off, MOE_ALIGN), size), :], sem.at[0])
        _for_row_chunks(last - first, copy, lambda cp: cp.start())
        _for_row_chunks(last - first, copy, lambda cp: cp.wait())

    @pl.when(p == 1)
    def _():
        rank = _dot(onehot.astype(BF16), tri_ref[...])
        starts = [jnp.zeros((1, 1), F32)]
        for k in range(1, N_GROUPS):
            starts.append(starts[-1] + npad[k - 1:k, :])
        for k in range(N_GROUPS):
            seg_ref[k] = jnp.max(starts[k]).astype(I32)
        start = jnp.concatenate(starts + [jnp.zeros((8 - N_GROUPS, 1), F32)], axis=0)
        dest = jnp.sum(onehot * (start + rank), axis=0, keepdims=True).astype(I32)
        perm = (lax.broadcasted_iota(I32, (rloc, tm), 0) == dest).astype(BF16)
        slot = i % 2
        loc_ref[slot, :, :D_MODEL] = _dot(perm, x_ref[...].astype(BF16)).astype(BF16)
        w = rt[0:EXPERTS_PER_GROUP, :]
        w_hi, w_lo = _split_bf16(w)
        w_lo2 = (w - w_hi.astype(F32) - w_lo.astype(F32)).astype(BF16)
        w_terms = jnp.concatenate([w_hi, w_lo, w_lo2, jnp.zeros((LANES - 3 * EXPERTS_PER_GROUP, tm), BF16)], axis=0)
        loc_ref[slot, :, D_MODEL:] = _dot_nt(perm, w_terms).astype(BF16)

        for k in range(N_GROUPS):
            def copy(off, size, k=k):
                src = loc_ref.at[slot, pl.ds(pl.multiple_of(seg_ref[k] + off, MOE_ALIGN), size), :]
                dst = xs_ref.at[pl.ds(pl.multiple_of(off_ref[k] + off, MOE_ALIGN), size), :]
                return pltpu.make_async_copy(src, dst, sem.at[slot])
            _for_row_chunks(seg_ref[N_GROUPS + k], copy, lambda cp: cp.start())

        @pl.when(i > 0)
        def _():
            wait_tile(1 - slot)
        for k in range(N_GROUPS):
            meta_ref[i * 2 * N_GROUPS + k] = off_ref[k]
            meta_ref[i * 2 * N_GROUPS + N_GROUPS + k] = seg_ref[N_GROUPS + k]
            off_ref[k] = off_ref[k] + seg_ref[N_GROUPS + k]
            prev_ref[k] = seg_ref[N_GROUPS + k]

    @pl.when((p == 1) & (i == ntiles - 1))
    def _():
        wait_tile((ntiles - 1) % 2)
        for k in range(N_GROUPS):
            bounds_ref[N_GROUPS + k] = off_ref[k]
            fill_zero(off_ref[k], bounds_ref[k + 1] if k + 1 < N_GROUPS else cap)


def _moe_sort(x1, route, tm):
    T = x1.shape[0]
    ntiles = T // tm
    rloc = _ceil_to(tm + N_GROUPS * MOE_ALIGN, LANES)
    cap = _ceil_to(T + MOE_ALIGN * N_GROUPS * ntiles, MOE_TR) + N_GROUPS * MOE_TR
    kern = functools.partial(_moe_sort_kernel, tm=tm, rloc=rloc, ntiles=ntiles, cap=cap)
    smem = pl.BlockSpec(memory_space=pltpu.SMEM)
    return pl.pallas_call(
        kern,
        grid=(2, ntiles),
        in_specs=[pl.BlockSpec((tm, D_MODEL), lambda p, i: (i * p, 0)),
                  pl.BlockSpec((tm, LANES), lambda p, i: (i, 0))],
        out_specs=[pl.BlockSpec(memory_space=pl.ANY), smem, smem],
        out_shape=[jax.ShapeDtypeStruct((cap, MOE_XW), BF16),
                   jax.ShapeDtypeStruct((ntiles * 2 * N_GROUPS,), I32),
                   jax.ShapeDtypeStruct((2 * N_GROUPS,), I32)],
        scratch_shapes=[pltpu.VMEM((2, rloc, MOE_XW), BF16), pltpu.VMEM((MOE_BIG, MOE_XW), BF16),
                        pltpu.VMEM((tm, tm), BF16),
                        pltpu.SMEM((N_GROUPS,), I32), pltpu.SMEM((2 * N_GROUPS,), I32), pltpu.SMEM((N_GROUPS,), I32),
                        pltpu.SemaphoreType.DMA((2,))],
        compiler_params=_params("arbitrary", "arbitrary"),
        name="moe_sort_dispatch",
    )(x1, route)


def _moe_tile_group(r, bounds_ref):
    row = r * MOE_TR
    g = 0
    for k in range(1, N_GROUPS):
        g = g + (row >= bounds_ref[k]).astype(I32)
    return g


def _moe_mlp_kernel(bounds_ref, xs_ref, w13_ref, w2_ref, ys_ref, acc_ref):
    r = pl.program_id(0)
    g = _moe_tile_group(r, bounds_ref)
    end = bounds_ref[N_GROUPS]
    for k in range(1, N_GROUPS):
        end = jnp.where(g == k, bounds_ref[N_GROUPS + k], end)
    used = r * MOE_TR < end

    @pl.when(used)
    def _():
        xb = xs_ref[:, :D_MODEL]
        terms = xs_ref[:, D_MODEL:].astype(F32)
        cw = terms + pltpu.roll(terms, LANES - EXPERTS_PER_GROUP, 1) + pltpu.roll(terms, LANES - 2 * EXPERTS_PER_GROUP, 1)
        for e in range(EXPERTS_PER_GROUP):
            h = _dot(xb, w13_ref[0, e])
            a = h[:, :EXPERT_FF]
            u = h[:, EXPERT_FF:]
            act = ((a * jax.nn.sigmoid(a)) * u * cw[:, e:e + 1]).astype(BF16)
            y = _dot(act, w2_ref[0, e * EXPERT_FF:(e + 1) * EXPERT_FF, :])
            if e == 0:
                acc_ref[...] = y
            else:
                acc_ref[...] += y
        ys_ref[...] = acc_ref[...].astype(BF16)

    @pl.when(jnp.logical_not(used))
    def _():
        ys_ref[...] = jnp.zeros(ys_ref.shape, BF16)


def _moe_mlp(bounds, xs, w13, w2):
    cap = xs.shape[0]
    grid_spec = pltpu.PrefetchScalarGridSpec(
        num_scalar_prefetch=1,
        grid=(cap // MOE_TR,),
        in_specs=[pl.BlockSpec((MOE_TR, MOE_XW), lambda r, b: (r, 0)),
                  pl.BlockSpec((1, EXPERTS_PER_GROUP, D_MODEL, 2 * EXPERT_FF),
                               lambda r, b: (_moe_tile_group(r, b), 0, 0, 0)),
                  pl.BlockSpec((1, EXPERTS_PER_GROUP * EXPERT_FF, D_MODEL),
                               lambda r, b: (_moe_tile_group(r, b), 0, 0))],
        out_specs=pl.BlockSpec((MOE_TR, D_MODEL), lambda r, b: (r, 0)),
        scratch_shapes=[pltpu.VMEM((MOE_TR, D_MODEL), F32)])
    return pl.pallas_call(
        _moe_mlp_kernel,
        grid_spec=grid_spec,
        out_shape=jax.ShapeDtypeStruct((cap, D_MODEL), BF16),
        compiler_params=_params("arbitrary"),
        name="moe_group_experts",
    )(bounds, xs, w13, w2)


def _moe_combine_kernel(meta_ref, x_ref, route_ref, ys_ref, g_ref, b_ref, o_ref, loc_ref, tri_ref, sem,
                        *, alpha, tm, rloc, ntiles):
    i = pl.program_id(0)

    def segments(t):
        offs = [meta_ref[t * 2 * N_GROUPS + k] for k in range(N_GROUPS)]
        rows = [meta_ref[t * 2 * N_GROUPS + N_GROUPS + k] for k in range(N_GROUPS)]
        segs = [0]
        for k in range(N_GROUPS):
            segs.append(segs[-1] + rows[k])
        return offs, rows, segs

    def for_chunks(t, fn):
        offs, rows, segs = segments(t)
        slot = t % 2
        for k in range(N_GROUPS):
            def copy(off, size, k=k):
                src = ys_ref.at[pl.ds(pl.multiple_of(offs[k] + off, MOE_ALIGN), size), :]
                dst = loc_ref.at[slot, pl.ds(pl.multiple_of(segs[k] + off, MOE_ALIGN), size), :]
                return pltpu.make_async_copy(src, dst, sem.at[slot])
            _for_row_chunks(rows[k], copy, fn)

    @pl.when(i == 0)
    def _():
        for_chunks(i, lambda cp: cp.start())

    @pl.when(i + 1 < ntiles)
    def _():
        for_chunks(i + 1, lambda cp: cp.start())

    segs = segments(i)[2]
    route = route_ref[...]
    lane = lax.broadcasted_iota(I32, (1, LANES), 1)
    onehot = ((route[:, EXPERTS_PER_GROUP:EXPERTS_PER_GROUP + 1] == lane.astype(F32)) & (lane < N_GROUPS)).astype(F32)
    @pl.when(i == 0)
    def _():
        tri_ref[...] = (lax.broadcasted_iota(I32, (tm, tm), 1) < lax.broadcasted_iota(I32, (tm, tm), 0)).astype(BF16)

    rank = _dot(tri_ref[...], onehot.astype(BF16))
    seg_start = jnp.zeros((1, LANES), F32)
    for k in range(N_GROUPS):
        seg_start = seg_start + jnp.where(lane == k, lax.convert_element_type(segs[k], F32), F32(0.0))
    dest = jnp.sum(onehot * (seg_start + rank), axis=1, keepdims=True).astype(I32)
    unperm = (lax.broadcasted_iota(I32, (1, rloc), 1) == dest).astype(BF16)
    for_chunks(i, lambda cp: cp.wait())
    valid = lax.broadcasted_iota(I32, (rloc, 1), 0) < segs[N_GROUPS]
    ysl = jnp.where(valid, loc_ref[i % 2], jnp.zeros((rloc, D_MODEL), BF16))
    o_ref[...] = _layer_norm(alpha * x_ref[...] + _dot(unperm, ysl), g_ref[...], b_ref[...])


def _moe_combine(meta, x1, route, ys, g, b, alpha, tm):
    T = x1.shape[0]
    rloc = _ceil_to(tm + N_GROUPS * MOE_ALIGN, LANES)
    kern = functools.partial(_moe_combine_kernel, alpha=alpha, tm=tm, rloc=rloc, ntiles=T // tm)
    grid_spec = pltpu.PrefetchScalarGridSpec(
        num_scalar_prefetch=1,
        grid=(T // tm,),
        in_specs=[pl.BlockSpec((tm, D_MODEL), lambda i, m: (i, 0)),
                  pl.BlockSpec((tm, LANES), lambda i, m: (i, 0)),
                  pl.BlockSpec(memory_space=pl.ANY),
                  pl.BlockSpec((1, D_MODEL), lambda i, m: (0, 0)),
                  pl.BlockSpec((1, D_MODEL), lambda i, m: (0, 0))],
        out_specs=pl.BlockSpec((tm, D_MODEL), lambda i, m: (i, 0)),
        scratch_shapes=[pltpu.VMEM((2, rloc, D_MODEL), BF16), pltpu.VMEM((tm, tm), BF16),
                        pltpu.SemaphoreType.DMA((2,))])
    return pl.pallas_call(
        kern,
        grid_spec=grid_spec,
        out_shape=jax.ShapeDtypeStruct((T, D_MODEL), F32),
        compiler_params=_params("arbitrary"),
        name="moe_combine_ln",
    )(meta, x1, route, ys, g, b)


def _pad_cols(a, width):
    return jnp.pad(a, ((0, 0), (0, width - a.shape[1])))


def _pad_heads(w, heads, dim, dim_pad):
    rows = w.shape[0]
    return jnp.pad(w.reshape(rows, heads, dim), ((0, 0), (0, 0), (0, dim_pad - dim))).reshape(rows, heads * dim_pad)


def _tile(n, pref):
    t = pref
    while n % t:
        t //= 2
    return t


def kernel(x, mem, positions, dsa_w_in, dsa_idx_k_g, dsa_idx_k_b, gla_w_in, gla_w_gate, gla_b_gate, gla_norm_g,
           w_mem_kv, w_out, ln1_g, ln1_b, ln2_g, ln2_b, moe_w_group, moe_b_group, moe_w_router, moe_b_router,
           moe_w13, moe_w2):
    B, S, D = x.shape
    T = B * S
    depth = w_out.shape[0]
    alpha = (2 * depth) ** 0.25
    tm = _tile(T, 512)
    tq = _tile(S, 512)
    tq_idx = _tile(S, 256)
    topk = min(DSA_MAX_TOPK, S // 4)

    inv = ROPE_THETA ** (-jnp.arange(0, HEAD_DIM, 2, dtype=F32) / HEAD_DIM)
    ang = positions.astype(F32).reshape(T, 1) * inv
    cos, sin = jnp.cos(ang), jnp.sin(ang)
    cosf = jnp.concatenate([cos, cos, cos, cos], axis=1)
    sinf = jnp.concatenate([-sin, sin, -sin, sin], axis=1)

    xc = x.reshape(T, D)
    ia = ib = 0
    for i in range(depth):
        if i % 2 == 0:
            w = dsa_w_in[ia]
            wq, wk, wv, wqi, wki, wwi, wqm = jnp.split(w, [768, 1536, 2304, 2816, 2880, 2888], axis=1)
            w_tok = jnp.concatenate([wk, wqm, _pad_cols(wki, LANES)], axis=1).astype(BF16)
            w_feat = jnp.concatenate([wq, wv, wqi, _pad_cols(wwi, 16)], axis=1).T.astype(BF16)
            lng = _pad_cols(dsa_idx_k_g[ia][None, :], LANES)
            lnb = _pad_cols(dsa_idx_k_b[ia][None, :], LANES)
            k, qm, ki, q_t, v_t, qi_t, wi_t = _proj_dsa(xc, w_tok, w_feat, cosf, sinf, cosf.T, sinf.T, lng, lnb,
                                                         tm, tq, tq_idx)
            bias = _idx_mask(qi_t, ki.reshape(B, S, LANES), wi_t, B, S, topk, tq_idx)
            seq = _dsa_attn(q_t, k.reshape(B, S, SEQ_WIDTH), v_t, bias, B, S, tq, 2).reshape(T, SEQ_WIDTH)
            wa = w_out[i][:SEQ_WIDTH].astype(BF16)
            ia += 1
        else:
            w = gla_w_in[ib]
            wq, wk, wv, wr_, wa1, wqm = jnp.split(w, [384, 768, 1536, 2304, 2320], axis=1)
            w_all = jnp.concatenate([
                _pad_heads(wq, GLA_HEADS, GLA_DK, GLA_DKP), _pad_heads(wk, GLA_HEADS, GLA_DK, GLA_DKP),
                _pad_heads(wv, GLA_HEADS, GLA_DV, GLA_DVP), _pad_heads(wr_, GLA_HEADS, GLA_DV, GLA_DVP),
                wqm, _pad_cols(wa1, LANES)], axis=1).astype(BF16)
            q, k, v, r, qm, a1 = _proj_gla(xc, w_all, tm)
            r3 = lambda a: a.reshape(B, S, a.shape[1])
            wg = _pad_heads(gla_w_gate[ib], GLA_HEADS, GLA_DK, GLA_DKP)
            wg = jnp.pad(wg, ((0, LANES - GLA_GATE_RANK), (0, 0)))
            wg = wg.reshape(LANES, GLA_HEADS, GLA_DKP).transpose(1, 0, 2)
            bg = _pad_heads(gla_b_gate[ib][None, :], GLA_HEADS, GLA_DK, GLA_DKP).reshape(GLA_HEADS, 1, GLA_DKP)
            ng = _pad_cols(gla_norm_g[ib][None, :], GLA_DVP)
            seq = _gla(r3(q), r3(k), r3(v), r3(r), r3(a1), wg, bg, ng).reshape(T, GLA_HEADS * GLA_DVP)
            wa = w_out[i][:SEQ_WIDTH].reshape(GLA_HEADS, GLA_DV, D)
            wa = jnp.pad(wa, ((0, 0), (0, GLA_DVP - GLA_DV), (0, 0))).reshape(GLA_HEADS * GLA_DVP, D).astype(BF16)
            ib += 1
        memo = _mem_attn(qm.reshape(B, S, MEM_WIDTH), mem, w_mem_kv[i].astype(BF16), tq).reshape(T, MEM_WIDTH)
        wb = w_out[i][SEQ_WIDTH:].astype(BF16)
        wr = jnp.concatenate([moe_w_group[i], moe_w_router[i].transpose(1, 0, 2).reshape(D, N_EXPERTS)], axis=1)
        wr = _pad_cols(wr, LANES)
        br = _pad_cols(jnp.concatenate([moe_b_group[i], moe_b_router[i].reshape(-1)])[None, :], LANES)
        x1, route = _mix_router(seq, memo, wa, wb, xc, ln1_g[i][None, :], ln1_b[i][None, :], wr, br, alpha,
                                _tile(T, 1024))
        w13g = moe_w13[i].astype(BF16)
        w2g = moe_w2[i].reshape(N_GROUPS, EXPERTS_PER_GROUP * EXPERT_FF, D).astype(BF16)
        tmoe = _tile(T, MOE_TM)
        xs, meta, bounds = _moe_sort(x1, route, tmoe)
        ys = _moe_mlp(bounds, xs, w13g, w2g)
        xc = _moe_combine(meta, x1, route, ys, ln2_g[i][None, :], ln2_b[i][None, :], alpha, tmoe)
    return xc.reshape(B, S, D)
```

```python
import functools
import math

import jax
import jax.numpy as jnp
from jax import lax
from jax.experimental import pallas as pl
from jax.experimental.pallas import tpu as pltpu

F32 = jnp.float32
BF16 = jnp.bfloat16
I32 = jnp.int32
I16 = jnp.int16

LANES = 128
D_MODEL = 1024
HEAD_DIM = 64
N_MEM_HEADS = 4
MEM_WIDTH = N_MEM_HEADS * HEAD_DIM
SEQ_WIDTH = D_MODEL - MEM_WIDTH
ROPE_THETA = 10000.0
DSA_HEADS = SEQ_WIDTH // HEAD_DIM
IDX_HEADS = 8
IDX_DIM = 64
DSA_MAX_TOPK = 256
GLA_HEADS = 4
GLA_DV = SEQ_WIDTH // GLA_HEADS
GLA_DK = GLA_DV // 2
GLA_DKP = 128
GLA_DVP = 256
GLA_GATE_RANK = 16
GLA_TAU = 16.0
GLA_CHUNK = 64
N_GROUPS = 4
EXPERTS_PER_GROUP = 8
N_EXPERTS = N_GROUPS * EXPERTS_PER_GROUP
EXPERT_FF = 256
LN_EPS = 1e-5
RMS_EPS = 1e-6
MASK_BIAS = -1e30
INT_MIN = -2 ** 31
VMEM_LIMIT = 56 * 1024 * 1024


def _dot(a, b):
    return jnp.dot(a, b, preferred_element_type=F32)


def _dot_nt(a, b):
    return lax.dot_general(a, b, (((1,), (1,)), ((), ())), preferred_element_type=F32)


def _dot_tn(a, b):
    return lax.dot_general(a, b, (((0,), (0,)), ((), ())), preferred_element_type=F32)


def _split_bf16(a):
    hi = a.astype(BF16)
    lo = (a - hi.astype(F32)).astype(BF16)
    return hi, lo


def _dot_f32(a, b):
    ah, al = _split_bf16(a)
    bh, bl = _split_bf16(b)
    return _dot(ah, bh) + (_dot(ah, bl) + _dot(al, bh))


def _layer_norm(y, g, b):
    mu = jnp.mean(y, axis=-1, keepdims=True)
    yc = y - mu
    var = jnp.mean(yc * yc, axis=-1, keepdims=True)
    return yc * lax.rsqrt(var + LN_EPS) * g + b


def _params(*sem):
    return pltpu.CompilerParams(dimension_semantics=sem, vmem_limit_bytes=VMEM_LIMIT)


def _rope_fn(cos, sin, axis):
    shape = (1, LANES) if axis == 1 else (LANES, 1)
    pos = lax.broadcasted_iota(I32, shape, axis)
    first_half = (pos % HEAD_DIM) < (HEAD_DIM // 2)

    def rope(a):
        swapped = jnp.where(first_half, pltpu.roll(a, LANES - HEAD_DIM // 2, axis),
                            pltpu.roll(a, HEAD_DIM // 2, axis))
        return a * cos + swapped * sin
    return rope


def _proj_dsa_kernel(x_ref, wa_ref, wb_ref, cos_ref, sin_ref, cost_ref, sint_ref, lng_ref, lnb_ref,
                     k_ref, qm_ref, ki_ref, qt_ref, vt_ref, qit_ref, wit_ref):
    xb = x_ref[...].astype(BF16)
    rope = _rope_fn(cos_ref[...], sin_ref[...], 1)
    rope_t = _rope_fn(cost_ref[...], sint_ref[...], 0)
    tok = _dot(xb, wa_ref[...])
    for c in range(SEQ_WIDTH // LANES):
        k_ref[:, c * LANES:(c + 1) * LANES] = rope(tok[:, c * LANES:(c + 1) * LANES]).astype(BF16)
    qm_ref[...] = tok[:, SEQ_WIDTH:SEQ_WIDTH + MEM_WIDTH].astype(BF16)
    acc = tok[:, D_MODEL:D_MODEL + LANES]
    lane = lax.broadcasted_iota(I32, (1, LANES), 1)
    is_ki = lane < IDX_DIM
    mu = jnp.sum(jnp.where(is_ki, acc, 0.0), axis=1, keepdims=True) * (1.0 / IDX_DIM)
    d = jnp.where(is_ki, acc - mu, 0.0)
    var = jnp.sum(d * d, axis=1, keepdims=True) * (1.0 / IDX_DIM)
    ki_ref[...] = rope(d * lax.rsqrt(var + LN_EPS) * lng_ref[...] + lnb_ref[...]).astype(BF16)

    def store_t(out_ref, r, val):
        nblk, _, tb = out_ref.shape
        rows = val.shape[0]
        for j in range(nblk):
            out_ref[j, r * rows:(r + 1) * rows, :] = val[:, j * tb:(j + 1) * tb].astype(out_ref.dtype)

    q_scale = HEAD_DIM ** -0.5 * math.log2(math.e)
    feat = _dot_nt(wb_ref[...], xb)
    for r in range(SEQ_WIDTH // LANES):
        store_t(qt_ref, r, rope_t(feat[r * LANES:(r + 1) * LANES]) * q_scale)
    off = SEQ_WIDTH
    for r in range(SEQ_WIDTH // LANES):
        store_t(vt_ref, r, feat[off + r * LANES:off + (r + 1) * LANES])
    off = 2 * SEQ_WIDTH
    for r in range(IDX_HEADS * IDX_DIM // LANES):
        store_t(qit_ref, r, rope_t(feat[off + r * LANES:off + (r + 1) * LANES]))
    off = 2 * SEQ_WIDTH + IDX_HEADS * IDX_DIM
    store_t(wit_ref, 0, feat[off:off + IDX_HEADS] * (IDX_HEADS ** -0.5 * IDX_DIM ** -0.5))


def _proj_dsa(x2d, wa, wb, cosf, sinf, cost, sint, lng, lnb, tm, tq_att, tq_idx):
    T = x2d.shape[0]
    row = lambda n: pl.BlockSpec((tm, n), lambda i: (i, 0))
    col = pl.BlockSpec((LANES, tm), lambda i: (0, i))
    full = lambda a: pl.BlockSpec(a.shape, lambda i: (0,) * a.ndim)
    featmaj = lambda n, tb: pl.BlockSpec((tm // tb, n, tb), lambda i: (i, 0, 0))
    fshape = lambda n, tb, dt: jax.ShapeDtypeStruct((T // tb, n, tb), dt)
    return pl.pallas_call(
        _proj_dsa_kernel,
        grid=(T // tm,),
        in_specs=[row(D_MODEL), full(wa), full(wb), row(LANES), row(LANES), col, col, full(lng), full(lnb)],
        out_specs=[row(SEQ_WIDTH), row(MEM_WIDTH), row(LANES),
                   featmaj(SEQ_WIDTH, tq_att), featmaj(SEQ_WIDTH, tq_att),
                   featmaj(IDX_HEADS * IDX_DIM, tq_idx), featmaj(IDX_HEADS, tq_idx)],
        out_shape=[jax.ShapeDtypeStruct((T, SEQ_WIDTH), BF16), jax.ShapeDtypeStruct((T, MEM_WIDTH), BF16),
                   jax.ShapeDtypeStruct((T, LANES), BF16),
                   fshape(SEQ_WIDTH, tq_att, BF16), fshape(SEQ_WIDTH, tq_att, BF16),
                   fshape(IDX_HEADS * IDX_DIM, tq_idx, BF16), fshape(IDX_HEADS, tq_idx, F32)],
        compiler_params=_params("parallel"),
        name="proj_dsa",
    )(x2d, wa, wb, cosf, sinf, cost, sint, lng, lnb)


def _proj_gla_kernel(x_ref, w_ref, q_ref, k_ref, v_ref, r_ref, qm_ref, a_ref):
    xb = x_ref[...].astype(BF16)
    segments = ((0, 512, q_ref), (512, 512, k_ref), (1024, 1024, v_ref), (2048, 1024, r_ref),
                (3072, 256, qm_ref), (3328, 128, a_ref))
    for off, width, out_ref in segments:
        out_ref[...] = _dot(xb, w_ref[:, off:off + width]).astype(out_ref.dtype)


def _proj_gla(x2d, w, tm):
    T = x2d.shape[0]
    row = lambda n: pl.BlockSpec((tm, n), lambda i: (i, 0))
    outs = [(512, BF16), (512, BF16), (1024, BF16), (1024, BF16), (256, BF16), (LANES, F32)]
    return pl.pallas_call(
        _proj_gla_kernel,
        grid=(T // tm,),
        in_specs=[row(D_MODEL), pl.BlockSpec(w.shape, lambda i: (0, 0))],
        out_specs=[row(n) for n, _ in outs],
        out_shape=[jax.ShapeDtypeStruct((T, n), dt) for n, dt in outs],
        compiler_params=_params("parallel"),
        name="proj_gla",
    )(x2d, w)


def _idx_kernel(qi_ref, ki_ref, wi_ref, bias_ref, keys_ref, hi_ref, lo_ref, cut_ref, *, tq, tk, nkb, topk, seq):
    qt = pl.program_id(1)
    n_act = qt + 1
    key_l = lax.broadcasted_iota(I32, (tk, tq), 0)
    qry_g = qt * tq + lax.broadcasted_iota(I32, (tk, tq), 1)
    wi = wi_ref[0]

    def score_body(kb, carry):
        kblk = ki_ref[0, pl.ds(pl.multiple_of(kb * tk, tk), tk), :][:, :IDX_DIM]
        sc = jnp.zeros((tk, tq), F32)
        for h in range(IDX_HEADS):
            qh = qi_ref[0, h * IDX_DIM:(h + 1) * IDX_DIM, :]
            sc = sc + jnp.maximum(_dot(kblk, qh), 0.0) * wi[h:h + 1, :]
        bits = lax.bitcast_convert_type(sc, I32)
        key = bits ^ ((bits >> 31) & 0x7FFFFFFF)
        key = jnp.where(kb * tk + key_l > qry_g, INT_MIN, key)
        keys_ref[kb] = key
        hi_ref[kb] = (key >> 16).astype(I16)
        lo_ref[kb] = (key ^ 0x8000).astype(I16)
        return carry

    lax.fori_loop(0, n_act, score_body, 0)

    def count16(ref, cand, strict=False):
        cand16 = cand.astype(I16)

        def body(kb, acc):
            v = ref[kb]
            m = jnp.where((v > cand16) if strict else (v >= cand16), jnp.int16(1), jnp.int16(0))
            parts = [m[j * 16:(j + 1) * 16] for j in range(4)]
            for j in range(4, tk // 16):
                parts[j % 4] = parts[j % 4] + m[j * 16:(j + 1) * 16]
            return acc + ((parts[0] + parts[1]) + (parts[2] + parts[3]))
        acc = lax.fori_loop(0, n_act, body, jnp.zeros((16, tq), I16))
        return jnp.sum(acc.astype(I32).astype(F32), axis=0, keepdims=True)

    def search16(ref, need):
        def bit_body(i, base_u):
            cand_u = base_u | lax.shift_left(jnp.int32(1), lax.convert_element_type(15 - i, I32))
            cnt = count16(ref, cand_u - 32768)
            return jnp.where(cnt >= need, cand_u, base_u)
        return lax.fori_loop(0, 16, bit_body, jnp.zeros((1, tq), I32)) - 32768

    base_hi = search16(hi_ref, topk)
    above = count16(hi_ref, base_hi, strict=True)
    base_hi16 = base_hi.astype(I16)

    def bucket_body(kb, carry):
        lo_ref[kb] = jnp.where(hi_ref[kb] == base_hi16, lo_ref[kb], jnp.int16(-32768))
        return carry

    lax.fori_loop(0, n_act, bucket_body, 0)
    base_lo = search16(lo_ref, topk - above)
    base = lax.shift_left(base_hi, jnp.int32(16)) | (base_lo + 32768)

    def count(pred):
        def body(kb, acc):
            m = pred(keys_ref[kb], kb).astype(F32)
            parts = [m[j * 8:(j + 1) * 8] for j in range(4)]
            for j in range(4, tk // 8):
                parts[j % 4] = parts[j % 4] + m[j * 8:(j + 1) * 8]
            return acc + ((parts[0] + parts[1]) + (parts[2] + parts[3]))
        acc = lax.fori_loop(0, n_act, body, jnp.zeros((8, tq), F32))
        return jnp.sum(acc, axis=0, keepdims=True)

    cnt_gt = count(lambda kk, kb: kk > base)
    cnt_ge = count(lambda kk, kb: kk >= base)
    need = topk - cnt_gt
    tie = (cnt_ge > topk) & (base != INT_MIN)
    cut_ref[...] = jnp.full((1, tq), seq, I32)

    @pl.when(jnp.max(tie.astype(F32)) > 0.0)
    def _():
        nbits = int(math.log2(seq))

        def idx_body(i, m):
            cand = m | lax.shift_left(jnp.int32(1), lax.convert_element_type(nbits - 1 - i, I32))
            below = count(lambda kk, kb: (kk == base) & (kb * tk + key_l < cand))
            return jnp.where(below < need, cand, m)

        m = lax.fori_loop(0, nbits, idx_body, jnp.zeros((1, tq), I32))
        cut_ref[...] = jnp.where(tie, m, seq)

    cut = cut_ref[...]
    for kb in range(nkb):
        @pl.when(kb <= qt)
        def _():
            kk = keys_ref[kb]
            sel = ((kk > base) | ((kk == base) & (kb * tk + key_l <= cut))) & (kk != INT_MIN)
            bias_ref[0, 0, kb] = jnp.where(sel, F32(0.0), F32(MASK_BIAS)).astype(BF16)

        @pl.when(kb > qt)
        def _():
            bias_ref[0, 0, kb] = jnp.full((tk, tq), MASK_BIAS, BF16)


def _idx_mask(qit, ki, wit, B, S, topk, tq):
    nq = S // tq
    kern = functools.partial(_idx_kernel, tq=tq, tk=tq, nkb=nq, topk=topk, seq=S)
    return pl.pallas_call(
        kern,
        grid=(B, nq),
        in_specs=[pl.BlockSpec((1, IDX_HEADS * IDX_DIM, tq), lambda b, q: (b * nq + q, 0, 0)),
                  pl.BlockSpec((1, S, LANES), lambda b, q: (b, 0, 0)),
                  pl.BlockSpec((1, IDX_HEADS, tq), lambda b, q: (b * nq + q, 0, 0))],
        out_specs=pl.BlockSpec((1, 1, nq, tq, tq), lambda b, q: (b, q, 0, 0, 0)),
        out_shape=jax.ShapeDtypeStruct((B, nq, nq, tq, tq), BF16),
        scratch_shapes=[pltpu.VMEM((nq, tq, tq), I32), pltpu.VMEM((nq, tq, tq), I16), pltpu.VMEM((nq, tq, tq), I16),
                        pltpu.VMEM((1, tq), I32)],
        compiler_params=_params("parallel", "arbitrary"),
        name="dsa_index_select",
    )(qit, ki, wit)


def _dsa_attn_kernel(q_ref, k_ref, v_ref, bias_ref, o_ref, m_ref, acc_ref, *, tq, tk, sub):
    qt = pl.program_id(2)
    nsub = tq // sub
    npair = q_ref.shape[1] // LANES
    nh = 2 * npair
    low = lax.broadcasted_iota(I32, (LANES, 1), 0) < HEAD_DIM
    q_heads = []
    for pr in range(npair):
        q = q_ref[0, pr * LANES:(pr + 1) * LANES, :]
        zero = jnp.zeros_like(q)
        q_heads += [jnp.where(low, q, zero), jnp.where(low, zero, q)]
    m_ref[...] = jnp.full(m_ref.shape, -jnp.inf, F32)
    acc_ref[...] = jnp.zeros(acc_ref.shape, F32)

    def body(kb, carry):
        kblk = k_ref[0, pl.ds(pl.multiple_of(kb * tk, tk), tk), :]
        v_heads = []
        for pr in range(npair):
            vt = v_ref[kb, pr * LANES:(pr + 1) * LANES, :]
            one = jnp.ones_like(vt)
            v_heads += [jnp.where(low, vt, one), jnp.where(low, one, vt)]
        bias = jnp.concatenate(
            [jnp.concatenate([bias_ref[0, c, kb * nsub + a] for c in range(nsub)], axis=1) for a in range(nsub)],
            axis=0).astype(F32)
        s = [_dot(kblk[:, (h // 2) * LANES:(h // 2 + 1) * LANES], q_heads[h]) + bias for h in range(nh)]
        m_old = [m_ref[h] for h in range(nh)]
        m_new = [jnp.maximum(m_old[h], jnp.max(s[h], axis=0, keepdims=True)) for h in range(nh)]
        p = [jnp.exp2(s[h] - m_new[h]).astype(BF16) for h in range(nh)]
        pv = [_dot(v_heads[h], p[h]) for h in range(nh)]
        for h in range(nh):
            acc_ref[h] = jnp.exp2(m_old[h] - m_new[h]) * acc_ref[h] + pv[h]
            m_ref[h] = m_new[h]
        return carry

    lax.fori_loop(0, qt + 1, body, 0)
    for pr in range(npair):
        a0 = acc_ref[2 * pr]
        a1 = acc_ref[2 * pr + 1]
        o = jnp.where(low, a0 / a0[HEAD_DIM:HEAD_DIM + 1, :], a1 / a1[0:1, :])
        o_ref[0, :, pr * LANES:(pr + 1) * LANES] = o.T.astype(o_ref.dtype)


def _dsa_attn(qt, k, vt, bias, B, S, tq, npair):
    nq = S // tq
    width = npair * LANES
    nsb, sub = bias.shape[2], bias.shape[3]
    kern = functools.partial(_dsa_attn_kernel, tq=tq, tk=tq, sub=sub)
    return pl.pallas_call(
        kern,
        grid=(B, SEQ_WIDTH // width, nq),
        in_specs=[pl.BlockSpec((1, width, tq), lambda b, h, i: (b * nq + i, h, 0)),
                  pl.BlockSpec((1, S, width), lambda b, h, i: (b, 0, h)),
                  pl.BlockSpec((nq, width, tq), lambda b, h, i: (b, h, 0)),
                  pl.BlockSpec((1, tq // sub, nsb, sub, sub), lambda b, h, i: (b, i, 0, 0, 0))],
        out_specs=pl.BlockSpec((1, tq, width), lambda b, h, i: (b, i, h)),
        out_shape=jax.ShapeDtypeStruct((B, S, SEQ_WIDTH), BF16),
        scratch_shapes=[pltpu.VMEM((2 * npair, 1, tq), F32), pltpu.VMEM((2 * npair, LANES, tq), F32)],
        compiler_params=_params("parallel", "parallel", "arbitrary"),
        name="dsa_attention",
    )(qt, k, vt, bias)


def _mem_attn_kernel(qm_ref, mem_ref, wkv_ref, o_ref, k_scr, v_scr):
    @pl.when(pl.program_id(1) == 0)
    def _():
        kv = _dot(mem_ref[0].astype(BF16), wkv_ref[...])
        k_scr[...] = kv[:, :MEM_WIDTH].astype(BF16)
        v_scr[...] = kv[:, MEM_WIDTH:].astype(BF16)

    lane = lax.broadcasted_iota(I32, (1, LANES), 1)
    low = lane < HEAD_DIM
    scale = HEAD_DIM ** -0.5
    for pair in range(MEM_WIDTH // LANES):
        cols = slice(pair * LANES, (pair + 1) * LANES)
        q = qm_ref[0, :, cols]
        zero = jnp.zeros_like(q)
        kp = k_scr[:, cols]
        vp = v_scr[:, cols]
        outs = []
        for qh in (jnp.where(low, q, zero), jnp.where(low, zero, q)):
            s = _dot_nt(qh, kp) * scale
            s = s - jnp.max(s, axis=1, keepdims=True)
            p = jnp.exp(s)
            p = p / jnp.sum(p, axis=1, keepdims=True)
            outs.append(_dot(p.astype(BF16), vp))
        o_ref[0, :, cols] = jnp.where(low, outs[0], outs[1]).astype(o_ref.dtype)


def _mem_attn(qm, mem, wkv, tq):
    B, S, _ = qm.shape
    M = mem.shape[1]
    return pl.pallas_call(
        _mem_attn_kernel,
        grid=(B, S // tq),
        in_specs=[pl.BlockSpec((1, tq, MEM_WIDTH), lambda b, i: (b, i, 0)),
                  pl.BlockSpec((1, M, D_MODEL), lambda b, i: (b, 0, 0)),
                  pl.BlockSpec(wkv.shape, lambda b, i: (0, 0))],
        out_specs=pl.BlockSpec((1, tq, MEM_WIDTH), lambda b, i: (b, i, 0)),
        out_shape=jax.ShapeDtypeStruct((B, S, MEM_WIDTH), BF16),
        scratch_shapes=[pltpu.VMEM((M, MEM_WIDTH), BF16), pltpu.VMEM((M, MEM_WIDTH), BF16)],
        compiler_params=_params("parallel", "arbitrary"),
        name="memory_attention",
    )(qm, mem, wkv)


def _gla_kernel(q_ref, k_ref, v_ref, r_ref, a_ref, wg_ref, bg_ref, ng_ref, o_ref,
                qb_ref, kb_ref, kd_ref, dl_ref, oi_ref, kv_ref, *, nchunk):
    C = GLA_CHUNK
    S = nchunk * C
    z = _dot_f32(a_ref[0], wg_ref[0]) + bg_ref[0]
    b = (jnp.minimum(z, 0.0) - jnp.log1p(jnp.exp(-jnp.abs(z)))) * (1.0 / GLA_TAU)
    pos = lax.broadcasted_iota(I32, (S, 1), 0) % C
    shift = 1
    while shift < C:
        b = b + jnp.where(pos >= shift, pltpu.roll(b, shift, 0), 0.0)
        shift *= 2
    b3 = b.reshape(nchunk, C, GLA_DKP)
    b_last = b3[:, C - 1:C, :]
    q = q_ref[0].astype(F32) * (GLA_DK ** -0.5)
    k = k_ref[0].astype(F32)
    qb_ref[...] = (q * jnp.exp(b)).astype(BF16)
    kb_ref[...] = (k * jnp.exp(-b)).astype(BF16)
    kd_ref[...] = (k.reshape(nchunk, C, GLA_DKP) * jnp.exp(b_last - b3)).reshape(S, GLA_DKP).astype(BF16)
    dl_ref[...] = jnp.exp(b_last)
    causal = lax.broadcasted_iota(I32, (C, C), 0) >= lax.broadcasted_iota(I32, (C, C), 1)

    def intra(c, carry):
        rows = pl.ds(pl.multiple_of(c * C, C), C)
        v = v_ref[0, rows, :]
        attn = jnp.where(causal, _dot_nt(qb_ref[rows, :], kb_ref[rows, :]), 0.0)
        oi_ref[rows, :] = _dot(attn.astype(BF16), v)
        kv_ref[c] = _dot_tn(v, kd_ref[rows, :])
        return carry

    lax.fori_loop(0, nchunk, intra, 0, unroll=8)
    ng = ng_ref[...]

    def inter(c, st):
        rows = pl.ds(pl.multiple_of(c * C, C), C)
        o = oi_ref[rows, :] + _dot_nt(qb_ref[rows, :], st.astype(BF16))
        ms = jnp.sum(o * o, axis=1, keepdims=True) * (1.0 / GLA_DV)
        o = o * lax.rsqrt(ms + RMS_EPS) * ng
        r = r_ref[0, rows, :].astype(F32)
        o_ref[0, rows, :] = (o * (r * jax.nn.sigmoid(r))).astype(o_ref.dtype)
        return st * dl_ref[c] + kv_ref[c]

    lax.fori_loop(0, nchunk, inter, jnp.zeros((GLA_DVP, GLA_DKP), F32), unroll=4)


def _gla(q, k, v, r, a1, wg, bg, ng):
    B, S, _ = q.shape
    nchunk = S // GLA_CHUNK
    kern = functools.partial(_gla_kernel, nchunk=nchunk)
    kspec = pl.BlockSpec((1, S, GLA_DKP), lambda b, h: (b, 0, h))
    vspec = pl.BlockSpec((1, S, GLA_DVP), lambda b, h: (b, 0, h))
    return pl.pallas_call(
        kern,
        grid=(B, GLA_HEADS),
        in_specs=[kspec, kspec, vspec, vspec,
                  pl.BlockSpec((1, S, LANES), lambda b, h: (b, 0, 0)),
                  pl.BlockSpec((1, LANES, GLA_DKP), lambda b, h: (h, 0, 0)),
                  pl.BlockSpec((1, 1, GLA_DKP), lambda b, h: (h, 0, 0)),
                  pl.BlockSpec((1, GLA_DVP), lambda b, h: (0, 0))],
        out_specs=vspec,
        out_shape=jax.ShapeDtypeStruct((B, S, GLA_HEADS * GLA_DVP), BF16),
        scratch_shapes=[pltpu.VMEM((S, GLA_DKP), BF16), pltpu.VMEM((S, GLA_DKP), BF16), pltpu.VMEM((S, GLA_DKP), BF16),
                        pltpu.VMEM((nchunk, 1, GLA_DKP), F32), pltpu.VMEM((S, GLA_DVP), F32),
                        pltpu.VMEM((nchunk, GLA_DVP, GLA_DKP), F32)],
        compiler_params=_params("parallel", "parallel"),
        name="gla",
    )(q, k, v, r, a1, wg, bg, ng)


def _mix_router_kernel(seq_ref, memo_ref, wa_ref, wb_ref, x_ref, g_ref, b_ref, wr_ref, br_ref,
                       x1_ref, comb_ref, *, alpha):
    mixed = _dot(seq_ref[...], wa_ref[...]) + _dot(memo_ref[...], wb_ref[...])
    x1 = _layer_norm(alpha * x_ref[...] + mixed, g_ref[...], b_ref[...])
    x1_ref[...] = x1
    logits = _dot_f32(x1, wr_ref[...]) + br_ref[...]
    lane = lax.broadcasted_iota(I32, logits.shape, 1).astype(F32)
    neg = -jnp.inf
    none = F32(LANES)
    glog = jnp.where(lane < N_GROUPS, logits, neg)
    gmax = jnp.max(glog, axis=1, keepdims=True)
    gsel = jnp.min(jnp.where(glog == gmax, lane, none), axis=1, keepdims=True)
    pg = 1.0 / jnp.sum(jnp.exp(glog - gmax), axis=1, keepdims=True)
    lo = N_GROUPS + EXPERTS_PER_GROUP * gsel
    elog = jnp.where((lane >= lo) & (lane < lo + EXPERTS_PER_GROUP), logits, neg)
    v1 = jnp.max(elog, axis=1, keepdims=True)
    i1 = jnp.min(jnp.where(elog == v1, lane, none), axis=1, keepdims=True)
    elog2 = jnp.where(lane == i1, neg, elog)
    v2 = jnp.max(elog2, axis=1, keepdims=True)
    i2 = jnp.min(jnp.where(elog2 == v2, lane, none), axis=1, keepdims=True)
    e2 = jnp.exp(v2 - v1)
    den = 1.0 + e2
    comb_ref[...] = (jnp.where(lane == i1 - lo, pg / den, 0.0) + jnp.where(lane == i2 - lo, pg * e2 / den, 0.0)
                     + jnp.where(lane == EXPERTS_PER_GROUP, gsel, 0.0))


def _mix_router(seq, memo, wa, wb, x2d, g, b, wr, br, alpha, tm):
    T = x2d.shape[0]
    row = lambda n: pl.BlockSpec((tm, n), lambda i: (i, 0))
    full = lambda a: pl.BlockSpec(a.shape, lambda i: (0,) * a.ndim)
    kern = functools.partial(_mix_router_kernel, alpha=alpha)
    return pl.pallas_call(
        kern,
        grid=(T // tm,),
        in_specs=[row(seq.shape[1]), row(MEM_WIDTH), full(wa), full(wb), row(D_MODEL), full(g), full(b),
                  full(wr), full(br)],
        out_specs=[row(D_MODEL), row(LANES)],
        out_shape=[jax.ShapeDtypeStruct((T, D_MODEL), F32), jax.ShapeDtypeStruct((T, LANES), F32)],
        compiler_params=_params("parallel"),
        name="outproj_ln_router",
    )(seq, memo, wa, wb, x2d, g, b, wr, br)


MOE_ALIGN = 16
MOE_BIG = 64
MOE_TM = 1024
MOE_TR = 1024
MOE_XW = D_MODEL + LANES


def _ceil_to(v, m):
    return (v + (m - 1)) // m * m


def _for_row_chunks(rows, make_copy, fn):
    nbig = rows // MOE_BIG

    def big(c, carry):
        fn(make_copy(pl.multiple_of(c * MOE_BIG, MOE_ALIGN), MOE_BIG))
        return carry

    def small(c, carry):
        fn(make_copy(pl.multiple_of(nbig * MOE_BIG + c * MOE_ALIGN, MOE_ALIGN), MOE_ALIGN))
        return carry

    lax.fori_loop(0, nbig, big, 0)
    lax.fori_loop(0, (rows - nbig * MOE_BIG) // MOE_ALIGN, small, 0)


def _moe_sort_kernel(x_ref, route_ref, xs_ref, meta_ref, bounds_ref,
                     loc_ref, zero_ref, tri_ref, off_ref, seg_ref, prev_ref, sem,
                     *, tm, rloc, ntiles, cap):
    p = pl.program_id(0)
    i = pl.program_id(1)
    rt = route_ref[...].T
    gid = rt[EXPERTS_PER_GROUP:EXPERTS_PER_GROUP + 1, :]
    grp = lax.broadcasted_iota(I32, (8, 1), 0).astype(F32)
    onehot = (gid == grp).astype(F32)
    cnt = jnp.sum(onehot, axis=1, keepdims=True)
    npad = jnp.floor((cnt + (MOE_ALIGN - 1)) * (1.0 / MOE_ALIGN)) * MOE_ALIGN
    for k in range(N_GROUPS):
        seg_ref[N_GROUPS + k] = jnp.max(npad[k:k + 1, :]).astype(I32)

    @pl.when((p == 0) & (i == 0))
    def _():
        for k in range(N_GROUPS):
            off_ref[k] = 0
        zero_ref[...] = jnp.zeros(zero_ref.shape, BF16)
        tri_ref[...] = (lax.broadcasted_iota(I32, (tm, tm), 0) < lax.broadcasted_iota(I32, (tm, tm), 1)).astype(BF16)

    @pl.when(p == 0)
    def _():
        for k in range(N_GROUPS):
            off_ref[k] = off_ref[k] + seg_ref[N_GROUPS + k]

    @pl.when((p == 1) & (i == 0))
    def _():
        base = 0
        for k in range(N_GROUPS):
            total = off_ref[k]
            bounds_ref[k] = base
            off_ref[k] = base
            base = base + _ceil_to(total, MOE_TR)

    def wait_tile(slot):
        for k in range(N_GROUPS):
            def copy(off, size):
                return pltpu.make_async_copy(loc_ref.at[slot, pl.ds(off, size), :], xs_ref.at[pl.ds(off, size), :],
                                             sem.at[slot])
            _for_row_chunks(prev_ref[k], copy, lambda cp: cp.wait())

    def fill_zero(first, last):
        def copy(off, size):
            return pltpu.make_async_copy(zero_ref.at[pl.ds(0, size), :],
                                         xs_ref.at[pl.ds(pl.multiple_of(first + off, MOE_ALIGN), size), :], sem.at[0])
        _for_row_chunks(last - first, copy, lambda cp: cp.start())
        _for_row_chunks(last - first, copy, lambda cp: cp.wait())

    @pl.when(p == 1)
    def _():
        rank = _dot(onehot.astype(BF16), tri_ref[...])
        starts = [jnp.zeros((1, 1), F32)]
        for k in range(1, N_GROUPS):
            starts.append(starts[-1] + npad[k - 1:k, :])
        for k in range(N_GROUPS):
            seg_ref[k] = jnp.max(starts[k]).astype(I32)
        start = jnp.concatenate(starts + [jnp.zeros((8 - N_GROUPS, 1), F32)], axis=0)
        dest = jnp.sum(onehot * (start + rank), axis=0, keepdims=True).astype(I32)
        perm = (lax.broadcasted_iota(I32, (rloc, tm), 0) == dest).astype(BF16)
        slot = i % 2
        loc_ref[slot, :, :D_MODEL] = _dot(perm, x_ref[...].astype(BF16)).astype(BF16)
        w = rt[0:EXPERTS_PER_GROUP, :]
        w_hi, w_lo = _split_bf16(w)
        w_lo2 = (w - w_hi.astype(F32) - w_lo.astype(F32)).astype(BF16)
        w_terms = jnp.concatenate([w_hi, w_lo, w_lo2, jnp.zeros((LANES - 3 * EXPERTS_PER_GROUP, tm), BF16)], axis=0)
        loc_ref[slot, :, D_MODEL:] = _dot_nt(perm, w_terms).astype(BF16)

        for k in range(N_GROUPS):
            def copy(off, size, k=k):
                src = loc_ref.at[slot, pl.ds(pl.multiple_of(seg_ref[k] + off, MOE_ALIGN), size), :]
                dst = xs_ref.at[pl.ds(pl.multiple_of(off_ref[k] + off, MOE_ALIGN), size), :]
                return pltpu.make_async_copy(src, dst, sem.at[slot])
            _for_row_chunks(seg_ref[N_GROUPS + k], copy, lambda cp: cp.start())

        @pl.when(i > 0)
        def _():
            wait_tile(1 - slot)
        for k in range(N_GROUPS):
            meta_ref[i * 2 * N_GROUPS + k] = off_ref[k]
            meta_ref[i * 2 * N_GROUPS + N_GROUPS + k] = seg_ref[N_GROUPS + k]
            off_ref[k] = off_ref[k] + seg_ref[N_GROUPS + k]
            prev_ref[k] = seg_ref[N_GROUPS + k]

    @pl.when((p == 1) & (i == ntiles - 1))
    def _():
        wait_tile((ntiles - 1) % 2)
        for k in range(N_GROUPS):
            bounds_ref[N_GROUPS + k] = off_ref[k]
            fill_zero(off_ref[k], bounds_ref[k + 1] if k + 1 < N_GROUPS else cap)


def _moe_sort(x1, route, tm):
    T = x1.shape[0]
    ntiles = T // tm
    rloc = _ceil_to(tm + N_GROUPS * MOE_ALIGN, LANES)
    cap = _ceil_to(T + MOE_ALIGN * N_GROUPS * ntiles, MOE_TR) + N_GROUPS * MOE_TR
    kern = functools.partial(_moe_sort_kernel, tm=tm, rloc=rloc, ntiles=ntiles, cap=cap)
    smem = pl.BlockSpec(memory_space=pltpu.SMEM)
    return pl.pallas_call(
        kern,
        grid=(2, ntiles),
        in_specs=[pl.BlockSpec((tm, D_MODEL), lambda p, i: (i * p, 0)),
                  pl.BlockSpec((tm, LANES), lambda p, i: (i, 0))],
        out_specs=[pl.BlockSpec(memory_space=pl.ANY), smem, smem],
        out_shape=[jax.ShapeDtypeStruct((cap, MOE_XW), BF16),
                   jax.ShapeDtypeStruct((ntiles * 2 * N_GROUPS,), I32),
                   jax.ShapeDtypeStruct((2 * N_GROUPS,), I32)],
        scratch_shapes=[pltpu.VMEM((2, rloc, MOE_XW), BF16), pltpu.VMEM((MOE_BIG, MOE_XW), BF16),
                        pltpu.VMEM((tm, tm), BF16),
                        pltpu.SMEM((N_GROUPS,), I32), pltpu.SMEM((2 * N_GROUPS,), I32), pltpu.SMEM((N_GROUPS,), I32),
                        pltpu.SemaphoreType.DMA((2,))],
        compiler_params=_params("arbitrary", "arbitrary"),
        name="moe_sort_dispatch",
    )(x1, route)


def _moe_tile_group(r, bounds_ref):
    row = r * MOE_TR
    g = 0
    for k in range(1, N_GROUPS):
        g = g + (row >= bounds_ref[k]).astype(I32)
    return g


def _moe_mlp_kernel(bounds_ref, xs_ref, w13_ref, w2_ref, ys_ref, acc_ref):
    r = pl.program_id(0)
    g = _moe_tile_group(r, bounds_ref)
    end = bounds_ref[N_GROUPS]
    for k in range(1, N_GROUPS):
        end = jnp.where(g == k, bounds_ref[N_GROUPS + k], end)
    used = r * MOE_TR < end

    @pl.when(used)
    def _():
        xb = xs_ref[:, :D_MODEL]
        terms = xs_ref[:, D_MODEL:].astype(F32)
        cw = terms + pltpu.roll(terms, LANES - EXPERTS_PER_GROUP, 1) + pltpu.roll(terms, LANES - 2 * EXPERTS_PER_GROUP, 1)
        for e in range(EXPERTS_PER_GROUP):
            h = _dot(xb, w13_ref[0, e])
            a = h[:, :EXPERT_FF]
            u = h[:, EXPERT_FF:]
            act = ((a * jax.nn.sigmoid(a)) * u * cw[:, e:e + 1]).astype(BF16)
            y = _dot(act, w2_ref[0, e * EXPERT_FF:(e + 1) * EXPERT_FF, :])
            if e == 0:
                acc_ref[...] = y
            else:
                acc_ref[...] += y
        ys_ref[...] = acc_ref[...].astype(BF16)

    @pl.when(jnp.logical_not(used))
    def _():
        ys_ref[...] = jnp.zeros(ys_ref.shape, BF16)


def _moe_mlp(bounds, xs, w13, w2):
    cap = xs.shape[0]
    grid_spec = pltpu.PrefetchScalarGridSpec(
        num_scalar_prefetch=1,
        grid=(cap // MOE_TR,),
        in_specs=[pl.BlockSpec((MOE_TR, MOE_XW), lambda r, b: (r, 0)),
                  pl.BlockSpec((1, EXPERTS_PER_GROUP, D_MODEL, 2 * EXPERT_FF),
                               lambda r, b: (_moe_tile_group(r, b), 0, 0, 0)),
                  pl.BlockSpec((1, EXPERTS_PER_GROUP * EXPERT_FF, D_MODEL),
                               lambda r, b: (_moe_tile_group(r, b), 0, 0))],
        out_specs=pl.BlockSpec((MOE_TR, D_MODEL), lambda r, b: (r, 0)),
        scratch_shapes=[pltpu.VMEM((MOE_TR, D_MODEL), F32)])
    return pl.pallas_call(
        _moe_mlp_kernel,
        grid_spec=grid_spec,
        out_shape=jax.ShapeDtypeStruct((cap, D_MODEL), BF16),
        compiler_params=_params("arbitrary"),
        name="moe_group_experts",
    )(bounds, xs, w13, w2)


def _moe_combine_kernel(meta_ref, x_ref, route_ref, ys_ref, g_ref, b_ref, o_ref, loc_ref, tri_ref, sem,
                        *, alpha, tm, rloc, ntiles):
    i = pl.program_id(0)

    def segments(t):
        offs = [meta_ref[t * 2 * N_GROUPS + k] for k in range(N_GROUPS)]
        rows = [meta_ref[t * 2 * N_GROUPS + N_GROUPS + k] for k in range(N_GROUPS)]
        segs = [0]
        for k in range(N_GROUPS):
            segs.append(segs[-1] + rows[k])
        return offs, rows, segs

    def for_chunks(t, fn):
        offs, rows, segs = segments(t)
        slot = t % 2
        for k in range(N_GROUPS):
            def copy(off, size, k=k):
                src = ys_ref.at[pl.ds(pl.multiple_of(offs[k] + off, MOE_ALIGN), size), :]
                dst = loc_ref.at[slot, pl.ds(pl.multiple_of(segs[k] + off, MOE_ALIGN), size), :]
                return pltpu.make_async_copy(src, dst, sem.at[slot])
            _for_row_chunks(rows[k], copy, fn)

    @pl.when(i == 0)
    def _():
        loc_ref[...] = jnp.zeros(loc_ref.shape, BF16)
        for_chunks(i, lambda cp: cp.start())

    @pl.when(i + 1 < ntiles)
    def _():
        for_chunks(i + 1, lambda cp: cp.start())

    segs = segments(i)[2]
    route = route_ref[...]
    lane = lax.broadcasted_iota(I32, (1, LANES), 1)
    onehot = ((route[:, EXPERTS_PER_GROUP:EXPERTS_PER_GROUP + 1] == lane.astype(F32)) & (lane < N_GROUPS)).astype(F32)
    @pl.when(i == 0)
    def _():
        tri_ref[...] = (lax.broadcasted_iota(I32, (tm, tm), 1) < lax.broadcasted_iota(I32, (tm, tm), 0)).astype(BF16)

    rank = _dot(tri_ref[...], onehot.astype(BF16))
    seg_start = jnp.zeros((1, LANES), F32)
    for k in range(N_GROUPS):
        seg_start = seg_start + jnp.where(lane == k, lax.convert_element_type(segs[k], F32), F32(0.0))
    dest = jnp.sum(onehot * (seg_start + rank), axis=1, keepdims=True).astype(I32)
    unperm = (lax.broadcasted_iota(I32, (1, rloc), 1) == dest).astype(BF16)
    for_chunks(i, lambda cp: cp.wait())
    valid = lax.broadcasted_iota(I32, (rloc, 1), 0) < segs[N_GROUPS]
    ysl = jnp.where(valid, loc_ref[i % 2], jnp.zeros((rloc, D_MODEL), BF16))
    o_ref[...] = _layer_norm(alpha * x_ref[...] + _dot(unperm, ysl), g_ref[...], b_ref[...])


def _moe_combine(meta, x1, route, ys, g, b, alpha, tm):
    T = x1.shape[0]
    rloc = _ceil_to(tm + N_GROUPS * MOE_ALIGN, LANES)
    kern = functools.partial(_moe_combine_kernel, alpha=alpha, tm=tm, rloc=rloc, ntiles=T // tm)
    grid_spec = pltpu.PrefetchScalarGridSpec(
        num_scalar_prefetch=1,
        grid=(T // tm,),
        in_specs=[pl.BlockSpec((tm, D_MODEL), lambda i, m: (i, 0)),
                  pl.BlockSpec((tm, LANES), lambda i, m: (i, 0)),
                  pl.BlockSpec(memory_space=pl.ANY),
                  pl.BlockSpec((1, D_MODEL), lambda i, m: (0, 0)),
                  pl.BlockSpec((1, D_MODEL), lambda i, m: (0, 0))],
        out_specs=pl.BlockSpec((tm, D_MODEL), lambda i, m: (i, 0)),
        scratch_shapes=[pltpu.VMEM((2, rloc, D_MODEL), BF16), pltpu.VMEM((tm, tm), BF16),
                        pltpu.SemaphoreType.DMA((2,))])
    return pl.pallas_call(
        kern,
        grid_spec=grid_spec,
        out_shape=jax.ShapeDtypeStruct((T, D_MODEL), F32),
        compiler_params=_params("arbitrary"),
        name="moe_combine_ln",
    )(meta, x1, route, ys, g, b)


def _pad_cols(a, width):
    return jnp.pad(a, ((0, 0), (0, width - a.shape[1])))


def _pad_heads(w, heads, dim, dim_pad):
    rows = w.shape[0]
    return jnp.pad(w.reshape(rows, heads, dim), ((0, 0), (0, 0), (0, dim_pad - dim))).reshape(rows, heads * dim_pad)


def _tile(n, pref):
    t = pref
    while n % t:
        t //= 2
    return t


def kernel(x, mem, positions, dsa_w_in, dsa_idx_k_g, dsa_idx_k_b, gla_w_in, gla_w_gate, gla_b_gate, gla_norm_g,
           w_mem_kv, w_out, ln1_g, ln1_b, ln2_g, ln2_b, moe_w_group, moe_b_group, moe_w_router, moe_b_router,
           moe_w13, moe_w2):
    B, S, D = x.shape
    T = B * S
    depth = w_out.shape[0]
    alpha = (2 * depth) ** 0.25
    tm = _tile(T, 512)
    tq = _tile(S, 512)
    tq_idx = _tile(S, 256)
    topk = min(DSA_MAX_TOPK, S // 4)

    inv = ROPE_THETA ** (-jnp.arange(0, HEAD_DIM, 2, dtype=F32) / HEAD_DIM)
    ang = positions.astype(F32).reshape(T, 1) * inv
    cos, sin = jnp.cos(ang), jnp.sin(ang)
    cosf = jnp.concatenate([cos, cos, cos, cos], axis=1)
    sinf = jnp.concatenate([-sin, sin, -sin, sin], axis=1)

    xc = x.reshape(T, D)
    ia = ib = 0
    for i in range(depth):
        if i % 2 == 0:
            w = dsa_w_in[ia]
            wq, wk, wv, wqi, wki, wwi, wqm = jnp.split(w, [768, 1536, 2304, 2816, 2880, 2888], axis=1)
            w_tok = jnp.concatenate([wk, wqm, _pad_cols(wki, LANES)], axis=1).astype(BF16)
            w_feat = jnp.concatenate([wq, wv, wqi, _pad_cols(wwi, 16)], axis=1).T.astype(BF16)
            lng = _pad_cols(dsa_idx_k_g[ia][None, :], LANES)
            lnb = _pad_cols(dsa_idx_k_b[ia][None, :], LANES)
            k, qm, ki, q_t, v_t, qi_t, wi_t = _proj_dsa(xc, w_tok, w_feat, cosf, sinf, cosf.T, sinf.T, lng, lnb,
                                                         tm, tq, tq_idx)
            bias = _idx_mask(qi_t, ki.reshape(B, S, LANES), wi_t, B, S, topk, tq_idx)
            seq = _dsa_attn(q_t, k.reshape(B, S, SEQ_WIDTH), v_t, bias, B, S, tq, 2).reshape(T, SEQ_WIDTH)
            wa = w_out[i][:SEQ_WIDTH].astype(BF16)
            ia += 1
        else:
            w = gla_w_in[ib]
            wq, wk, wv, wr_, wa1, wqm = jnp.split(w, [384, 768, 1536, 2304, 2320], axis=1)
            w_all = jnp.concatenate([
                _pad_heads(wq, GLA_HEADS, GLA_DK, GLA_DKP), _pad_heads(wk, GLA_HEADS, GLA_DK, GLA_DKP),
                _pad_heads(wv, GLA_HEADS, GLA_DV, GLA_DVP), _pad_heads(wr_, GLA_HEADS, GLA_DV, GLA_DVP),
                wqm, _pad_cols(wa1, LANES)], axis=1).astype(BF16)
            q, k, v, r, qm, a1 = _proj_gla(xc, w_all, tm)
            r3 = lambda a: a.reshape(B, S, a.shape[1])
            wg = _pad_heads(gla_w_gate[ib], GLA_HEADS, GLA_DK, GLA_DKP)
            wg = jnp.pad(wg, ((0, LANES - GLA_GATE_RANK), (0, 0)))
            wg = wg.reshape(LANES, GLA_HEADS, GLA_DKP).transpose(1, 0, 2)
            bg = _pad_heads(gla_b_gate[ib][None, :], GLA_HEADS, GLA_DK, GLA_DKP).reshape(GLA_HEADS, 1, GLA_DKP)
            ng = _pad_cols(gla_norm_g[ib][None, :], GLA_DVP)
            seq = _gla(r3(q), r3(k), r3(v), r3(r), r3(a1), wg, bg, ng).reshape(T, GLA_HEADS * GLA_DVP)
            wa = w_out[i][:SEQ_WIDTH].reshape(GLA_HEADS, GLA_DV, D)
            wa = jnp.pad(wa, ((0, 0), (0, GLA_DVP - GLA_DV), (0, 0))).reshape(GLA_HEADS * GLA_DVP, D).astype(BF16)
            ib += 1
        memo = _mem_attn(qm.reshape(B, S, MEM_WIDTH), mem, w_mem_kv[i].astype(BF16), tq).reshape(T, MEM_WIDTH)
        wb = w_out[i][SEQ_WIDTH:].astype(BF16)
        wr = jnp.concatenate([moe_w_group[i], moe_w_router[i].transpose(1, 0, 2).reshape(D, N_EXPERTS)], axis=1)
        wr = _pad_cols(wr, LANES)
        br = _pad_cols(jnp.concatenate([moe_b_group[i], moe_b_router[i].reshape(-1)])[None, :], LANES)
        x1, route = _mix_router(seq, memo, wa, wb, xc, ln1_g[i][None, :], ln1_b[i][None, :], wr, br, alpha,
                                _tile(T, 1024))
        w13g = moe_w13[i].astype(BF16)
        w2g = moe_w2[i].reshape(N_GROUPS, EXPERTS_PER_GROUP * EXPERT_FF, D).astype(BF16)
        tmoe = _tile(T, MOE_TM)
        xs, meta, bounds = _moe_sort(x1, route, tmoe)
        ys = _moe_mlp(bounds, xs, w13g, w2g)
        xc = _moe_combine(meta, x1, route, ys, ln2_g[i][None, :], ln2_b[i][None, :], alpha, tmoe)
    return xc.reshape(B, S, D)
```

```python
import functools
import math

import jax
import jax.numpy as jnp
from jax import lax
from jax.experimental import pallas as pl
from jax.experimental.pallas import tpu as pltpu

F32 = jnp.float32
BF16 = jnp.bfloat16
I32 = jnp.int32
I16 = jnp.int16

LANES = 128
D_MODEL = 1024
HEAD_DIM = 64
N_MEM_HEADS = 4
MEM_WIDTH = N_MEM_HEADS * HEAD_DIM
SEQ_WIDTH = D_MODEL - MEM_WIDTH
ROPE_THETA = 10000.0
DSA_HEADS = SEQ_WIDTH // HEAD_DIM
IDX_HEADS = 8
IDX_DIM = 64
DSA_MAX_TOPK = 256
GLA_HEADS = 4
GLA_DV = SEQ_WIDTH // GLA_HEADS
GLA_DK = GLA_DV // 2
GLA_DKP = 128
GLA_DVP = 256
GLA_GATE_RANK = 16
GLA_TAU = 16.0
GLA_CHUNK = 64
N_GROUPS = 4
EXPERTS_PER_GROUP = 8
N_EXPERTS = N_GROUPS * EXPERTS_PER_GROUP
EXPERT_FF = 256
LN_EPS = 1e-5
RMS_EPS = 1e-6
MASK_BIAS = -1e30
INT_MIN = -2 ** 31
VMEM_LIMIT = 56 * 1024 * 1024


def _dot(a, b):
    return jnp.dot(a, b, preferred_element_type=F32)


def _dot_nt(a, b):
    return lax.dot_general(a, b, (((1,), (1,)), ((), ())), preferred_element_type=F32)


def _dot_tn(a, b):
    return lax.dot_general(a, b, (((0,), (0,)), ((), ())), preferred_element_type=F32)


def _split_bf16(a):
    hi = a.astype(BF16)
    lo = (a - hi.astype(F32)).astype(BF16)
    return hi, lo


def _dot_f32(a, b):
    ah, al = _split_bf16(a)
    bh, bl = _split_bf16(b)
    return _dot(ah, bh) + (_dot(ah, bl) + _dot(al, bh))


def _layer_norm(y, g, b):
    mu = jnp.mean(y, axis=-1, keepdims=True)
    yc = y - mu
    var = jnp.mean(yc * yc, axis=-1, keepdims=True)
    return yc * lax.rsqrt(var + LN_EPS) * g + b


def _params(*sem):
    return pltpu.CompilerParams(dimension_semantics=sem, vmem_limit_bytes=VMEM_LIMIT)


def _rope_fn(cos, sin, axis):
    shape = (1, LANES) if axis == 1 else (LANES, 1)
    pos = lax.broadcasted_iota(I32, shape, axis)
    first_half = (pos % HEAD_DIM) < (HEAD_DIM // 2)

    def rope(a):
        swapped = jnp.where(first_half, pltpu.roll(a, LANES - HEAD_DIM // 2, axis),
                            pltpu.roll(a, HEAD_DIM // 2, axis))
        return a * cos + swapped * sin
    return rope


def _proj_dsa_kernel(x_ref, wa_ref, wb_ref, cos_ref, sin_ref, cost_ref, sint_ref, lng_ref, lnb_ref,
                     k_ref, qm_ref, ki_ref, qt_ref, vt_ref, qit_ref, wit_ref):
    xb = x_ref[...].astype(BF16)
    rope = _rope_fn(cos_ref[...], sin_ref[...], 1)
    rope_t = _rope_fn(cost_ref[...], sint_ref[...], 0)
    tok = _dot(xb, wa_ref[...])
    for c in range(SEQ_WIDTH // LANES):
        k_ref[:, c * LANES:(c + 1) * LANES] = rope(tok[:, c * LANES:(c + 1) * LANES]).astype(BF16)
    qm_ref[...] = tok[:, SEQ_WIDTH:SEQ_WIDTH + MEM_WIDTH].astype(BF16)
    acc = tok[:, D_MODEL:D_MODEL + LANES]
    lane = lax.broadcasted_iota(I32, (1, LANES), 1)
    is_ki = lane < IDX_DIM
    mu = jnp.sum(jnp.where(is_ki, acc, 0.0), axis=1, keepdims=True) * (1.0 / IDX_DIM)
    d = jnp.where(is_ki, acc - mu, 0.0)
    var = jnp.sum(d * d, axis=1, keepdims=True) * (1.0 / IDX_DIM)
    ki_ref[...] = rope(d * lax.rsqrt(var + LN_EPS) * lng_ref[...] + lnb_ref[...]).astype(BF16)

    def store_t(out_ref, r, val):
        nblk, _, tb = out_ref.shape
        rows = val.shape[0]
        for j in range(nblk):
            out_ref[j, r * rows:(r + 1) * rows, :] = val[:, j * tb:(j + 1) * tb].astype(out_ref.dtype)

    q_scale = HEAD_DIM ** -0.5 * math.log2(math.e)
    feat = _dot_nt(wb_ref[...], xb)
    for r in range(SEQ_WIDTH // LANES):
        store_t(qt_ref, r, rope_t(feat[r * LANES:(r + 1) * LANES]) * q_scale)
    off = SEQ_WIDTH
    for r in range(SEQ_WIDTH // LANES):
        store_t(vt_ref, r, feat[off + r * LANES:off + (r + 1) * LANES])
    off = 2 * SEQ_WIDTH
    for r in range(IDX_HEADS * IDX_DIM // LANES):
        store_t(qit_ref, r, rope_t(feat[off + r * LANES:off + (r + 1) * LANES]))
    off = 2 * SEQ_WIDTH + IDX_HEADS * IDX_DIM
    store_t(wit_ref, 0, feat[off:off + IDX_HEADS] * (IDX_HEADS ** -0.5 * IDX_DIM ** -0.5))


def _proj_dsa(x2d, wa, wb, cosf, sinf, cost, sint, lng, lnb, tm, tq_att, tq_idx):
    T = x2d.shape[0]
    row = lambda n: pl.BlockSpec((tm, n), lambda i: (i, 0))
    col = pl.BlockSpec((LANES, tm), lambda i: (0, i))
    full = lambda a: pl.BlockSpec(a.shape, lambda i: (0,) * a.ndim)
    featmaj = lambda n, tb: pl.BlockSpec((tm // tb, n, tb), lambda i: (i, 0, 0))
    fshape = lambda n, tb, dt: jax.ShapeDtypeStruct((T // tb, n, tb), dt)
    return pl.pallas_call(
        _proj_dsa_kernel,
        grid=(T // tm,),
        in_specs=[row(D_MODEL), full(wa), full(wb), row(LANES), row(LANES), col, col, full(lng), full(lnb)],
        out_specs=[row(SEQ_WIDTH), row(MEM_WIDTH), row(LANES),
                   featmaj(SEQ_WIDTH, tq_att), featmaj(SEQ_WIDTH, tq_att),
                   featmaj(IDX_HEADS * IDX_DIM, tq_idx), featmaj(IDX_HEADS, tq_idx)],
        out_shape=[jax.ShapeDtypeStruct((T, SEQ_WIDTH), BF16), jax.ShapeDtypeStruct((T, MEM_WIDTH), BF16),
                   jax.ShapeDtypeStruct((T, LANES), BF16),
                   fshape(SEQ_WIDTH, tq_att, BF16), fshape(SEQ_WIDTH, tq_att, BF16),
                   fshape(IDX_HEADS * IDX_DIM, tq_idx, BF16), fshape(IDX_HEADS, tq_idx, F32)],
        compiler_params=_params("parallel"),
        name="proj_dsa",
    )(x2d, wa, wb, cosf, sinf, cost, sint, lng, lnb)


def _proj_gla_kernel(x_ref, w_ref, q_ref, k_ref, v_ref, r_ref, qm_ref, a_ref):
    xb = x_ref[...].astype(BF16)
    segments = ((0, 512, q_ref), (512, 512, k_ref), (1024, 1024, v_ref), (2048, 1024, r_ref),
                (3072, 256, qm_ref), (3328, 128, a_ref))
    for off, width, out_ref in segments:
        out_ref[...] = _dot(xb, w_ref[:, off:off + width]).astype(out_ref.dtype)


def _proj_gla(x2d, w, tm):
    T = x2d.shape[0]
    row = lambda n: pl.BlockSpec((tm, n), lambda i: (i, 0))
    outs = [(512, BF16), (512, BF16), (1024, BF16), (1024, BF16), (256, BF16), (LANES, F32)]
    return pl.pallas_call(
        _proj_gla_kernel,
        grid=(T // tm,),
        in_specs=[row(D_MODEL), pl.BlockSpec(w.shape, lambda i: (0, 0))],
        out_specs=[row(n) for n, _ in outs],
        out_shape=[jax.ShapeDtypeStruct((T, n), dt) for n, dt in outs],
        compiler_params=_params("parallel"),
        name="proj_gla",
    )(x2d, w)


def _idx_kernel(qi_ref, ki_ref, wi_ref, bias_ref, keys_ref, hi_ref, lo_ref, cut_ref, *, tq, tk, nkb, topk, seq):
    qt = pl.program_id(1)
    n_act = qt + 1
    key_l = lax.broadcasted_iota(I32, (tk, tq), 0)
    qry_g = qt * tq + lax.broadcasted_iota(I32, (tk, tq), 1)
    wi = wi_ref[0]

    def score_body(kb, carry):
        kblk = ki_ref[0, pl.ds(pl.multiple_of(kb * tk, tk), tk), :][:, :IDX_DIM]
        sc = jnp.zeros((tk, tq), F32)
        for h in range(IDX_HEADS):
            qh = qi_ref[0, h * IDX_DIM:(h + 1) * IDX_DIM, :]
            sc = sc + jnp.maximum(_dot(kblk, qh), 0.0) * wi[h:h + 1, :]
        bits = lax.bitcast_convert_type(sc, I32)
        key = bits ^ ((bits >> 31) & 0x7FFFFFFF)
        key = jnp.where(kb * tk + key_l > qry_g, INT_MIN, key)
        keys_ref[kb] = key
        hi_ref[kb] = (key >> 16).astype(I16)
        lo_ref[kb] = (key ^ 0x8000).astype(I16)
        return carry

    lax.fori_loop(0, n_act, score_body, 0)

    def count16(ref, cand, strict=False):
        cand16 = cand.astype(I16)

        def body(kb, acc):
            v = ref[kb]
            m = jnp.where((v > cand16) if strict else (v >= cand16), jnp.int16(1), jnp.int16(0))
            parts = [m[j * 16:(j + 1) * 16] for j in range(4)]
            for j in range(4, tk // 16):
                parts[j % 4] = parts[j % 4] + m[j * 16:(j + 1) * 16]
            return acc + ((parts[0] + parts[1]) + (parts[2] + parts[3]))
        acc = lax.fori_loop(0, n_act, body, jnp.zeros((16, tq), I16))
        return jnp.sum(acc.astype(I32).astype(F32), axis=0, keepdims=True)

    def search16(ref, need):
        def bit_body(i, base_u):
            cand_u = base_u | lax.shift_left(jnp.int32(1), lax.convert_element_type(15 - i, I32))
            cnt = count16(ref, cand_u - 32768)
            return jnp.where(cnt >= need, cand_u, base_u)
        return lax.fori_loop(0, 16, bit_body, jnp.zeros((1, tq), I32)) - 32768

    base_hi = search16(hi_ref, topk)
    above = count16(hi_ref, base_hi, strict=True)
    base_hi16 = base_hi.astype(I16)

    def bucket_body(kb, carry):
        lo_ref[kb] = jnp.where(hi_ref[kb] == base_hi16, lo_ref[kb], jnp.int16(-32768))
        return carry

    lax.fori_loop(0, n_act, bucket_body, 0)
    base_lo = search16(lo_ref, topk - above)
    base = lax.shift_left(base_hi, jnp.int32(16)) | (base_lo + 32768)

    def count(pred):
        def body(kb, acc):
            m = pred(keys_ref[kb], kb).astype(F32)
            parts = [m[j * 8:(j + 1) * 8] for j in range(4)]
            for j in range(4, tk // 8):
                parts[j % 4] = parts[j % 4] + m[j * 8:(j + 1) * 8]
            return acc + ((parts[0] + parts[1]) + (parts[2] + parts[3]))
        acc = lax.fori_loop(0, n_act, body, jnp.zeros((8, tq), F32))
        return jnp.sum(acc, axis=0, keepdims=True)

    cnt_gt = count(lambda kk, kb: kk > base)
    cnt_ge = count(lambda kk, kb: kk >= base)
    need = topk - cnt_gt
    tie = (cnt_ge > topk) & (base != INT_MIN)
    cut_ref[...] = jnp.full((1, tq), seq, I32)

    @pl.when(jnp.max(tie.astype(F32)) > 0.0)
    def _():
        nbits = int(math.log2(seq))

        def idx_body(i, m):
            cand = m | lax.shift_left(jnp.int32(1), lax.convert_element_type(nbits - 1 - i, I32))
            below = count(lambda kk, kb: (kk == base) & (kb * tk + key_l < cand))
            return jnp.where(below < need, cand, m)

        m = lax.fori_loop(0, nbits, idx_body, jnp.zeros((1, tq), I32))
        cut_ref[...] = jnp.where(tie, m, seq)

    cut = cut_ref[...]
    for kb in range(nkb):
        @pl.when(kb <= qt)
        def _():
            kk = keys_ref[kb]
            sel = ((kk > base) | ((kk == base) & (kb * tk + key_l <= cut))) & (kk != INT_MIN)
            bias_ref[0, 0, kb] = jnp.where(sel, F32(0.0), F32(MASK_BIAS)).astype(BF16)

        @pl.when(kb > qt)
        def _():
            bias_ref[0, 0, kb] = jnp.full((tk, tq), MASK_BIAS, BF16)


def _idx_mask(qit, ki, wit, B, S, topk, tq):
    nq = S // tq
    kern = functools.partial(_idx_kernel, tq=tq, tk=tq, nkb=nq, topk=topk, seq=S)
    return pl.pallas_call(
        kern,
        grid=(B, nq),
        in_specs=[pl.BlockSpec((1, IDX_HEADS * IDX_DIM, tq), lambda b, q: (b * nq + q, 0, 0)),
                  pl.BlockSpec((1, S, LANES), lambda b, q: (b, 0, 0)),
                  pl.BlockSpec((1, IDX_HEADS, tq), lambda b, q: (b * nq + q, 0, 0))],
        out_specs=pl.BlockSpec((1, 1, nq, tq, tq), lambda b, q: (b, q, 0, 0, 0)),
        out_shape=jax.ShapeDtypeStruct((B, nq, nq, tq, tq), BF16),
        scratch_shapes=[pltpu.VMEM((nq, tq, tq), I32), pltpu.VMEM((nq, tq, tq), I16), pltpu.VMEM((nq, tq, tq), I16),
                        pltpu.VMEM((1, tq), I32)],
        compiler_params=_params("parallel", "arbitrary"),
        name="dsa_index_select",
    )(qit, ki, wit)


def _dsa_attn_kernel(q_ref, k_ref, v_ref, bias_ref, o_ref, m_ref, acc_ref, *, tq, tk, sub):
    qt = pl.program_id(2)
    nsub = tq // sub
    npair = q_ref.shape[1] // LANES
    nh = 2 * npair
    low = lax.broadcasted_iota(I32, (LANES, 1), 0) < HEAD_DIM
    q_heads = []
    for pr in range(npair):
        q = q_ref[0, pr * LANES:(pr + 1) * LANES, :]
        zero = jnp.zeros_like(q)
        q_heads += [jnp.where(low, q, zero), jnp.where(low, zero, q)]
    m_ref[...] = jnp.full(m_ref.shape, -jnp.inf, F32)
    acc_ref[...] = jnp.zeros(acc_ref.shape, F32)

    def body(kb, carry):
        kblk = k_ref[0, pl.ds(pl.multiple_of(kb * tk, tk), tk), :]
        v_heads = []
        for pr in range(npair):
            vt = v_ref[kb, pr * LANES:(pr + 1) * LANES, :]
            one = jnp.ones_like(vt)
            v_heads += [jnp.where(low, vt, one), jnp.where(low, one, vt)]
        bias = jnp.concatenate(
            [jnp.concatenate([bias_ref[0, c, kb * nsub + a] for c in range(nsub)], axis=1) for a in range(nsub)],
            axis=0).astype(F32)
        s = [_dot(kblk[:, (h // 2) * LANES:(h // 2 + 1) * LANES], q_heads[h]) + bias for h in range(nh)]
        m_old = [m_ref[h] for h in range(nh)]
        m_new = [jnp.maximum(m_old[h], jnp.max(s[h], axis=0, keepdims=True)) for h in range(nh)]
        p = [jnp.exp2(s[h] - m_new[h]).astype(BF16) for h in range(nh)]
        pv = [_dot(v_heads[h], p[h]) for h in range(nh)]
        for h in range(nh):
            acc_ref[h] = jnp.exp2(m_old[h] - m_new[h]) * acc_ref[h] + pv[h]
            m_ref[h] = m_new[h]
        return carry

    lax.fori_loop(0, qt + 1, body, 0)
    for pr in range(npair):
        a0 = acc_ref[2 * pr]
        a1 = acc_ref[2 * pr + 1]
        o = jnp.where(low, a0 / a0[HEAD_DIM:HEAD_DIM + 1, :], a1 / a1[0:1, :])
        o_ref[0, :, pr * LANES:(pr + 1) * LANES] = o.T.astype(o_ref.dtype)


def _dsa_attn(qt, k, vt, bias, B, S, tq, npair):
    nq = S // tq
    width = npair * LANES
    nsb, sub = bias.shape[2], bias.shape[3]
    kern = functools.partial(_dsa_attn_kernel, tq=tq, tk=tq, sub=sub)
    return pl.pallas_call(
        kern,
        grid=(B, SEQ_WIDTH // width, nq),
        in_specs=[pl.BlockSpec((1, width, tq), lambda b, h, i: (b * nq + i, h, 0)),
                  pl.BlockSpec((1, S, width), lambda b, h, i: (b, 0, h)),
                  pl.BlockSpec((nq, width, tq), lambda b, h, i: (b, h, 0)),
                  pl.BlockSpec((1, tq // sub, nsb, sub, sub), lambda b, h, i: (b, i, 0, 0, 0))],
        out_specs=pl.BlockSpec((1, tq, width), lambda b, h, i: (b, i, h)),
        out_shape=jax.ShapeDtypeStruct((B, S, SEQ_WIDTH), BF16),
        scratch_shapes=[pltpu.VMEM((2 * npair, 1, tq), F32), pltpu.VMEM((2 * npair, LANES, tq), F32)],
        compiler_params=_params("parallel", "parallel", "arbitrary"),
        name="dsa_attention",
    )(qt, k, vt, bias)


def _mem_attn_kernel(qm_ref, mem_ref, wkv_ref, o_ref, k_scr, v_scr):
    @pl.when(pl.program_id(1) == 0)
    def _():
        kv = _dot(mem_ref[0].astype(BF16), wkv_ref[...])
        k_scr[...] = kv[:, :MEM_WIDTH].astype(BF16)
        v_scr[...] = kv[:, MEM_WIDTH:].astype(BF16)

    lane = lax.broadcasted_iota(I32, (1, LANES), 1)
    low = lane < HEAD_DIM
    scale = HEAD_DIM ** -0.5
    for pair in range(MEM_WIDTH // LANES):
        cols = slice(pair * LANES, (pair + 1) * LANES)
        q = qm_ref[0, :, cols]
        zero = jnp.zeros_like(q)
        kp = k_scr[:, cols]
        vp = v_scr[:, cols]
        outs = []
        for qh in (jnp.where(low, q, zero), jnp.where(low, zero, q)):
            s = _dot_nt(qh, kp) * scale
            s = s - jnp.max(s, axis=1, keepdims=True)
            p = jnp.exp(s)
            p = p / jnp.sum(p, axis=1, keepdims=True)
            outs.append(_dot(p.astype(BF16), vp))
        o_ref[0, :, cols] = jnp.where(low, outs[0], outs[1]).astype(o_ref.dtype)


def _mem_attn(qm, mem, wkv, tq):
    B, S, _ = qm.shape
    M = mem.shape[1]
    return pl.pallas_call(
        _mem_attn_kernel,
        grid=(B, S // tq),
        in_specs=[pl.BlockSpec((1, tq, MEM_WIDTH), lambda b, i: (b, i, 0)),
                  pl.BlockSpec((1, M, D_MODEL), lambda b, i: (b, 0, 0)),
                  pl.BlockSpec(wkv.shape, lambda b, i: (0, 0))],
        out_specs=pl.BlockSpec((1, tq, MEM_WIDTH), lambda b, i: (b, i, 0)),
        out_shape=jax.ShapeDtypeStruct((B, S, MEM_WIDTH), BF16),
        scratch_shapes=[pltpu.VMEM((M, MEM_WIDTH), BF16), pltpu.VMEM((M, MEM_WIDTH), BF16)],
        compiler_params=_params("parallel", "arbitrary"),
        name="memory_attention",
    )(qm, mem, wkv)


def _gla_kernel(q_ref, k_ref, v_ref, r_ref, a_ref, wg_ref, bg_ref, ng_ref, o_ref,
                qb_ref, kb_ref, kd_ref, dl_ref, oi_ref, kv_ref, *, nchunk):
    C = GLA_CHUNK
    S = nchunk * C
    z = _dot_f32(a_ref[0], wg_ref[0]) + bg_ref[0]
    b = (jnp.minimum(z, 0.0) - jnp.log1p(jnp.exp(-jnp.abs(z)))) * (1.0 / GLA_TAU)
    pos = lax.broadcasted_iota(I32, (S, 1), 0) % C
    shift = 1
    while shift < C:
        b = b + jnp.where(pos >= shift, pltpu.roll(b, shift, 0), 0.0)
        shift *= 2
    b3 = b.reshape(nchunk, C, GLA_DKP)
    b_last = b3[:, C - 1:C, :]
    q = q_ref[0].astype(F32) * (GLA_DK ** -0.5)
    k = k_ref[0].astype(F32)
    qb_ref[...] = (q * jnp.exp(b)).astype(BF16)
    kb_ref[...] = (k * jnp.exp(-b)).astype(BF16)
    kd_ref[...] = (k.reshape(nchunk, C, GLA_DKP) * jnp.exp(b_last - b3)).reshape(S, GLA_DKP).astype(BF16)
    dl_ref[...] = jnp.exp(b_last)
    causal = lax.broadcasted_iota(I32, (C, C), 0) >= lax.broadcasted_iota(I32, (C, C), 1)

    def intra(c, carry):
        rows = pl.ds(pl.multiple_of(c * C, C), C)
        v = v_ref[0, rows, :]
        attn = jnp.where(causal, _dot_nt(qb_ref[rows, :], kb_ref[rows, :]), 0.0)
        oi_ref[rows, :] = _dot(attn.astype(BF16), v)
        kv_ref[c] = _dot_tn(v, kd_ref[rows, :])
        return carry

    lax.fori_loop(0, nchunk, intra, 0, unroll=8)
    ng = ng_ref[...]

    def inter(c, st):
        rows = pl.ds(pl.multiple_of(c * C, C), C)
        o = oi_ref[rows, :] + _dot_nt(qb_ref[rows, :], st.astype(BF16))
        ms = jnp.sum(o * o, axis=1, keepdims=True) * (1.0 / GLA_DV)
        o = o * lax.rsqrt(ms + RMS_EPS) * ng
        r = r_ref[0, rows, :].astype(F32)
        o_ref[0, rows, :] = (o * (r * jax.nn.sigmoid(r))).astype(o_ref.dtype)
        return st * dl_ref[c] + kv_ref[c]

    lax.fori_loop(0, nchunk, inter, jnp.zeros((GLA_DVP, GLA_DKP), F32), unroll=8)


def _gla(q, k, v, r, a1, wg, bg, ng):
    B, S, _ = q.shape
    nchunk = S // GLA_CHUNK
    kern = functools.partial(_gla_kernel, nchunk=nchunk)
    kspec = pl.BlockSpec((1, S, GLA_DKP), lambda b, h: (b, 0, h))
    vspec = pl.BlockSpec((1, S, GLA_DVP), lambda b, h: (b, 0, h))
    return pl.pallas_call(
        kern,
        grid=(B, GLA_HEADS),
        in_specs=[kspec, kspec, vspec, vspec,
                  pl.BlockSpec((1, S, LANES), lambda b, h: (b, 0, 0)),
                  pl.BlockSpec((1, LANES, GLA_DKP), lambda b, h: (h, 0, 0)),
                  pl.BlockSpec((1, 1, GLA_DKP), lambda b, h: (h, 0, 0)),
                  pl.BlockSpec((1, GLA_DVP), lambda b, h: (0, 0))],
        out_specs=vspec,
        out_shape=jax.ShapeDtypeStruct((B, S, GLA_HEADS * GLA_DVP), BF16),
        scratch_shapes=[pltpu.VMEM((S, GLA_DKP), BF16), pltpu.VMEM((S, GLA_DKP), BF16), pltpu.VMEM((S, GLA_DKP), BF16),
                        pltpu.VMEM((nchunk, 1, GLA_DKP), F32), pltpu.VMEM((S, GLA_DVP), F32),
                        pltpu.VMEM((nchunk, GLA_DVP, GLA_DKP), F32)],
        compiler_params=_params("parallel", "parallel"),
        name="gla",
    )(q, k, v, r, a1, wg, bg, ng)


def _mix_router_kernel(seq_ref, memo_ref, wa_ref, wb_ref, x_ref, g_ref, b_ref, wr_ref, br_ref,
                       x1_ref, comb_ref, *, alpha):
    mixed = _dot(seq_ref[...], wa_ref[...]) + _dot(memo_ref[...], wb_ref[...])
    x1 = _layer_norm(alpha * x_ref[...] + mixed, g_ref[...], b_ref[...])
    x1_ref[...] = x1
    logits = _dot_f32(x1, wr_ref[...]) + br_ref[...]
    lane = lax.broadcasted_iota(I32, logits.shape, 1).astype(F32)
    neg = -jnp.inf
    none = F32(LANES)
    glog = jnp.where(lane < N_GROUPS, logits, neg)
    gmax = jnp.max(glog, axis=1, keepdims=True)
    gsel = jnp.min(jnp.where(glog == gmax, lane, none), axis=1, keepdims=True)
    pg = 1.0 / jnp.sum(jnp.exp(glog - gmax), axis=1, keepdims=True)
    lo = N_GROUPS + EXPERTS_PER_GROUP * gsel
    elog = jnp.where((lane >= lo) & (lane < lo + EXPERTS_PER_GROUP), logits, neg)
    v1 = jnp.max(elog, axis=1, keepdims=True)
    i1 = jnp.min(jnp.where(elog == v1, lane, none), axis=1, keepdims=True)
    elog2 = jnp.where(lane == i1, neg, elog)
    v2 = jnp.max(elog2, axis=1, keepdims=True)
    i2 = jnp.min(jnp.where(elog2 == v2, lane, none), axis=1, keepdims=True)
    e2 = jnp.exp(v2 - v1)
    den = 1.0 + e2
    comb_ref[...] = (jnp.where(lane == i1 - lo, pg / den, 0.0) + jnp.where(lane == i2 - lo, pg * e2 / den, 0.0)
                     + jnp.where(lane == EXPERTS_PER_GROUP, gsel, 0.0))


def _mix_router(seq, memo, wa, wb, x2d, g, b, wr, br, alpha, tm):
    T = x2d.shape[0]
    row = lambda n: pl.BlockSpec((tm, n), lambda i: (i, 0))
    full = lambda a: pl.BlockSpec(a.shape, lambda i: (0,) * a.ndim)
    kern = functools.partial(_mix_router_kernel, alpha=alpha)
    return pl.pallas_call(
        kern,
        grid=(T // tm,),
        in_specs=[row(seq.shape[1]), row(MEM_WIDTH), full(wa), full(wb), row(D_MODEL), full(g), full(b),
                  full(wr), full(br)],
        out_specs=[row(D_MODEL), row(LANES)],
        out_shape=[jax.ShapeDtypeStruct((T, D_MODEL), F32), jax.ShapeDtypeStruct((T, LANES), F32)],
        compiler_params=_params("parallel"),
        name="outproj_ln_router",
    )(seq, memo, wa, wb, x2d, g, b, wr, br)


MOE_ALIGN = 16
MOE_BIG = 64
MOE_TM = 1024
MOE_TR = 1024
MOE_XW = D_MODEL + LANES


def _ceil_to(v, m):
    return (v + (m - 1)) // m * m


def _for_row_chunks(rows, make_copy, fn):
    nbig = rows // MOE_BIG

    def big(c, carry):
        fn(make_copy(pl.multiple_of(c * MOE_BIG, MOE_ALIGN), MOE_BIG))
        return carry

    def small(c, carry):
        fn(make_copy(pl.multiple_of(nbig * MOE_BIG + c * MOE_ALIGN, MOE_ALIGN), MOE_ALIGN))
        return carry

    lax.fori_loop(0, nbig, big, 0)
    lax.fori_loop(0, (rows - nbig * MOE_BIG) // MOE_ALIGN, small, 0)


def _moe_sort_kernel(x_ref, route_ref, xs_ref, meta_ref, bounds_ref,
                     loc_ref, zero_ref, tri_ref, off_ref, seg_ref, prev_ref, sem,
                     *, tm, rloc, ntiles, cap):
    p = pl.program_id(0)
    i = pl.program_id(1)
    rt = route_ref[...].T
    gid = rt[EXPERTS_PER_GROUP:EXPERTS_PER_GROUP + 1, :]
    grp = lax.broadcasted_iota(I32, (8, 1), 0).astype(F32)
    onehot = (gid == grp).astype(F32)
    cnt = jnp.sum(onehot, axis=1, keepdims=True)
    npad = jnp.floor((cnt + (MOE_ALIGN - 1)) * (1.0 / MOE_ALIGN)) * MOE_ALIGN
    for k in range(N_GROUPS):
        seg_ref[N_GROUPS + k] = jnp.max(npad[k:k + 1, :]).astype(I32)

    @pl.when((p == 0) & (i == 0))
    def _():
        for k in range(N_GROUPS):
            off_ref[k] = 0
        zero_ref[...] = jnp.zeros(zero_ref.shape, BF16)
        tri_ref[...] = (lax.broadcasted_iota(I32, (tm, tm), 0) < lax.broadcasted_iota(I32, (tm, tm), 1)).astype(BF16)

    @pl.when(p == 0)
    def _():
        for k in range(N_GROUPS):
            off_ref[k] = off_ref[k] + seg_ref[N_GROUPS + k]

    @pl.when((p == 1) & (i == 0))
    def _():
        base = 0
        for k in range(N_GROUPS):
            total = off_ref[k]
            bounds_ref[k] = base
            off_ref[k] = base
            base = base + _ceil_to(total, MOE_TR)

    def wait_tile(slot):
        for k in range(N_GROUPS):
            def copy(off, size):
                return pltpu.make_async_copy(loc_ref.at[slot, pl.ds(off, size), :], xs_ref.at[pl.ds(off, size), :],
                                             sem.at[slot])
            _for_row_chunks(prev_ref[k], copy, lambda cp: cp.wait())

    def fill_zero(first, last):
        def copy(off, size):
            return pltpu.make_async_copy(zero_ref.at[pl.ds(0, size), :],
                                         xs_ref.at[pl.ds(pl.multiple_of(first + off, MOE_ALIGN), size), :], sem.at[0])
        _for_row_chunks(last - first, copy, lambda cp: cp.start())
        _for_row_chunks(last - first, copy, lambda cp: cp.wait())

    @pl.when(p == 1)
    def _():
        rank = _dot(onehot.astype(BF16), tri_ref[...])
        starts = [jnp.zeros((1, 1), F32)]
        for k in range(1, N_GROUPS):
            starts.append(starts[-1] + npad[k - 1:k, :])
        for k in range(N_GROUPS):
            seg_ref[k] = jnp.max(starts[k]).astype(I32)
        start = jnp.concatenate(starts + [jnp.zeros((8 - N_GROUPS, 1), F32)], axis=0)
        dest = jnp.sum(onehot * (start + rank), axis=0, keepdims=True).astype(I32)
        perm = (lax.broadcasted_iota(I32, (rloc, tm), 0) == dest).astype(BF16)
        slot = i % 2
        loc_ref[slot, :, :D_MODEL] = _dot(perm, x_ref[...].astype(BF16)).astype(BF16)
        w = rt[0:EXPERTS_PER_GROUP, :]
        w_hi, w_lo = _split_bf16(w)
        w_lo2 = (w - w_hi.astype(F32) - w_lo.astype(F32)).astype(BF16)
        w_terms = jnp.concatenate([w_hi, w_lo, w_lo2, jnp.zeros((LANES - 3 * EXPERTS_PER_GROUP, tm), BF16)], axis=0)
        loc_ref[slot, :, D_MODEL:] = _dot_nt(perm, w_terms).astype(BF16)

        for k in range(N_GROUPS):
            def copy(off, size, k=k):
                src = loc_ref.at[slot, pl.ds(pl.multiple_of(seg_ref[k] + off, MOE_ALIGN), size), :]
                dst = xs_ref.at[pl.ds(pl.multiple_of(off_ref[k] + off, MOE_ALIGN), size), :]
                return pltpu.make_async_copy(src, dst, sem.at[slot])
            _for_row_chunks(seg_ref[N_GROUPS + k], copy, lambda cp: cp.start())

        @pl.when(i > 0)
        def _():
            wait_tile(1 - slot)
        for k in range(N_GROUPS):
            meta_ref[i * 2 * N_GROUPS + k] = off_ref[k]
            meta_ref[i * 2 * N_GROUPS + N_GROUPS + k] = seg_ref[N_GROUPS + k]
            off_ref[k] = off_ref[k] + seg_ref[N_GROUPS + k]
            prev_ref[k] = seg_ref[N_GROUPS + k]

    @pl.when((p == 1) & (i == ntiles - 1))
    def _():
        wait_tile((ntiles - 1) % 2)
        for k in range(N_GROUPS):
            bounds_ref[N_GROUPS + k] = off_ref[k]
            fill_zero(off_ref[k], bounds_ref[k + 1] if k + 1 < N_GROUPS else cap)


def _moe_sort(x1, route, tm):
    T = x1.shape[0]
    ntiles = T // tm
    rloc = _ceil_to(tm + N_GROUPS * MOE_ALIGN, LANES)
    cap = _ceil_to(T + MOE_ALIGN * N_GROUPS * ntiles, MOE_TR) + N_GROUPS * MOE_TR
    kern = functools.partial(_moe_sort_kernel, tm=tm, rloc=rloc, ntiles=ntiles, cap=cap)
    smem = pl.BlockSpec(memory_space=pltpu.SMEM)
    return pl.pallas_call(
        kern,
        grid=(2, ntiles),
        in_specs=[pl.BlockSpec((tm, D_MODEL), lambda p, i: (i * p, 0)),
                  pl.BlockSpec((tm, LANES), lambda p, i: (i, 0))],
        out_specs=[pl.BlockSpec(memory_space=pl.ANY), smem, smem],
        out_shape=[jax.ShapeDtypeStruct((cap, MOE_XW), BF16),
                   jax.ShapeDtypeStruct((ntiles * 2 * N_GROUPS,), I32),
                   jax.ShapeDtypeStruct((2 * N_GROUPS,), I32)],
        scratch_shapes=[pltpu.VMEM((2, rloc, MOE_XW), BF16), pltpu.VMEM((MOE_BIG, MOE_XW), BF16),
                        pltpu.VMEM((tm, tm), BF16),
                        pltpu.SMEM((N_GROUPS,), I32), pltpu.SMEM((2 * N_GROUPS,), I32), pltpu.SMEM((N_GROUPS,), I32),
                        pltpu.SemaphoreType.DMA((2,))],
        compiler_params=_params("arbitrary", "arbitrary"),
        name="moe_sort_dispatch",
    )(x1, route)


def _moe_tile_group(r, bounds_ref):
    row = r * MOE_TR
    g = 0
    for k in range(1, N_GROUPS):
        g = g + (row >= bounds_ref[k]).astype(I32)
    return g


def _moe_mlp_kernel(bounds_ref, xs_ref, w13_ref, w2_ref, ys_ref, acc_ref):
    r = pl.program_id(0)
    g = _moe_tile_group(r, bounds_ref)
    end = bounds_ref[N_GROUPS]
    for k in range(1, N_GROUPS):
        end = jnp.where(g == k, bounds_ref[N_GROUPS + k], end)
    used = r * MOE_TR < end

    @pl.when(used)
    def _():
        xb = xs_ref[:, :D_MODEL]
        terms = xs_ref[:, D_MODEL:].astype(F32)
        cw = terms + pltpu.roll(terms, LANES - EXPERTS_PER_GROUP, 1) + pltpu.roll(terms, LANES - 2 * EXPERTS_PER_GROUP, 1)
        for e in range(EXPERTS_PER_GROUP):
            h = _dot(xb, w13_ref[0, e])
            a = h[:, :EXPERT_FF]
            u = h[:, EXPERT_FF:]
            act = ((a * jax.nn.sigmoid(a)) * u * cw[:, e:e + 1]).astype(BF16)
            y = _dot(act, w2_ref[0, e * EXPERT_FF:(e + 1) * EXPERT_FF, :])
            if e == 0:
                acc_ref[...] = y
            else:
                acc_ref[...] += y
        ys_ref[...] = acc_ref[...].astype(BF16)

    @pl.when(jnp.logical_not(used))
    def _():
        ys_ref[...] = jnp.zeros(ys_ref.shape, BF16)


def _moe_mlp(bounds, xs, w13, w2):
    cap = xs.shape[0]
    grid_spec = pltpu.PrefetchScalarGridSpec(
        num_scalar_prefetch=1,
        grid=(cap // MOE_TR,),
        in_specs=[pl.BlockSpec((MOE_TR, MOE_XW), lambda r, b: (r, 0)),
                  pl.BlockSpec((1, EXPERTS_PER_GROUP, D_MODEL, 2 * EXPERT_FF),
                               lambda r, b: (_moe_tile_group(r, b), 0, 0, 0)),
                  pl.BlockSpec((1, EXPERTS_PER_GROUP * EXPERT_FF, D_MODEL),
                               lambda r, b: (_moe_tile_group(r, b), 0, 0))],
        out_specs=pl.BlockSpec((MOE_TR, D_MODEL), lambda r, b: (r, 0)),
        scratch_shapes=[pltpu.VMEM((MOE_TR, D_MODEL), F32)])
    return pl.pallas_call(
        _moe_mlp_kernel,
        grid_spec=grid_spec,
        out_shape=jax.ShapeDtypeStruct((cap, D_MODEL), BF16),
        compiler_params=_params("arbitrary"),
        name="moe_group_experts",
    )(bounds, xs, w13, w2)


def _moe_combine_kernel(meta_ref, x_ref, route_ref, ys_ref, g_ref, b_ref, o_ref, loc_ref, tri_ref, sem,
                        *, alpha, tm, rloc, ntiles):
    i = pl.program_id(0)

    def segments(t):
        offs = [meta_ref[t * 2 * N_GROUPS + k] for k in range(N_GROUPS)]
        rows = [meta_ref[t * 2 * N_GROUPS + N_GROUPS + k] for k in range(N_GROUPS)]
        segs = [0]
        for k in range(N_GROUPS):
            segs.append(segs[-1] + rows[k])
        return offs, rows, segs

    def for_chunks(t, fn):
        offs, rows, segs = segments(t)
        slot = t % 2
        for k in range(N_GROUPS):
            def copy(off, size, k=k):
                src = ys_ref.at[pl.ds(pl.multiple_of(offs[k] + off, MOE_ALIGN), size), :]
                dst = loc_ref.at[slot, pl.ds(pl.multiple_of(segs[k] + off, MOE_ALIGN), size), :]
                return pltpu.make_async_copy(src, dst, sem.at[slot])
            _for_row_chunks(rows[k], copy, fn)

    @pl.when(i == 0)
    def _():
        loc_ref[...] = jnp.zeros(loc_ref.shape, BF16)
        for_chunks(i, lambda cp: cp.start())

    @pl.when(i + 1 < ntiles)
    def _():
        for_chunks(i + 1, lambda cp: cp.start())

    segs = segments(i)[2]
    route = route_ref[...]
    lane = lax.broadcasted_iota(I32, (1, LANES), 1)
    onehot = ((route[:, EXPERTS_PER_GROUP:EXPERTS_PER_GROUP + 1] == lane.astype(F32)) & (lane < N_GROUPS)).astype(F32)
    @pl.when(i == 0)
    def _():
        tri_ref[...] = (lax.broadcasted_iota(I32, (tm, tm), 1) < lax.broadcasted_iota(I32, (tm, tm), 0)).astype(BF16)

    rank = _dot(tri_ref[...], onehot.astype(BF16))
    seg_start = jnp.zeros((1, LANES), F32)
    for k in range(N_GROUPS):
        seg_start = seg_start + jnp.where(lane == k, lax.convert_element_type(segs[k], F32), F32(0.0))
    dest = jnp.sum(onehot * (seg_start + rank), axis=1, keepdims=True).astype(I32)
    unperm = (lax.broadcasted_iota(I32, (1, rloc), 1) == dest).astype(BF16)
    for_chunks(i, lambda cp: cp.wait())
    valid = lax.broadcasted_iota(I32, (rloc, 1), 0) < segs[N_GROUPS]
    ysl = jnp.where(valid, loc_ref[i % 2], jnp.zeros((rloc, D_MODEL), BF16))
    o_ref[...] = _layer_norm(alpha * x_ref[...] + _dot(unperm, ysl), g_ref[...], b_ref[...])


def _moe_combine(meta, x1, route, ys, g, b, alpha, tm):
    T = x1.shape[0]
    rloc = _ceil_to(tm + N_GROUPS * MOE_ALIGN, LANES)
    kern = functools.partial(_moe_combine_kernel, alpha=alpha, tm=tm, rloc=rloc, ntiles=T // tm)
    grid_spec = pltpu.PrefetchScalarGridSpec(
        num_scalar_prefetch=1,
        grid=(T // tm,),
        in_specs=[pl.BlockSpec((tm, D_MODEL), lambda i, m: (i, 0)),
                  pl.BlockSpec((tm, LANES), lambda i, m: (i, 0)),
                  pl.BlockSpec(memory_space=pl.ANY),
                  pl.BlockSpec((1, D_MODEL), lambda i, m: (0, 0)),
                  pl.BlockSpec((1, D_MODEL), lambda i, m: (0, 0))],
        out_specs=pl.BlockSpec((tm, D_MODEL), lambda i, m: (i, 0)),
        scratch_shapes=[pltpu.VMEM((2, rloc, D_MODEL), BF16), pltpu.VMEM((tm, tm), BF16),
                        pltpu.SemaphoreType.DMA((2,))])
    return pl.pallas_call(
        kern,
        grid_spec=grid_spec,
        out_shape=jax.ShapeDtypeStruct((T, D_MODEL), F32),
        compiler_params=_params("arbitrary"),
        name="moe_combine_ln",
    )(meta, x1, route, ys, g, b)


def _pad_cols(a, width):
    return jnp.pad(a, ((0, 0), (0, width - a.shape[1])))


def _pad_heads(w, heads, dim, dim_pad):
    rows = w.shape[0]
    return jnp.pad(w.reshape(rows, heads, dim), ((0, 0), (0, 0), (0, dim_pad - dim))).reshape(rows, heads * dim_pad)


def _tile(n, pref):
    t = pref
    while n % t:
        t //= 2
    return t


def kernel(x, mem, positions, dsa_w_in, dsa_idx_k_g, dsa_idx_k_b, gla_w_in, gla_w_gate, gla_b_gate, gla_norm_g,
           w_mem_kv, w_out, ln1_g, ln1_b, ln2_g, ln2_b, moe_w_group, moe_b_group, moe_w_router, moe_b_router,
           moe_w13, moe_w2):
    B, S, D = x.shape
    T = B * S
    depth = w_out.shape[0]
    alpha = (2 * depth) ** 0.25
    tm = _tile(T, 512)
    tq = _tile(S, 512)
    tq_idx = _tile(S, 256)
    topk = min(DSA_MAX_TOPK, S // 4)

    inv = ROPE_THETA ** (-jnp.arange(0, HEAD_DIM, 2, dtype=F32) / HEAD_DIM)
    ang = positions.astype(F32).reshape(T, 1) * inv
    cos, sin = jnp.cos(ang), jnp.sin(ang)
    cosf = jnp.concatenate([cos, cos, cos, cos], axis=1)
    sinf = jnp.concatenate([-sin, sin, -sin, sin], axis=1)

    xc = x.reshape(T, D)
    ia = ib = 0
    for i in range(depth):
        if i % 2 == 0:
            w = dsa_w_in[ia]
            wq, wk, wv, wqi, wki, wwi, wqm = jnp.split(w, [768, 1536, 2304, 2816, 2880, 2888], axis=1)
            w_tok = jnp.concatenate([wk, wqm, _pad_cols(wki, LANES)], axis=1).astype(BF16)
            w_feat = jnp.concatenate([wq, wv, wqi, _pad_cols(wwi, 16)], axis=1).T.astype(BF16)
            lng = _pad_cols(dsa_idx_k_g[ia][None, :], LANES)
            lnb = _pad_cols(dsa_idx_k_b[ia][None, :], LANES)
            k, qm, ki, q_t, v_t, qi_t, wi_t = _proj_dsa(xc, w_tok, w_feat, cosf, sinf, cosf.T, sinf.T, lng, lnb,
                                                         tm, tq, tq_idx)
            bias = _idx_mask(qi_t, ki.reshape(B, S, LANES), wi_t, B, S, topk, tq_idx)
            seq = _dsa_attn(q_t, k.reshape(B, S, SEQ_WIDTH), v_t, bias, B, S, tq, 2).reshape(T, SEQ_WIDTH)
            wa = w_out[i][:SEQ_WIDTH].astype(BF16)
            ia += 1
        else:
            w = gla_w_in[ib]
            wq, wk, wv, wr_, wa1, wqm = jnp.split(w, [384, 768, 1536, 2304, 2320], axis=1)
            w_all = jnp.concatenate([
                _pad_heads(wq, GLA_HEADS, GLA_DK, GLA_DKP), _pad_heads(wk, GLA_HEADS, GLA_DK, GLA_DKP),
                _pad_heads(wv, GLA_HEADS, GLA_DV, GLA_DVP), _pad_heads(wr_, GLA_HEADS, GLA_DV, GLA_DVP),
                wqm, _pad_cols(wa1, LANES)], axis=1).astype(BF16)
            q, k, v, r, qm, a1 = _proj_gla(xc, w_all, tm)
            r3 = lambda a: a.reshape(B, S, a.shape[1])
            wg = _pad_heads(gla_w_gate[ib], GLA_HEADS, GLA_DK, GLA_DKP)
            wg = jnp.pad(wg, ((0, LANES - GLA_GATE_RANK), (0, 0)))
            wg = wg.reshape(LANES, GLA_HEADS, GLA_DKP).transpose(1, 0, 2)
            bg = _pad_heads(gla_b_gate[ib][None, :], GLA_HEADS, GLA_DK, GLA_DKP).reshape(GLA_HEADS, 1, GLA_DKP)
            ng = _pad_cols(gla_norm_g[ib][None, :], GLA_DVP)
            seq = _gla(r3(q), r3(k), r3(v), r3(r), r3(a1), wg, bg, ng).reshape(T, GLA_HEADS * GLA_DVP)
            wa = w_out[i][:SEQ_WIDTH].reshape(GLA_HEADS, GLA_DV, D)
            wa = jnp.pad(wa, ((0, 0), (0, GLA_DVP - GLA_DV), (0, 0))).reshape(GLA_HEADS * GLA_DVP, D).astype(BF16)
            ib += 1
        memo = _mem_attn(qm.reshape(B, S, MEM_WIDTH), mem, w_mem_kv[i].astype(BF16), tq).reshape(T, MEM_WIDTH)
        wb = w_out[i][SEQ_WIDTH:].astype(BF16)
        wr = jnp.concatenate([moe_w_group[i], moe_w_router[i].transpose(1, 0, 2).reshape(D, N_EXPERTS)], axis=1)
        wr = _pad_cols(wr, LANES)
        br = _pad_cols(jnp.concatenate([moe_b_group[i], moe_b_router[i].reshape(-1)])[None, :], LANES)
        x1, route = _mix_router(seq, memo, wa, wb, xc, ln1_g[i][None, :], ln1_b[i][None, :], wr, br, alpha,
                                _tile(T, 1024))
        w13g = moe_w13[i].astype(BF16)
        w2g = moe_w2[i].reshape(N_GROUPS, EXPERTS_PER_GROUP * EXPERT_FF, D).astype(BF16)
        tmoe = _tile(T, MOE_TM)
        xs, meta, bounds = _moe_sort(x1, route, tmoe)
        ys = _moe_mlp(bounds, xs, w13g, w2g)
        xc = _moe_combine(meta, x1, route, ys, ln2_g[i][None, :], ln2_b[i][None, :], alpha, tmoe)
    return xc.reshape(B, S, D)
```
